```python
import math
import jax, jax.numpy as jnp
from jax import lax
import numpy as np

D_MODEL = 1024
BATCH = 16
SEQ = 4096
DEPTH = 1
DEC_BATCH = 8
DEC_SEQ = 16
PAST_LEN = 2048

CHUNK = 64
EPS = 1e-6
N_MOD = 6
HG_HEADS = 8
HG_DK = 128
HG_DV = D_MODEL // HG_HEADS
HG_WIDTH = HG_HEADS * HG_DV
HG_KW = HG_HEADS * HG_DK
SSM_EXPAND = 2
SSM_D_INNER = SSM_EXPAND * D_MODEL
SSM_HEADDIM = 64
SSM_HEADS = SSM_D_INNER // SSM_HEADDIM
SSM_GROUPS = 4
SSM_HPG = SSM_HEADS // SSM_GROUPS
SSM_STATE = 128
SSM_CONV = 4
SSM_XBC = SSM_D_INNER + 2 * SSM_GROUPS * SSM_STATE
IN_SIZES = (HG_KW, HG_KW, HG_WIDTH, HG_WIDTH, SSM_D_INNER, SSM_XBC, SSM_HEADS, D_MODEL, D_MODEL)
IN_COLS = sum(IN_SIZES)
PEER_HEADS = 8
PEER_NKEYS = 128
PEER_EXPERTS = PEER_NKEYS * PEER_NKEYS
PEER_TOPK = 16
PEER_DK = 256
PEER_DK_HALF = PEER_DK // 2
PEER_BLOCK = 256

kernel_name = 'hgrn2_mamba2_peer_adaln_stream_step'


def _rms(x):
    xf = x.astype(jnp.float32)
    return xf * lax.rsqrt(jnp.mean(xf * xf, axis=-1, keepdims=True) + EPS)


def _to_blocks(a, blk):
    b, t = a.shape[0], a.shape[1]
    return jnp.moveaxis(a.reshape((b, t // blk, blk) + a.shape[2:]), 1, 0)


def _from_blocks(a):
    n, b, l = a.shape[0], a.shape[1], a.shape[2]
    return jnp.moveaxis(a, 0, 1).reshape((b, n * l) + a.shape[3:])


def _gla_chunked(q, k, v, logf, s0, blk):
    causal = jnp.tril(jnp.ones((blk, blk), dtype=bool))[None, :, :, None, None]

    def step(s, inp):
        qb, kb, vb, gb = inp
        g = jnp.cumsum(gb, axis=1)
        decay = jnp.exp(jnp.where(causal, g[:, :, None] - g[:, None, :], -jnp.inf))
        att = jnp.einsum('bihk,bjhk,bijhk->bhij', qb, kb, decay)
        o = (jnp.einsum('bhij,bjhv->bihv', att, vb)
             + jnp.einsum('bihk,bhkv->bihv', qb * jnp.exp(g), s))
        g_last = g[:, -1]
        s = (jnp.exp(g_last)[..., None] * s
             + jnp.einsum('bjhk,bjhv->bhkv', kb * jnp.exp(g_last[:, None] - g), vb))
        return s, o

    s_fin, o = lax.scan(step, s0, (_to_blocks(q, blk), _to_blocks(k, blk),
                                   _to_blocks(v, blk), _to_blocks(logf, blk)))
    return _from_blocks(o), s_fin


def _ssd_chunked(x, dt, a_neg, bm, cm, s0, blk):
    bn, t = x.shape[0], x.shape[1]
    xg = x.reshape(bn, t, SSM_GROUPS, SSM_HPG, SSM_HEADDIM)
    dtg = dt.reshape(bn, t, SSM_GROUPS, SSM_HPG)
    ag = (dt * a_neg).reshape(bn, t, SSM_GROUPS, SSM_HPG)
    h0 = s0.reshape(bn, SSM_GROUPS, SSM_HPG, SSM_HEADDIM, SSM_STATE)
    causal = jnp.tril(jnp.ones((blk, blk), dtype=bool))

    def step(h, inp):
        xb, dtb, ab, bb, cb = inp
        acum = jnp.cumsum(ab, axis=1)
        ac = jnp.transpose(acum, (0, 2, 3, 1))
        dtt = jnp.transpose(dtb, (0, 2, 3, 1))
        seg = jnp.exp(jnp.where(causal, ac[..., :, None] - ac[..., None, :], -jnp.inf))
        cbm = jnp.einsum('bign,bjgn->bgij', cb, bb)
        w = cbm[:, :, None] * seg * dtt[..., None, :]
        y = (jnp.einsum('bgrij,bjgrp->bigrp', w, xb)
             + jnp.einsum('bign,bgrpn->bigrp', cb, h) * jnp.exp(acum)[..., None])
        a_last = ac[..., -1]
        wd = jnp.exp(a_last[..., None] - ac) * dtt
        h = (jnp.exp(a_last)[..., None, None] * h
             + jnp.einsum('bjgn,bgrj,bjgrp->bgrpn', bb, wd, xb))
        return h, y

    h_fin, y = lax.scan(step, h0, (_to_blocks(xg, blk), _to_blocks(dtg, blk), _to_blocks(ag, blk),
                                   _to_blocks(bm, blk), _to_blocks(cm, blk)))
    y = _from_blocks(y).reshape(bn, t, SSM_HEADS, SSM_HEADDIM)
    return y, h_fin.reshape(bn, SSM_HEADS, SSM_HEADDIM, SSM_STATE)


def _token_mixer(h, s_hg, s_ssm, conv_buf, lb, w_in, hg_norm_w, conv_w, conv_b, dt_bias, a_log,
                 ssm_d, ssm_norm_w, w_branch_a, w_branch_b, w_out):
    f32 = jnp.float32
    bn, t = h.shape[0], h.shape[1]
    blk = min(CHUNK, t)
    proj = h @ w_in
    cuts = []
    acc = 0
    for s in IN_SIZES[:-1]:
        acc += s
        cuts.append(acc)
    q_r, f_r, i_r, g_r, z, xbc, dt_r, ga_r, gb_r = jnp.split(proj, cuts, axis=-1)

    q = jax.nn.silu(q_r.astype(f32)).reshape(bn, t, HG_HEADS, HG_DK)
    ff = f_r.astype(f32)
    logf = jnp.log(lb + (1.0 - lb) * jax.nn.sigmoid(ff)).reshape(bn, t, HG_HEADS, HG_DK)
    k = ((1.0 - lb) * jax.nn.sigmoid(-ff)).reshape(bn, t, HG_HEADS, HG_DK)
    v = i_r.astype(f32).reshape(bn, t, HG_HEADS, HG_DV)
    o, s_hg_new = _gla_chunked(q, k, v, logf, s_hg.astype(f32), blk)
    o = (_rms(o) * hg_norm_w.astype(f32).reshape(HG_HEADS, HG_DV)).reshape(bn, t, HG_WIDTH)
    o = (o * jax.nn.silu(g_r.astype(f32))).astype(h.dtype)
    pa = o @ w_branch_a

    xp = jnp.concatenate([conv_buf.astype(xbc.dtype), xbc], axis=1)
    conv_new = xp[:, xp.shape[1] - (SSM_CONV - 1):]
    xc = conv_b + xp[:, 0:t] * conv_w[0]
    for j in range(1, SSM_CONV):
        xc = xc + xp[:, j:j + t] * conv_w[j]
    xc = jax.nn.silu(xc.astype(f32))
    xs, bm, cm = jnp.split(xc, [SSM_D_INNER, SSM_D_INNER + SSM_GROUPS * SSM_STATE], axis=-1)
    xs = xs.reshape(bn, t, SSM_HEADS, SSM_HEADDIM)
    bm = bm.reshape(bn, t, SSM_GROUPS, SSM_STATE)
    cm = cm.reshape(bn, t, SSM_GROUPS, SSM_STATE)
    dt = jax.nn.softplus(dt_r.astype(f32) + dt_bias.astype(f32))
    a_neg = -jnp.exp(a_log.astype(f32))
    y, s_ssm_new = _ssd_chunked(xs, dt, a_neg, bm, cm, s_ssm.astype(f32), blk)
    y = y + ssm_d.astype(f32)[:, None] * xs
    y = y.reshape(bn, t, SSM_D_INNER) * jax.nn.silu(z.astype(f32))
    y = _rms(y.reshape(bn, t, SSM_GROUPS, SSM_D_INNER // SSM_GROUPS)).reshape(bn, t, SSM_D_INNER)
    y = (y * ssm_norm_w.astype(f32)).astype(h.dtype)
    pb = y @ w_branch_b

    merged = jax.nn.sigmoid(ga_r) * pa + jax.nn.sigmoid(gb_r) * pb
    return (merged @ w_out, s_hg_new.astype(s_hg.dtype), s_ssm_new.astype(s_ssm.dtype),
            conv_new.astype(conv_buf.dtype))


def _peer(h, wq, keys1, keys2, u, v):
    f32 = jnp.float32
    bn, t, d = h.shape
    n = bn * t
    blk = min(PEER_BLOCK, n)
    pad = (-n) % blk
    tok = jnp.pad(h.reshape(n, d), ((0, pad), (0, 0))).reshape(-1, blk, d)
    k1 = keys1.astype(f32)
    k2 = keys2.astype(f32)

    def one(tb):
        q = (tb @ wq).astype(f32).reshape(blk, PEER_HEADS, 2, PEER_DK_HALF)
        s1 = jnp.einsum('thd,hkd->thk', q[:, :, 0], k1)
        s2 = jnp.einsum('thd,hkd->thk', q[:, :, 1], k2)
        v1, i1 = lax.top_k(s1, PEER_TOPK)
        v2, i2 = lax.top_k(s2, PEER_TOPK)
        cand = (v1[..., :, None] + v2[..., None, :]).reshape(blk, PEER_HEADS, PEER_TOPK * PEER_TOPK)
        cidx = (i1[..., :, None] * PEER_NKEYS + i2[..., None, :]).reshape(blk, PEER_HEADS, PEER_TOPK * PEER_TOPK)
        sv, si = lax.top_k(cand, PEER_TOPK)
        eidx = jnp.take_along_axis(cidx, si, axis=-1)
        gw = jax.nn.softmax(sv, axis=-1)
        ue = jnp.take(u, eidx, axis=0)
        act = jax.nn.gelu(jnp.einsum('td,thed->the', tb, ue).astype(f32), approximate=False)
        ve = jnp.take(v, eidx, axis=0)
        return jnp.einsum('the,thed->td', (gw * act).astype(tb.dtype), ve)

    out = lax.map(one, tok).reshape(-1, d)[:n]
    return out.reshape(bn, t, d)


def _layer(x, c, s_hg, s_ssm, conv_buf, lb, w_ada, b_ada, norm1_w, w_in, hg_norm_w, conv_w, conv_b,
           dt_bias, a_log, ssm_d, ssm_norm_w, w_branch_a, w_branch_b, w_out, norm2_w,
           peer_wq, peer_keys1, peer_keys2, peer_u, peer_v):
    mod = (jax.nn.silu(c) @ w_ada + b_ada)[:, None, :]
    sh1, sc1, g1, sh2, sc2, g2 = jnp.split(mod, N_MOD, axis=-1)
    h = (_rms(x) * norm1_w * (1.0 + sc1) + sh1).astype(x.dtype)
    mix, s_hg_new, s_ssm_new, conv_new = _token_mixer(
        h, s_hg, s_ssm, conv_buf, lb, w_in, hg_norm_w, conv_w, conv_b, dt_bias, a_log,
        ssm_d, ssm_norm_w, w_branch_a, w_branch_b, w_out)
    x = x + g1 * mix
    h2 = (_rms(x) * norm2_w * (1.0 + sc2) + sh2).astype(x.dtype)
    x = x + g2 * _peer(h2, peer_wq, peer_keys1, peer_keys2, peer_u, peer_v)
    return x, s_hg_new, s_ssm_new, conv_new


def setup_inputs(seed: int = 0) -> dict:
    key = jax.random.key(seed)
    ks = jax.random.split(key, 32)
    nrm = jax.random.normal
    f32 = jnp.float32
    dt0 = jnp.exp(jax.random.uniform(ks[14], (DEPTH, SSM_HEADS), minval=math.log(1e-3), maxval=math.log(1e-1)))
    return {
        'x_prompt': nrm(ks[0], (BATCH, SEQ, D_MODEL), f32),
        'x_sample': nrm(ks[1], (DEC_BATCH, DEC_SEQ, D_MODEL), f32),
        'c_prompt': nrm(ks[2], (BATCH, D_MODEL), f32),
        'c_sample': nrm(ks[3], (DEC_BATCH, D_MODEL), f32),
        'state_hgrn': 0.5 * nrm(ks[4], (DEPTH, DEC_BATCH, HG_HEADS, HG_DK, HG_DV), f32),
        'state_ssm': 0.1 * nrm(ks[5], (DEPTH, DEC_BATCH, SSM_HEADS, SSM_HEADDIM, SSM_STATE), f32),
        'state_conv': nrm(ks[6], (DEPTH, DEC_BATCH, SSM_CONV - 1, SSM_XBC), f32),
        'w_ada': 0.3 * D_MODEL ** -0.5 * nrm(ks[7], (DEPTH, D_MODEL, N_MOD * D_MODEL), f32),
        'b_ada': 0.02 * nrm(ks[8], (DEPTH, N_MOD * D_MODEL), f32),
        'norm1_w': 1.0 + 0.02 * nrm(ks[9], (DEPTH, D_MODEL), f32),
        'w_in': D_MODEL ** -0.5 * nrm(ks[10], (DEPTH, D_MODEL, IN_COLS), f32),
        'hgrn_lower_bounds': 0.1 * nrm(ks[11], (DEPTH + 1, HG_KW), f32),
        'hgrn_norm_w': 1.0 + 0.02 * nrm(ks[12], (DEPTH, HG_WIDTH), f32),
        'conv_w': SSM_CONV ** -0.5 * nrm(ks[13], (DEPTH, SSM_CONV, SSM_XBC), f32),
        'conv_b': 0.02 * nrm(ks[15], (DEPTH, SSM_XBC), f32),
        'dt_bias': dt0 + jnp.log(-jnp.expm1(-dt0)),
        'a_log': jnp.log(jax.random.uniform(ks[16], (DEPTH, SSM_HEADS), minval=1.0, maxval=16.0)),
        'ssm_d': 1.0 + 0.02 * nrm(ks[17], (DEPTH, SSM_HEADS), f32),
        'ssm_norm_w': 1.0 + 0.02 * nrm(ks[18], (DEPTH, SSM_D_INNER), f32),
        'w_branch_a': HG_WIDTH ** -0.5 * nrm(ks[19], (DEPTH, HG_WIDTH, D_MODEL), f32),
        'w_branch_b': SSM_D_INNER ** -0.5 * nrm(ks[20], (DEPTH, SSM_D_INNER, D_MODEL), f32),
        'w_out': D_MODEL ** -0.5 * nrm(ks[21], (DEPTH, D_MODEL, D_MODEL), f32),
        'norm2_w': 1.0 + 0.02 * nrm(ks[22], (DEPTH, D_MODEL), f32),
        'peer_wq': D_MODEL ** -0.5 * nrm(ks[23], (DEPTH, D_MODEL, PEER_HEADS * PEER_DK), f32),
        'peer_keys1': PEER_DK_HALF ** -0.5 * nrm(ks[24], (DEPTH, PEER_HEADS, PEER_NKEYS, PEER_DK_HALF), f32),
        'peer_keys2': PEER_DK_HALF ** -0.5 * nrm(ks[25], (DEPTH, PEER_HEADS, PEER_NKEYS, PEER_DK_HALF), f32),
        'peer_u': D_MODEL ** -0.5 * nrm(ks[26], (DEPTH, PEER_EXPERTS, D_MODEL), f32),
        'peer_v': PEER_HEADS ** -0.5 * nrm(ks[27], (DEPTH, PEER_EXPERTS, D_MODEL), f32),
        'final_norm_w': 1.0 + 0.02 * nrm(ks[28], (D_MODEL,), f32),
    }


def reference(x_prompt, x_sample, c_prompt, c_sample, state_hgrn, state_ssm, state_conv,
              w_ada, b_ada, norm1_w, w_in, hgrn_lower_bounds, hgrn_norm_w, conv_w, conv_b,
              dt_bias, a_log, ssm_d, ssm_norm_w, w_branch_a, w_branch_b, w_out, norm2_w,
              peer_wq, peer_keys1, peer_keys2, peer_u, peer_v, final_norm_w):
    lb_all = jnp.cumsum(jax.nn.softmax(hgrn_lower_bounds.astype(jnp.float32), axis=0), axis=0)
    xp, xs = x_prompt, x_sample
    bp = x_prompt.shape[0]
    hg_p, ssm_p, conv_p, hg_s, ssm_s, conv_s = [], [], [], [], [], []
    for l in range(DEPTH):
        wl = (lb_all[l], w_ada[l], b_ada[l], norm1_w[l], w_in[l], hgrn_norm_w[l], conv_w[l], conv_b[l],
              dt_bias[l], a_log[l], ssm_d[l], ssm_norm_w[l], w_branch_a[l], w_branch_b[l], w_out[l],
              norm2_w[l], peer_wq[l], peer_keys1[l], peer_keys2[l], peer_u[l], peer_v[l])
        z_hg = jnp.zeros((bp, HG_HEADS, HG_DK, HG_DV), x_prompt.dtype)
        z_ssm = jnp.zeros((bp, SSM_HEADS, SSM_HEADDIM, SSM_STATE), x_prompt.dtype)
        z_conv = jnp.zeros((bp, SSM_CONV - 1, SSM_XBC), x_prompt.dtype)
        xp, a1, a2, a3 = _layer(xp, c_prompt, z_hg, z_ssm, z_conv, *wl)
        xs, b1, b2, b3 = _layer(xs, c_sample, state_hgrn[l], state_ssm[l], state_conv[l], *wl)
        hg_p.append(a1); ssm_p.append(a2); conv_p.append(a3)
        hg_s.append(b1); ssm_s.append(b2); conv_s.append(b3)
    y_prompt = (_rms(xp) * final_norm_w).astype(x_prompt.dtype)
    y_sample = (_rms(xs) * final_norm_w).astype(x_sample.dtype)
    hgrn_prompt = jnp.stack(hg_p)
    ssm_prompt = jnp.stack(ssm_p)
    conv_prompt = jnp.stack(conv_p)
    hgrn_sample = jnp.stack(hg_s)
    ssm_sample = jnp.stack(ssm_s)
    conv_sample = jnp.stack(conv_s)
    return (y_prompt, y_sample, hgrn_prompt, ssm_prompt, conv_prompt, hgrn_sample, ssm_sample, conv_sample)
```

```python
import functools

import jax
import jax.numpy as jnp
from jax import lax
from jax.experimental import pallas as pl
from jax.experimental.pallas import tpu as pltpu

F32 = jnp.float32
BF16 = jnp.bfloat16
I32 = jnp.int32

EPS = 1e-6
D = 1024
N_MOD = 6
HG_HEADS = 8
HG_DK = 128
SSM_HEADS = 32
SSM_P = 64
SSM_G = 4
SSM_R = SSM_HEADS // SSM_G
SSM_N = 128
SSM_INNER = SSM_HEADS * SSM_P
SSM_GW = SSM_INNER // SSM_G
SSM_XBC = SSM_INNER + 2 * SSM_G * SSM_N
PEER_HEADS = 8
PEER_NK = 128
PEER_TOPK = 16
PEER_HK = PEER_HEADS * PEER_TOPK
PEER_DK = 256

LANES = 128
GLA_CHUNK = 64
SSD_CHUNK = 256
VMEM_LIMIT = 56 * 1024 * 1024

COL_HG = 0
COL_Z = 4096
COL_XS = 6144
COL_B = COL_XS + SSM_INNER
COL_C = COL_B + SSM_G * SSM_N
COL_GA = 9216
COL_GB = 10240
COL_DT = 11264
PROJ_COLS = 11520
PROJ_TN = 1280


def _cparams(sem):
    return pltpu.CompilerParams(dimension_semantics=sem, vmem_limit_bytes=VMEM_LIMIT)


def _silu(x):
    return x * jax.nn.sigmoid(x)


def _split3(x):
    hi = x.astype(BF16)
    r = x - hi.astype(F32)
    mid = r.astype(BF16)
    lo = (r - mid.astype(F32)).astype(BF16)
    return hi, mid, lo


def _dot(a, b):
    return jnp.dot(a, b, preferred_element_type=F32)


def _dot_nt(a, b):
    return lax.dot_general(a, b, (((1,), (1,)), ((), ())), preferred_element_type=F32)


def _dot_tn(a, b):
    return lax.dot_general(a, b, (((0,), (0,)), ((), ())), preferred_element_type=F32)


def _dot3_left(m_bf16, x):
    hi, mid, lo = _split3(x)
    return _dot(m_bf16, hi) + _dot(m_bf16, mid) + _dot(m_bf16, lo)


def _dot3_right(x, m_bf16):
    hi, mid, lo = _split3(x)
    return _dot(hi, m_bf16) + _dot(mid, m_bf16) + _dot(lo, m_bf16)


def _tril(n):
    r = lax.broadcasted_iota(I32, (n, n), 0)
    c = lax.broadcasted_iota(I32, (n, n), 1)
    return r >= c


def _adaln_kernel(c_ref, w_ref, b_ref, o_ref):
    c = c_ref[...]
    o_ref[...] = _dot(_silu(c).astype(BF16), w_ref[...].astype(BF16)) + b_ref[...]


def _adaln(c, w_ada, b_ada):
    nb = c.shape[0]
    ncol = w_ada.shape[1]
    return pl.pallas_call(
        _adaln_kernel,
        grid=(ncol // D,),
        in_specs=[pl.BlockSpec((nb, D), lambda j: (0, 0)),
                  pl.BlockSpec((D, D), lambda j: (0, j)),
                  pl.BlockSpec((1, D), lambda j: (0, j))],
        out_specs=pl.BlockSpec((nb, D), lambda j: (0, j)),
        out_shape=jax.ShapeDtypeStruct((nb, ncol), F32),
        compiler_params=_cparams(("arbitrary",)),
        name="adaln",
    )(c, w_ada, b_ada.reshape(1, ncol))


def _inproj_kernel(x_ref, sh_ref, sc_ref, nw_ref, w_ref, o_ref, h_ref):
    bb, tt, _ = x_ref.shape

    @pl.when(pl.program_id(2) == 0)
    def _():
        x = x_ref[...]
        xn = x * lax.rsqrt(jnp.mean(x * x, axis=-1, keepdims=True) + EPS)
        h = xn * nw_ref[...] * (1.0 + sc_ref[...]) + sh_ref[...]
        h_ref[...] = h.reshape(bb * tt, D).astype(BF16)

    o_ref[...] = _dot(h_ref[...], w_ref[...]).reshape(bb, tt, -1)


def _inproj(x, mod3, norm_w, w_pad, bb, tt):
    b, t, _ = x.shape
    return pl.pallas_call(
        _inproj_kernel,
        grid=(b // bb, t // tt, PROJ_COLS // PROJ_TN),
        in_specs=[pl.BlockSpec((bb, tt, D), lambda i, j, n: (i, j, 0)),
                  pl.BlockSpec((bb, 1, D), lambda i, j, n: (i, 0, 0)),
                  pl.BlockSpec((bb, 1, D), lambda i, j, n: (i, 0, 1)),
                  pl.BlockSpec((1, D), lambda i, j, n: (0, 0)),
                  pl.BlockSpec((D, PROJ_TN), lambda i, j, n: (0, n))],
        out_specs=pl.BlockSpec((bb, tt, PROJ_TN), lambda i, j, n: (i, j, n)),
        out_shape=jax.ShapeDtypeStruct((b, t, PROJ_COLS), F32),
        scratch_shapes=[pltpu.VMEM((bb * tt, D), BF16)],
        compiler_params=_cparams(("arbitrary", "arbitrary", "arbitrary")),
        name="inproj",
    )(x, mod3, mod3, norm_w.reshape(1, D), w_pad)


def _gla_kernel(q_ref, f_ref, i_ref, g_ref, lb_ref, nw_ref, s0_ref, o_ref, s_out_ref, s_ref, *, chunk):
    tt = q_ref.shape[1]
    nchunk = tt // chunk
    t_idx = pl.program_id(2)

    @pl.when(t_idx == 0)
    def _():
        s_ref[...] = s0_ref[0, 0].T

    lb = lb_ref[...]
    tril = _tril(chunk)
    tril_bf = tril.astype(BF16)
    mid = chunk // 2

    def body(c, carry):
        rows = pl.ds(pl.multiple_of(c * chunk, chunk), chunk)
        q = _silu(q_ref[0, rows, :])
        ff = f_ref[0, rows, :]
        sig = jax.nn.sigmoid(ff)
        logf = jnp.log(lb + (1.0 - lb) * sig)
        k = (1.0 - lb) * jax.nn.sigmoid(-ff)
        v = i_ref[0, rows, :].astype(BF16)
        g = _dot3_left(tril_bf, logf)
        g_mid = g[mid:mid + 1, :]
        g_last = g[chunk - 1:chunk, :]
        qs = (q * jnp.exp(g - g_mid)).astype(BF16)
        ks = (k * jnp.exp(g_mid - g)).astype(BF16)
        att = jnp.where(tril, _dot_nt(qs, ks), 0.0).astype(BF16)
        st_old = s_ref[...]
        o = _dot(att, v) + _dot_nt((q * jnp.exp(g)).astype(BF16), st_old.astype(BF16))
        kd = (k * jnp.exp(g_last - g)).astype(BF16)
        s_ref[...] = jnp.exp(g_last) * st_old + _dot_tn(v, kd)
        on = o * lax.rsqrt(jnp.mean(o * o, axis=-1, keepdims=True) + EPS) * nw_ref[...]
        o_ref[0, rows, :] = (on * _silu(g_ref[0, rows, :])).astype(BF16)
        return carry

    lax.fori_loop(0, nchunk, body, 0)

    @pl.when(t_idx == pl.num_programs(2) - 1)
    def _():
        s_out_ref[0, 0] = s_ref[...].T


def _gla(proj, lb, hg_norm_w, s0, tt, chunk):
    b, t, _ = proj.shape
    col = lambda base: (lambda i, h, j: (i, j, base // LANES + h))
    return pl.pallas_call(
        functools.partial(_gla_kernel, chunk=chunk),
        grid=(b, HG_HEADS, t // tt),
        in_specs=[pl.BlockSpec((1, tt, LANES), col(COL_HG)),
                  pl.BlockSpec((1, tt, LANES), col(COL_HG + 1024)),
                  pl.BlockSpec((1, tt, LANES), col(COL_HG + 2048)),
                  pl.BlockSpec((1, tt, LANES), col(COL_HG + 3072)),
                  pl.BlockSpec((1, LANES), lambda i, h, j: (0, h)),
                  pl.BlockSpec((1, LANES), lambda i, h, j: (0, h)),
                  pl.BlockSpec((1, 1, HG_DK, LANES), lambda i, h, j: (i, h, 0, 0))],
        out_specs=[pl.BlockSpec((1, tt, LANES), lambda i, h, j: (i, j, h)),
                   pl.BlockSpec((1, 1, HG_DK, LANES), lambda i, h, j: (i, h, 0, 0))],
        out_shape=[jax.ShapeDtypeStruct((b, t, D), BF16),
                   jax.ShapeDtypeStruct((b, HG_HEADS, HG_DK, LANES), F32)],
        scratch_shapes=[pltpu.VMEM((HG_DK, LANES), F32)],
        compiler_params=_cparams(("arbitrary", "arbitrary", "arbitrary")),
        name="gla",
    )(proj, proj, proj, proj, lb.reshape(1, D), hg_norm_w.reshape(1, D), s0)


def _ssd_kernel(xs_ref, b_ref, c_ref, z_ref, dt_ref, cwx_ref, cwb_ref, cwc_ref, cbx_ref, cbb_ref, cbc_ref,
                bufx_ref, bufb_ref, bufc_ref, dtb_ref, alog_ref, dpar_ref, sel_ref, exp_ref, nw_ref, h0_ref,
                y_ref, h_out_ref, h_ref, xpx_ref, xpb_ref, xpc_ref):
    L = xs_ref.shape[1]
    t_idx = pl.program_id(2)

    @pl.when(t_idx == 0)
    def _():
        h_ref[...] = h0_ref[0, 0]
        xpx_ref[0:8, :] = bufx_ref[0]
        xpb_ref[0:8, :] = bufb_ref[0]
        xpc_ref[0:8, :] = bufc_ref[0]

    def conv(xp_ref, raw, w_ref, bias_ref):
        xp_ref[8:8 + L, :] = raw
        acc = bias_ref[...] + xp_ref[5:5 + L, :] * w_ref[0:1, :]
        for j in range(1, 4):
            acc = acc + xp_ref[5 + j:5 + j + L, :] * w_ref[j:j + 1, :]
        xp_ref[0:8, :] = xp_ref[L:L + 8, :]
        return _silu(acc)

    xs = conv(xpx_ref, xs_ref[0], cwx_ref, cbx_ref)
    bm = conv(xpb_ref, b_ref[0], cwb_ref, cbb_ref)
    cm = conv(xpc_ref, c_ref[0], cwc_ref, cbc_ref)
    bm_bf = bm.astype(BF16)
    cm_bf = cm.astype(BF16)

    sel = sel_ref[0]
    expand = exp_ref[...]
    dt_raw = _dot3_right(dt_ref[0], sel)
    dt = jax.nn.softplus(dt_raw + dtb_ref[0])
    a_neg = -jnp.exp(alog_ref[0])
    tril = _tril(L)
    ac = _dot3_left(tril.astype(BF16), dt * a_neg)
    a_last = ac[L - 1:L, :]

    def expand2(v):
        hi = v.astype(BF16)
        lo = (v - hi.astype(F32)).astype(BF16)
        return _dot(hi, expand) + _dot(lo, expand)

    dt_x = expand2(dt)
    eac_x = expand2(jnp.exp(ac))
    wd_x = expand2(jnp.exp(a_last - ac) * dt)
    row8 = lax.broadcasted_iota(I32, (8, LANES), 0)
    tail_x = expand2(jnp.where(row8 == 0, jnp.exp(a_last), jnp.where(row8 == 1, dpar_ref[0], 0.0)))
    decay_x = tail_x[0:1, :]
    d_x = tail_x[1:2, :]

    xdt = (xs * dt_x).astype(BF16)
    cbm = jnp.where(tril, _dot_nt(cm_bf, bm_bf), 0.0)
    ac_t = ac.T
    lane = lax.broadcasted_iota(I32, (L, LANES), 1)
    zero = jnp.zeros((), BF16)
    ys = []
    for pair in range(SSM_R // 2):
        xq = xdt[:, pair * LANES:(pair + 1) * LANES]
        acc = None
        for half in range(2):
            r = 2 * pair + half
            seg = jnp.exp(jnp.minimum(ac[:, r:r + 1] - ac_t[r:r + 1, :], 0.0))
            w = (cbm * seg).astype(BF16)
            keep = (lane < SSM_P) if half == 0 else (lane >= SSM_P)
            part = _dot(w, jnp.where(keep, xq, zero))
            acc = part if acc is None else acc + part
        ys.append(acc)
    h_old = h_ref[...]
    y = jnp.concatenate(ys, axis=1) + _dot(cm_bf, h_old.astype(BF16)) * eac_x
    xw = (xs * wd_x).astype(BF16)
    h_ref[...] = decay_x * h_old + _dot_tn(bm_bf, xw)

    y = (y + d_x * xs) * _silu(z_ref[0])
    y = y * lax.rsqrt(jnp.mean(y * y, axis=-1, keepdims=True) + EPS) * nw_ref[...]
    y_ref[0] = y.astype(BF16)

    @pl.when(t_idx == pl.num_programs(2) - 1)
    def _():
        h_out_ref[0, 0] = h_ref[...]


def _ssd(proj, conv_w, conv_b, conv_buf8, dt_bias, a_log, ssm_d, ssm_norm_w, h0t, chunk):
    b, t, _ = proj.shape
    L = chunk
    grp = lambda v: jnp.pad(v.reshape(SSM_G, 1, SSM_R), ((0, 0), (0, 0), (0, LANES - SSM_R)))
    hh = jnp.arange(LANES)
    sel = (hh[None, :, None] == (jnp.arange(SSM_G)[:, None, None] * SSM_R + hh[None, None, :])) & (hh[None, None, :] < SSM_R)
    expand = (hh[:, None] == (jnp.arange(SSM_GW)[None, :] // SSM_P))
    cb2 = conv_b.reshape(1, SSM_XBC)
    nb_x, nb_b, nb_c = 0, SSM_INNER // LANES, (SSM_INNER + SSM_G * SSM_N) // LANES
    pcol = lambda base, width: (lambda i, g, j: (i, j, base // width + g))
    return pl.pallas_call(
        _ssd_kernel,
        grid=(b, SSM_G, t // L),
        in_specs=[pl.BlockSpec((1, L, SSM_GW), pcol(COL_XS, SSM_GW)),
                  pl.BlockSpec((1, L, SSM_N), pcol(COL_B, SSM_N)),
                  pl.BlockSpec((1, L, SSM_N), pcol(COL_C, SSM_N)),
                  pl.BlockSpec((1, L, SSM_GW), pcol(COL_Z, SSM_GW)),
                  pl.BlockSpec((1, L, LANES), lambda i, g, j: (i, j, COL_DT // LANES)),
                  pl.BlockSpec((4, SSM_GW), lambda i, g, j: (0, g)),
                  pl.BlockSpec((4, SSM_N), lambda i, g, j: (0, nb_b + g)),
                  pl.BlockSpec((4, SSM_N), lambda i, g, j: (0, nb_c + g)),
                  pl.BlockSpec((1, SSM_GW), lambda i, g, j: (0, g)),
                  pl.BlockSpec((1, SSM_N), lambda i, g, j: (0, nb_b + g)),
                  pl.BlockSpec((1, SSM_N), lambda i, g, j: (0, nb_c + g)),
                  pl.BlockSpec((1, 8, SSM_GW), lambda i, g, j: (i, 0, g)),
                  pl.BlockSpec((1, 8, SSM_N), lambda i, g, j: (i, 0, nb_b + g)),
                  pl.BlockSpec((1, 8, SSM_N), lambda i, g, j: (i, 0, nb_c + g)),
                  pl.BlockSpec((1, 1, LANES), lambda i, g, j: (g, 0, 0)),
                  pl.BlockSpec((1, 1, LANES), lambda i, g, j: (g, 0, 0)),
                  pl.BlockSpec((1, 1, LANES), lambda i, g, j: (g, 0, 0)),
                  pl.BlockSpec((1, LANES, LANES), lambda i, g, j: (g, 0, 0)),
                  pl.BlockSpec((LANES, SSM_GW), lambda i, g, j: (0, 0)),
                  pl.BlockSpec((1, SSM_GW), lambda i, g, j: (0, g)),
                  pl.BlockSpec((1, 1, SSM_N, SSM_GW), lambda i, g, j: (i, g, 0, 0))],
        out_specs=[pl.BlockSpec((1, L, SSM_GW), lambda i, g, j: (i, j, g)),
                   pl.BlockSpec((1, 1, SSM_N, SSM_GW), lambda i, g, j: (i, g, 0, 0))],
        out_shape=[jax.ShapeDtypeStruct((b, t, SSM_INNER), BF16),
                   jax.ShapeDtypeStruct((b, SSM_G, SSM_N, SSM_GW), F32)],
        scratch_shapes=[pltpu.VMEM((SSM_N, SSM_GW), F32),
                        pltpu.VMEM((L + 8, SSM_GW), F32),
                        pltpu.VMEM((L + 8, SSM_N), F32),
                        pltpu.VMEM((L + 8, SSM_N), F32)],
        compiler_params=_cparams(("arbitrary", "arbitrary", "arbitrary")),
        name="ssd",
    )(proj, proj, proj, proj, proj, conv_w, conv_w, conv_w, cb2, cb2, cb2, conv_buf8, conv_buf8, conv_buf8,
      grp(dt_bias), grp(a_log), grp(ssm_d), sel.astype(BF16), expand.astype(BF16),
      ssm_norm_w.reshape(1, SSM_INNER), h0t)


def _merge_kernel(o_ref, y_ref, ga_ref, gb_ref, x_ref, g1_ref, sh2_ref, sc2_ref, nw_ref, wa_ref, wb_ref, wo_ref,
                  x1_ref, h2_ref):
    bb, tt, _ = x_ref.shape
    n = bb * tt
    pa = _dot(o_ref[...].reshape(n, D), wa_ref[...])
    pb = _dot(y_ref[...].reshape(n, SSM_INNER), wb_ref[...])
    merged = (jax.nn.sigmoid(ga_ref[...].reshape(n, D)) * pa + jax.nn.sigmoid(gb_ref[...].reshape(n, D)) * pb)
    mix = _dot(merged.astype(BF16), wo_ref[...]).reshape(bb, tt, D)
    x1 = x_ref[...] + g1_ref[...] * mix
    x1_ref[...] = x1
    xn = x1 * lax.rsqrt(jnp.mean(x1 * x1, axis=-1, keepdims=True) + EPS)
    h2_ref[...] = (xn * nw_ref[...] * (1.0 + sc2_ref[...]) + sh2_ref[...]).astype(BF16)


def _merge(o_g, y_g, proj, x, mod3, norm2_w, wa, wb, wo, bb, tt):
    b, t, _ = x.shape
    row = lambda i, j: (i, j, 0)
    modc = lambda c: (lambda i, j: (i, 0, c))
    const = lambda i, j: (0, 0)
    return pl.pallas_call(
        _merge_kernel,
        grid=(b // bb, t // tt),
        in_specs=[pl.BlockSpec((bb, tt, D), row),
                  pl.BlockSpec((bb, tt, SSM_INNER), row),
                  pl.BlockSpec((bb, tt, D), lambda i, j: (i, j, COL_GA // D)),
                  pl.BlockSpec((bb, tt, D), lambda i, j: (i, j, COL_GB // D)),
                  pl.BlockSpec((bb, tt, D), row),
                  pl.BlockSpec((bb, 1, D), modc(2)),
                  pl.BlockSpec((bb, 1, D), modc(3)),
                  pl.BlockSpec((bb, 1, D), modc(4)),
                  pl.BlockSpec((1, D), const),
                  pl.BlockSpec((D, D), const),
                  pl.BlockSpec((SSM_INNER, D), const),
                  pl.BlockSpec((D, D), const)],
        out_specs=[pl.BlockSpec((bb, tt, D), row), pl.BlockSpec((bb, tt, D), row)],
        out_shape=[jax.ShapeDtypeStruct((b, t, D), F32), jax.ShapeDtypeStruct((b, t, D), BF16)],
        compiler_params=_cparams(("arbitrary", "arbitrary")),
        name="merge",
    )(o_g, y_g, proj, proj, x, mod3, mod3, mod3, norm2_w.reshape(1, D), wa, wb, wo)


def _topk_rows(s, k):
    nrow = s.shape[0]
    rows = lax.broadcasted_iota(I32, s.shape, 0)
    vals, idxs = [], []
    for _ in range(k):
        m = jnp.max(s, axis=0, keepdims=True)
        i = jnp.min(jnp.where(s == m, rows, nrow), axis=0, keepdims=True)
        vals.append(m)
        idxs.append(i)
        s = jnp.where(rows == i, -jnp.inf, s)
    return jnp.concatenate(vals, axis=0), jnp.concatenate(idxs, axis=0)


def _pick_rows(table, sel):
    out = jnp.zeros(sel.shape, table.dtype)
    for i in range(table.shape[0]):
        out = jnp.where(sel == i, table[i:i + 1, :], out)
    return out


def _route_kernel(h2_ref, wq_ref, k1_ref, k2_ref, e1_ref, e2_ref, gw_ref):
    q = _dot(h2_ref[...], wq_ref[...])
    half = PEER_DK // 2
    s1 = _dot_nt(k1_ref[0], q[:, :half].astype(BF16))
    s2 = _dot_nt(k2_ref[0], q[:, half:].astype(BF16))
    v1, i1 = _topk_rows(s1, PEER_TOPK)
    v2, i2 = _topk_rows(s2, PEER_TOPK)
    cand = jnp.concatenate([v1[i:i + 1, :] + v2 for i in range(PEER_TOPK)], axis=0)
    sv, si = _topk_rows(cand, PEER_TOPK)
    e1_ref[...] = _pick_rows(i1, si // PEER_TOPK)
    e2_ref[...] = _pick_rows(i2, si % PEER_TOPK)
    p = jnp.exp(sv - sv[0:1, :])
    gw_ref[...] = p / jnp.sum(p, axis=0, keepdims=True)


def _route(h2, wq, k1, k2, tm):
    n = h2.shape[0]
    out = jax.ShapeDtypeStruct((PEER_HK, n), I32)
    ospec = pl.BlockSpec((PEER_TOPK, tm), lambda i, h: (h, i))
    return pl.pallas_call(
        _route_kernel,
        grid=(n // tm, PEER_HEADS),
        in_specs=[pl.BlockSpec((tm, D), lambda i, h: (i, 0)),
                  pl.BlockSpec((D, PEER_DK), lambda i, h: (0, h)),
                  pl.BlockSpec((1, PEER_NK, PEER_DK // 2), lambda i, h: (h, 0, 0)),
                  pl.BlockSpec((1, PEER_NK, PEER_DK // 2), lambda i, h: (h, 0, 0))],
        out_specs=[ospec, ospec, ospec],
        out_shape=[out, out, jax.ShapeDtypeStruct((PEER_HK, n), F32)],
        compiler_params=_cparams(("arbitrary", "arbitrary")),
        name="route",
    )(h2, wq, k1, k2)


EXPERT_AB = 8


def _experts_kernel(h2_ref, e1t_ref, e2t_ref, gwt_ref, u_ref, v_ref, x1_ref, g2_ref, fw_ref, y_ref,
                    e1_ref, e2_ref, gact_ref, coef_ref, w_ref, acc_ref):
    tm = h2_ref.shape[0]
    phase = pl.program_id(2)
    j = pl.program_id(3)
    nj = pl.num_programs(3)

    @pl.when((phase == 0) & (j == 0))
    def _():
        e1_ref[...] = e1t_ref[...].astype(F32).T.astype(I32)
        e2_ref[...] = e2t_ref[...].astype(F32).T.astype(I32)
        gact_ref[...] = jnp.zeros_like(gact_ref)

    @pl.when(phase == 0)
    def _():
        act = _dot_nt(h2_ref[...], u_ref[...])
        e1 = e1_ref[...]
        e2 = e2_ref[...]
        g = gact_ref[...]
        for a in range(EXPERT_AB):
            blk = act[:, a * LANES:(a + 1) * LANES]
            picked = jnp.take_along_axis(blk, e2, axis=1)
            g = g + jnp.where(e1 == j * EXPERT_AB + a, picked, 0.0)
        gact_ref[...] = g

    @pl.when((phase == 1) & (j == 0))
    def _():
        ga = gact_ref[...]
        gelu = 0.5 * ga * (1.0 + lax.erf(ga * (0.5 ** 0.5)))
        coef_ref[...] = gwt_ref[...].T * gelu
        acc_ref[...] = jnp.zeros_like(acc_ref)
        rows = lax.broadcasted_iota(I32, (PEER_NK, PEER_HK), 0)

        def scatter(t, carry):
            ct = jnp.where(rows == e1_ref[pl.ds(t, 1), :], coef_ref[pl.ds(t, 1), :], 0.0).astype(BF16)
            bt = (rows == e2_ref[pl.ds(t, 1), :]).astype(BF16)
            w_ref[pl.ds(pl.multiple_of(t * PEER_NK, PEER_NK), PEER_NK), :] = _dot_nt(ct, bt)
            return carry

        lax.fori_loop(0, tm, scatter, 0)

    @pl.when(phase == 1)
    def _():
        parts = [w_ref[pl.ds(j * EXPERT_AB + a, tm, stride=PEER_NK), :].astype(BF16) for a in range(EXPERT_AB)]
        acc_ref[...] += _dot(jnp.concatenate(parts, axis=1), v_ref[...])

    @pl.when((phase == 1) & (j == nj - 1))
    def _():
        bb, tt, _ = x1_ref.shape
        x2 = x1_ref[...] + g2_ref[...] * acc_ref[...].reshape(bb, tt, D)
        y_ref[...] = x2 * lax.rsqrt(jnp.mean(x2 * x2, axis=-1, keepdims=True) + EPS) * fw_ref[...]


def _experts(h2, e1t, e2t, gwt, u, v, x1, mod3, final_w, bb, tt):
    b, t, _ = x1.shape
    tm = bb * tt
    nj = PEER_NK // EXPERT_AB
    eb = EXPERT_AB * PEER_NK
    nt = t // tt
    tspec = pl.BlockSpec((PEER_HK, tm), lambda i, k, p, j: (0, i * nt + k))
    rspec = pl.BlockSpec((bb, tt, D), lambda i, k, p, j: (i, k, 0))
    return pl.pallas_call(
        _experts_kernel,
        grid=(b // bb, nt, 2, nj),
        in_specs=[pl.BlockSpec((tm, D), lambda i, k, p, j: (i * nt + k, 0)), tspec, tspec, tspec,
                  pl.BlockSpec((eb, D), lambda i, k, p, j: (jnp.where(p == 0, j, nj - 1), 0)),
                  pl.BlockSpec((eb, D), lambda i, k, p, j: (jnp.where(p == 1, j, 0), 0)),
                  rspec,
                  pl.BlockSpec((bb, 1, D), lambda i, k, p, j: (i, 0, 5)),
                  pl.BlockSpec((1, D), lambda i, k, p, j: (0, 0))],
        out_specs=rspec,
        out_shape=jax.ShapeDtypeStruct((b, t, D), F32),
        scratch_shapes=[pltpu.VMEM((tm, PEER_HK), I32),
                        pltpu.VMEM((tm, PEER_HK), I32),
                        pltpu.VMEM((tm, PEER_HK), F32),
                        pltpu.VMEM((tm, PEER_HK), F32),
                        pltpu.VMEM((tm * PEER_NK, PEER_NK), F32),
                        pltpu.VMEM((tm, D), F32)],
        compiler_params=_cparams(("arbitrary", "arbitrary", "arbitrary", "arbitrary")),
        name="experts",
    )(h2, e1t, e2t, gwt, u, v, x1, mod3, final_w.reshape(1, D))


def _tile(total, cap):
    t = min(total, cap)
    assert total % t == 0
    return t


def _group(x, mod, s_hg, s_ssm, conv_buf, lb, w, gla_chunk, ssd_chunk):
    b, t, _ = x.shape
    n = b * t
    mod3 = mod.reshape(b, 1, N_MOD * D)
    if t >= 256:
        bb, tt = 1, _tile(t, 512)
    else:
        bb, tt = b, t
    proj = _inproj(x, mod3, w["norm1_w"], w["w_in"], bb, tt)

    o_g, s_hg_new = _gla(proj, lb, w["hgrn_norm_w"], s_hg, _tile(t, 512), gla_chunk)

    conv_buf8 = jnp.pad(conv_buf, ((0, 0), (5, 0), (0, 0)))
    h0t = jnp.swapaxes(s_ssm.reshape(b, SSM_G, SSM_R * SSM_P, SSM_N), 2, 3)
    y_g, h_t = _ssd(proj, w["conv_w"], w["conv_b"], conv_buf8, w["dt_bias"], w["a_log"], w["ssm_d"],
                    w["ssm_norm_w"], h0t, ssd_chunk)
    s_ssm_new = jnp.swapaxes(h_t, 2, 3).reshape(b, SSM_HEADS, SSM_P, SSM_N)
    xbc = proj[:, :, COL_XS:COL_XS + SSM_XBC]
    assert t >= 3
    conv_new = xbc[:, t - 3:]

    if t >= 256:
        bb2, tt2 = 1, _tile(t, 256)
    else:
        bb2, tt2 = b, t
    x1, h2 = _merge(o_g, y_g, proj, x, mod3, w["norm2_w"], w["w_branch_a"], w["w_branch_b"], w["w_out"], bb2, tt2)

    h2f = h2.reshape(n, D)
    e1t, e2t, gwt = _route(h2f, w["peer_wq"], w["peer_keys1"], w["peer_keys2"], _tile(n, 256))
    y = _experts(h2f, e1t, e2t, gwt, w["peer_u"], w["peer_v"], x1, mod3, w["final_norm_w"], bb2, tt2)
    return y, s_hg_new, s_ssm_new, conv_new


def kernel(x_prompt, x_sample, c_prompt, c_sample, state_hgrn, state_ssm, state_conv, w_ada, b_ada, norm1_w, w_in,
           hgrn_lower_bounds, hgrn_norm_w, conv_w, conv_b, dt_bias, a_log, ssm_d, ssm_norm_w, w_branch_a,
           w_branch_b, w_out, norm2_w, peer_wq, peer_keys1, peer_keys2, peer_u, peer_v, final_norm_w):
    depth = w_ada.shape[0]
    assert depth == 1
    bp, tp, _ = x_prompt.shape
    bs, ts, _ = x_sample.shape
    lb_all = jnp.cumsum(jax.nn.softmax(hgrn_lower_bounds.astype(F32), axis=0), axis=0)

    l = 0
    wi = w_in[l]
    c_hg, c_z, c_xbc, c_dt = 4096, 6144, 9216, 9248
    w_pad = jnp.concatenate(
        [wi[:, :c_xbc], wi[:, c_dt:], wi[:, c_xbc:c_dt],
         jnp.zeros((D, PROJ_COLS - COL_DT - SSM_HEADS), wi.dtype)], axis=1).astype(BF16)
    w = dict(norm1_w=norm1_w[l], w_in=w_pad, hgrn_norm_w=hgrn_norm_w[l], conv_w=conv_w[l], conv_b=conv_b[l],
             dt_bias=dt_bias[l], a_log=a_log[l], ssm_d=ssm_d[l], ssm_norm_w=ssm_norm_w[l],
             w_branch_a=w_branch_a[l].astype(BF16), w_branch_b=w_branch_b[l].astype(BF16),
             w_out=w_out[l].astype(BF16), norm2_w=norm2_w[l], peer_wq=peer_wq[l].astype(BF16),
             peer_keys1=peer_keys1[l].astype(BF16), peer_keys2=peer_keys2[l].astype(BF16),
             peer_u=peer_u[l].astype(BF16), peer_v=peer_v[l].astype(BF16), final_norm_w=final_norm_w)

    mod = _adaln(jnp.concatenate([c_prompt, c_sample], axis=0), w_ada[l], b_ada[l])

    z_hg = jnp.zeros((bp, HG_HEADS, HG_DK, LANES), F32)
    z_ssm = jnp.zeros((bp, SSM_HEADS, SSM_P, SSM_N), F32)
    z_conv = jnp.zeros((bp, 3, SSM_XBC), F32)
    yp, hg_p, ssm_p, conv_p = _group(x_prompt, mod[:bp], z_hg, z_ssm, z_conv, lb_all[l], w,
                                     min(GLA_CHUNK, tp), min(SSD_CHUNK, tp))
    ys, hg_s, ssm_s, conv_s = _group(x_sample, mod[bp:], state_hgrn[l], state_ssm[l], state_conv[l], lb_all[l], w,
                                     min(GLA_CHUNK, ts), min(SSD_CHUNK, ts))
    return (yp, ys, hg_p[None], ssm_p[None], conv_p[None], hg_s[None], ssm_s[None], conv_s[None])
```

```python
import functools

import jax
import jax.numpy as jnp
from jax import lax
from jax.experimental import pallas as pl
from jax.experimental.pallas import tpu as pltpu

F32 = jnp.float32
BF16 = jnp.bfloat16
I32 = jnp.int32

EPS = 1e-6
D = 1024
N_MOD = 6
HG_HEADS = 8
HG_DK = 128
SSM_HEADS = 32
SSM_P = 64
SSM_G = 4
SSM_R = SSM_HEADS // SSM_G
SSM_N = 128
SSM_INNER = SSM_HEADS * SSM_P
SSM_GW = SSM_INNER // SSM_G
SSM_XBC = SSM_INNER + 2 * SSM_G * SSM_N
PEER_HEADS = 8
PEER_NK = 128
PEER_TOPK = 16
PEER_HK = PEER_HEADS * PEER_TOPK
PEER_DK = 256

LANES = 128
GLA_CHUNK = 64
SSD_CHUNK = 256
VMEM_LIMIT = 56 * 1024 * 1024

COL_HG = 0
COL_Z = 4096
COL_XS = 6144
COL_B = COL_XS + SSM_INNER
COL_C = COL_B + SSM_G * SSM_N
COL_GA = 9216
COL_GB = 10240
COL_DT = 11264
PROJ_COLS = 11520
PROJ_TN = 1280


def _cparams(sem):
    return pltpu.CompilerParams(dimension_semantics=sem, vmem_limit_bytes=VMEM_LIMIT)


def _silu(x):
    return x * jax.nn.sigmoid(x)


def _split3(x):
    hi = x.astype(BF16)
    r = x - hi.astype(F32)
    mid = r.astype(BF16)
    lo = (r - mid.astype(F32)).astype(BF16)
    return hi, mid, lo


def _dot(a, b):
    return jnp.dot(a, b, preferred_element_type=F32)


def _dot_nt(a, b):
    return lax.dot_general(a, b, (((1,), (1,)), ((), ())), preferred_element_type=F32)


def _dot_tn(a, b):
    return lax.dot_general(a, b, (((0,), (0,)), ((), ())), preferred_element_type=F32)


def _dot3_left(m_bf16, x):
    hi, mid, lo = _split3(x)
    return _dot(m_bf16, hi) + _dot(m_bf16, mid) + _dot(m_bf16, lo)


def _dot3_right(x, m_bf16):
    hi, mid, lo = _split3(x)
    return _dot(hi, m_bf16) + _dot(mid, m_bf16) + _dot(lo, m_bf16)


def _tril(n):
    r = lax.broadcasted_iota(I32, (n, n), 0)
    c = lax.broadcasted_iota(I32, (n, n), 1)
    return r >= c


def _adaln_kernel(c_ref, w_ref, b_ref, o_ref):
    c = c_ref[...]
    o_ref[...] = _dot(_silu(c).astype(BF16), w_ref[...].astype(BF16)) + b_ref[...]


def _adaln(c, w_ada, b_ada):
    nb = c.shape[0]
    ncol = w_ada.shape[1]
    return pl.pallas_call(
        _adaln_kernel,
        grid=(ncol // D,),
        in_specs=[pl.BlockSpec((nb, D), lambda j: (0, 0)),
                  pl.BlockSpec((D, D), lambda j: (0, j)),
                  pl.BlockSpec((1, D), lambda j: (0, j))],
        out_specs=pl.BlockSpec((nb, D), lambda j: (0, j)),
        out_shape=jax.ShapeDtypeStruct((nb, ncol), F32),
        compiler_params=_cparams(("arbitrary",)),
        name="adaln",
    )(c, w_ada, b_ada.reshape(1, ncol))


def _inproj_kernel(x_ref, sh_ref, sc_ref, nw_ref, w_ref, o_ref, h_ref):
    bb, tt, _ = x_ref.shape

    @pl.when(pl.program_id(2) == 0)
    def _():
        x = x_ref[...]
        xn = x * lax.rsqrt(jnp.mean(x * x, axis=-1, keepdims=True) + EPS)
        h = xn * nw_ref[...] * (1.0 + sc_ref[...]) + sh_ref[...]
        h_ref[...] = h.reshape(bb * tt, D).astype(BF16)

    o_ref[...] = _dot(h_ref[...], w_ref[...]).reshape(bb, tt, -1)


def _inproj(x, mod3, norm_w, w_pad, bb, tt):
    b, t, _ = x.shape
    return pl.pallas_call(
        _inproj_kernel,
        grid=(b // bb, t // tt, PROJ_COLS // PROJ_TN),
        in_specs=[pl.BlockSpec((bb, tt, D), lambda i, j, n: (i, j, 0)),
                  pl.BlockSpec((bb, 1, D), lambda i, j, n: (i, 0, 0)),
                  pl.BlockSpec((bb, 1, D), lambda i, j, n: (i, 0, 1)),
                  pl.BlockSpec((1, D), lambda i, j, n: (0, 0)),
                  pl.BlockSpec((D, PROJ_TN), lambda i, j, n: (0, n))],
        out_specs=pl.BlockSpec((bb, tt, PROJ_TN), lambda i, j, n: (i, j, n)),
        out_shape=jax.ShapeDtypeStruct((b, t, PROJ_COLS), F32),
        scratch_shapes=[pltpu.VMEM((bb * tt, D), BF16)],
        compiler_params=_cparams(("arbitrary", "arbitrary", "arbitrary")),
        name="inproj",
    )(x, mod3, mod3, norm_w.reshape(1, D), w_pad)


def _gla_kernel(q_ref, f_ref, i_ref, g_ref, lb_ref, nw_ref, s0_ref, o_ref, s_out_ref, s_ref, *, chunk):
    tt = q_ref.shape[1]
    nchunk = tt // chunk
    t_idx = pl.program_id(2)

    @pl.when(t_idx == 0)
    def _():
        s_ref[...] = s0_ref[0, 0].T

    lb = lb_ref[...]
    tril = _tril(chunk)
    tril_bf = tril.astype(BF16)
    mid = chunk // 2

    q_all = _silu(q_ref[0])
    ff = f_ref[0]
    logf = jnp.log(lb + (1.0 - lb) * jax.nn.sigmoid(ff))
    k_all = (1.0 - lb) * jax.nn.sigmoid(-ff)
    v_all = i_ref[0].astype(BF16)
    lf3 = jnp.concatenate(_split3(logf), axis=1)

    st = s_ref[...]
    outs = []
    for c in range(nchunk):
        sl = slice(c * chunk, (c + 1) * chunk)
        q, k, v = q_all[sl], k_all[sl], v_all[sl]
        g3 = _dot(tril_bf, lf3[sl])
        g = g3[:, :LANES] + g3[:, LANES:2 * LANES] + g3[:, 2 * LANES:]
        g_mid = g[mid:mid + 1, :]
        g_last = g[chunk - 1:chunk, :]
        qs = (q * jnp.exp(g - g_mid)).astype(BF16)
        ks = (k * jnp.exp(g_mid - g)).astype(BF16)
        att = jnp.where(tril, _dot_nt(qs, ks), 0.0).astype(BF16)
        outs.append(_dot(att, v) + _dot_nt((q * jnp.exp(g)).astype(BF16), st.astype(BF16)))
        kd = (k * jnp.exp(g_last - g)).astype(BF16)
        st = jnp.exp(g_last) * st + _dot_tn(v, kd)
    s_ref[...] = st
    o = jnp.concatenate(outs, axis=0) if nchunk > 1 else outs[0]
    on = o * lax.rsqrt(jnp.mean(o * o, axis=-1, keepdims=True) + EPS) * nw_ref[...]
    o_ref[0] = (on * _silu(g_ref[0])).astype(BF16)

    @pl.when(t_idx == pl.num_programs(2) - 1)
    def _():
        s_out_ref[0, 0] = s_ref[...].T


def _gla(proj, lb, hg_norm_w, s0, tt, chunk):
    b, t, _ = proj.shape
    col = lambda base: (lambda i, h, j: (i, j, base // LANES + h))
    return pl.pallas_call(
        functools.partial(_gla_kernel, chunk=chunk),
        grid=(b, HG_HEADS, t // tt),
        in_specs=[pl.BlockSpec((1, tt, LANES), col(COL_HG)),
                  pl.BlockSpec((1, tt, LANES), col(COL_HG + 1024)),
                  pl.BlockSpec((1, tt, LANES), col(COL_HG + 2048)),
                  pl.BlockSpec((1, tt, LANES), col(COL_HG + 3072)),
                  pl.BlockSpec((1, LANES), lambda i, h, j: (0, h)),
                  pl.BlockSpec((1, LANES), lambda i, h, j: (0, h)),
                  pl.BlockSpec((1, 1, HG_DK, LANES), lambda i, h, j: (i, h, 0, 0))],
        out_specs=[pl.BlockSpec((1, tt, LANES), lambda i, h, j: (i, j, h)),
                   pl.BlockSpec((1, 1, HG_DK, LANES), lambda i, h, j: (i, h, 0, 0))],
        out_shape=[jax.ShapeDtypeStruct((b, t, D), BF16),
                   jax.ShapeDtypeStruct((b, HG_HEADS, HG_DK, LANES), F32)],
        scratch_shapes=[pltpu.VMEM((HG_DK, LANES), F32)],
        compiler_params=_cparams(("arbitrary", "arbitrary", "arbitrary")),
        name="gla",
    )(proj, proj, proj, proj, lb.reshape(1, D), hg_norm_w.reshape(1, D), s0)


def _ssd_kernel(xs_ref, b_ref, c_ref, z_ref, dt_ref, cwx_ref, cwb_ref, cwc_ref, cbx_ref, cbb_ref, cbc_ref,
                bufx_ref, bufb_ref, bufc_ref, dtb_ref, alog_ref, dpar_ref, sel_ref, exp_ref, nw_ref, h0_ref,
                y_ref, h_out_ref, h_ref, xpx_ref, xpb_ref, xpc_ref):
    L = xs_ref.shape[1]
    t_idx = pl.program_id(2)

    @pl.when(t_idx == 0)
    def _():
        h_ref[...] = h0_ref[0, 0]
        xpx_ref[0:8, :] = bufx_ref[0]
        xpb_ref[0:8, :] = bufb_ref[0]
        xpc_ref[0:8, :] = bufc_ref[0]

    def conv(xp_ref, raw, w_ref, bias_ref):
        xp_ref[8:8 + L, :] = raw
        acc = bias_ref[...] + xp_ref[5:5 + L, :] * w_ref[0:1, :]
        for j in range(1, 4):
            acc = acc + xp_ref[5 + j:5 + j + L, :] * w_ref[j:j + 1, :]
        xp_ref[0:8, :] = xp_ref[L:L + 8, :]
        return _silu(acc)

    xs = conv(xpx_ref, xs_ref[0], cwx_ref, cbx_ref)
    bm = conv(xpb_ref, b_ref[0], cwb_ref, cbb_ref)
    cm = conv(xpc_ref, c_ref[0], cwc_ref, cbc_ref)
    bm_bf = bm.astype(BF16)
    cm_bf = cm.astype(BF16)

    sel = sel_ref[0]
    expand = exp_ref[...]
    dt_raw = _dot3_right(dt_ref[0], sel)
    dt = jax.nn.softplus(dt_raw + dtb_ref[0])
    a_neg = -jnp.exp(alog_ref[0])
    tril = _tril(L)
    ac = _dot3_left(tril.astype(BF16), dt * a_neg)
    a_last = ac[L - 1:L, :]

    def expand2(v):
        hi = v.astype(BF16)
        lo = (v - hi.astype(F32)).astype(BF16)
        return _dot(hi, expand) + _dot(lo, expand)

    dt_x = expand2(dt)
    eac_x = expand2(jnp.exp(ac))
    wd_x = expand2(jnp.exp(a_last - ac) * dt)
    row8 = lax.broadcasted_iota(I32, (8, LANES), 0)
    tail_x = expand2(jnp.where(row8 == 0, jnp.exp(a_last), jnp.where(row8 == 1, dpar_ref[0], 0.0)))
    decay_x = tail_x[0:1, :]
    d_x = tail_x[1:2, :]

    xdt = (xs * dt_x).astype(BF16)
    cbm = jnp.where(tril, _dot_nt(cm_bf, bm_bf), 0.0)
    ac_t = ac.T
    lane = lax.broadcasted_iota(I32, (L, LANES), 1)
    zero = jnp.zeros((), BF16)
    ys = []
    for pair in range(SSM_R // 2):
        xq = xdt[:, pair * LANES:(pair + 1) * LANES]
        acc = None
        for half in range(2):
            r = 2 * pair + half
            seg = jnp.exp(jnp.minimum(ac[:, r:r + 1] - ac_t[r:r + 1, :], 0.0))
            w = (cbm * seg).astype(BF16)
            keep = (lane < SSM_P) if half == 0 else (lane >= SSM_P)
            part = _dot(w, jnp.where(keep, xq, zero))
            acc = part if acc is None else acc + part
        ys.append(acc)
    h_old = h_ref[...]
    y = jnp.concatenate(ys, axis=1) + _dot(cm_bf, h_old.astype(BF16)) * eac_x
    xw = (xs * wd_x).astype(BF16)
    h_ref[...] = decay_x * h_old + _dot_tn(bm_bf, xw)

    y = (y + d_x * xs) * _silu(z_ref[0])
    y = y * lax.rsqrt(jnp.mean(y * y, axis=-1, keepdims=True) + EPS) * nw_ref[...]
    y_ref[0] = y.astype(BF16)

    @pl.when(t_idx == pl.num_programs(2) - 1)
    def _():
        h_out_ref[0, 0] = h_ref[...]


def _ssd(proj, conv_w, conv_b, conv_buf8, dt_bias, a_log, ssm_d, ssm_norm_w, h0t, chunk):
    b, t, _ = proj.shape
    L = chunk
    grp = lambda v: jnp.pad(v.reshape(SSM_G, 1, SSM_R), ((0, 0), (0, 0), (0, LANES - SSM_R)))
    hh = jnp.arange(LANES)
    sel = (hh[None, :, None] == (jnp.arange(SSM_G)[:, None, None] * SSM_R + hh[None, None, :])) & (hh[None, None, :] < SSM_R)
    expand = (hh[:, None] == (jnp.arange(SSM_GW)[None, :] // SSM_P))
    cb2 = conv_b.reshape(1, SSM_XBC)
    nb_x, nb_b, nb_c = 0, SSM_INNER // LANES, (SSM_INNER + SSM_G * SSM_N) // LANES
    pcol = lambda base, width: (lambda i, g, j: (i, j, base // width + g))
    return pl.pallas_call(
        _ssd_kernel,
        grid=(b, SSM_G, t // L),
        in_specs=[pl.BlockSpec((1, L, SSM_GW), pcol(COL_XS, SSM_GW)),
                  pl.BlockSpec((1, L, SSM_N), pcol(COL_B, SSM_N)),
                  pl.BlockSpec((1, L, SSM_N), pcol(COL_C, SSM_N)),
                  pl.BlockSpec((1, L, SSM_GW), pcol(COL_Z, SSM_GW)),
                  pl.BlockSpec((1, L, LANES), lambda i, g, j: (i, j, COL_DT // LANES)),
                  pl.BlockSpec((4, SSM_GW), lambda i, g, j: (0, g)),
                  pl.BlockSpec((4, SSM_N), lambda i, g, j: (0, nb_b + g)),
                  pl.BlockSpec((4, SSM_N), lambda i, g, j: (0, nb_c + g)),
                  pl.BlockSpec((1, SSM_GW), lambda i, g, j: (0, g)),
                  pl.BlockSpec((1, SSM_N), lambda i, g, j: (0, nb_b + g)),
                  pl.BlockSpec((1, SSM_N), lambda i, g, j: (0, nb_c + g)),
                  pl.BlockSpec((1, 8, SSM_GW), lambda i, g, j: (i, 0, g)),
                  pl.BlockSpec((1, 8, SSM_N), lambda i, g, j: (i, 0, nb_b + g)),
                  pl.BlockSpec((1, 8, SSM_N), lambda i, g, j: (i, 0, nb_c + g)),
                  pl.BlockSpec((1, 1, LANES), lambda i, g, j: (g, 0, 0)),
                  pl.BlockSpec((1, 1, LANES), lambda i, g, j: (g, 0, 0)),
                  pl.BlockSpec((1, 1, LANES), lambda i, g, j: (g, 0, 0)),
                  pl.BlockSpec((1, LANES, LANES), lambda i, g, j: (g, 0, 0)),
                  pl.BlockSpec((LANES, SSM_GW), lambda i, g, j: (0, 0)),
                  pl.BlockSpec((1, SSM_GW), lambda i, g, j: (0, g)),
                  pl.BlockSpec((1, 1, SSM_N, SSM_GW), lambda i, g, j: (i, g, 0, 0))],
        out_specs=[pl.BlockSpec((1, L, SSM_GW), lambda i, g, j: (i, j, g)),
                   pl.BlockSpec((1, 1, SSM_N, SSM_GW), lambda i, g, j: (i, g, 0, 0))],
        out_shape=[jax.ShapeDtypeStruct((b, t, SSM_INNER), BF16),
                   jax.ShapeDtypeStruct((b, SSM_G, SSM_N, SSM_GW), F32)],
        scratch_shapes=[pltpu.VMEM((SSM_N, SSM_GW), F32),
                        pltpu.VMEM((L + 8, SSM_GW), F32),
                        pltpu.VMEM((L + 8, SSM_N), F32),
                        pltpu.VMEM((L + 8, SSM_N), F32)],
        compiler_params=_cparams(("arbitrary", "arbitrary", "arbitrary")),
        name="ssd",
    )(proj, proj, proj, proj, proj, conv_w, conv_w, conv_w, cb2, cb2, cb2, conv_buf8, conv_buf8, conv_buf8,
      grp(dt_bias), grp(a_log), grp(ssm_d), sel.astype(BF16), expand.astype(BF16),
      ssm_norm_w.reshape(1, SSM_INNER), h0t)


def _merge_kernel(o_ref, y_ref, ga_ref, gb_ref, x_ref, g1_ref, sh2_ref, sc2_ref, nw_ref, wa_ref, wb_ref, wo_ref,
                  x1_ref, h2_ref):
    bb, tt, _ = x_ref.shape
    n = bb * tt
    pa = _dot(o_ref[...].reshape(n, D), wa_ref[...])
    pb = _dot(y_ref[...].reshape(n, SSM_INNER), wb_ref[...])
    merged = (jax.nn.sigmoid(ga_ref[...].reshape(n, D)) * pa + jax.nn.sigmoid(gb_ref[...].reshape(n, D)) * pb)
    mix = _dot(merged.astype(BF16), wo_ref[...]).reshape(bb, tt, D)
    x1 = x_ref[...] + g1_ref[...] * mix
    x1_ref[...] = x1
    xn = x1 * lax.rsqrt(jnp.mean(x1 * x1, axis=-1, keepdims=True) + EPS)
    h2_ref[...] = (xn * nw_ref[...] * (1.0 + sc2_ref[...]) + sh2_ref[...]).astype(BF16)


def _merge(o_g, y_g, proj, x, mod3, norm2_w, wa, wb, wo, bb, tt):
    b, t, _ = x.shape
    row = lambda i, j: (i, j, 0)
    modc = lambda c: (lambda i, j: (i, 0, c))
    const = lambda i, j: (0, 0)
    return pl.pallas_call(
        _merge_kernel,
        grid=(b // bb, t // tt),
        in_specs=[pl.BlockSpec((bb, tt, D), row),
                  pl.BlockSpec((bb, tt, SSM_INNER), row),
                  pl.BlockSpec((bb, tt, D), lambda i, j: (i, j, COL_GA // D)),
                  pl.BlockSpec((bb, tt, D), lambda i, j: (i, j, COL_GB // D)),
                  pl.BlockSpec((bb, tt, D), row),
                  pl.BlockSpec((bb, 1, D), modc(2)),
                  pl.BlockSpec((bb, 1, D), modc(3)),
                  pl.BlockSpec((bb, 1, D), modc(4)),
                  pl.BlockSpec((1, D), const),
                  pl.BlockSpec((D, D), const),
                  pl.BlockSpec((SSM_INNER, D), const),
                  pl.BlockSpec((D, D), const)],
        out_specs=[pl.BlockSpec((bb, tt, D), row), pl.BlockSpec((bb, tt, D), row)],
        out_shape=[jax.ShapeDtypeStruct((b, t, D), F32), jax.ShapeDtypeStruct((b, t, D), BF16)],
        compiler_params=_cparams(("arbitrary", "arbitrary")),
        name="merge",
    )(o_g, y_g, proj, proj, x, mod3, mod3, mod3, norm2_w.reshape(1, D), wa, wb, wo)


_NO_ROW = float(1 << 20)


def _topk_rows(s, ids, k):
    vals, idxs = [], []
    for _ in range(k):
        m = jnp.max(s, axis=0, keepdims=True)
        i = jnp.min(jnp.where(s == m, ids, _NO_ROW), axis=0, keepdims=True)
        vals.append(m)
        idxs.append(i)
        s = jnp.where(ids == i, -jnp.inf, s)
    return jnp.concatenate(vals, axis=0), jnp.concatenate(idxs, axis=0)


def _pick_rows(table, sel):
    out = jnp.zeros(sel.shape, table.dtype)
    for i in range(table.shape[0]):
        out = jnp.where(sel == float(i), table[i:i + 1, :], out)
    return out


def _pair_candidates(v1, v2):
    kk = PEER_TOPK
    n = v1.shape[1]
    r = lax.broadcasted_iota(I32, (8, n), 0)
    rf = r.astype(F32)
    vals, ids = [], []

    def add(val, flat, lo=0, hi=7):
        ok = (r >= lo) & (r <= hi)
        vals.append(jnp.where(ok, val, -jnp.inf))
        ids.append(jnp.where(ok, flat, _NO_ROW))

    add(v1[0:1] + v2[0:8], rf)
    add(v1[0:1] + v2[8:16], rf + 8.0)
    add(v1[1:2] + v2[0:8], rf + float(kk))
    add(v1[8:16] + v2[0:1], (rf + 8.0) * float(kk))
    for j in range(5):
        hi = kk // (j + 1) - 1
        add(v1[0:8] + v2[j:j + 1], rf * float(kk) + float(j), lo=2, hi=min(hi, 7))
    return jnp.concatenate(vals, axis=0), jnp.concatenate(ids, axis=0)


def _route_kernel(h2_ref, wq_ref, k1_ref, k2_ref, e1_ref, e2_ref, gw_ref):
    q = _dot(h2_ref[...], wq_ref[...])
    half = PEER_DK // 2
    s1 = _dot_nt(k1_ref[0], q[:, :half].astype(BF16))
    s2 = _dot_nt(k2_ref[0], q[:, half:].astype(BF16))
    key_ids = lax.broadcasted_iota(I32, s1.shape, 0).astype(F32)
    v1, i1 = _topk_rows(s1, key_ids, PEER_TOPK)
    v2, i2 = _topk_rows(s2, key_ids, PEER_TOPK)
    cand, flat = _pair_candidates(v1, v2)
    sv, sf = _topk_rows(cand, flat, PEER_TOPK)
    sel_i = jnp.floor(sf * (1.0 / PEER_TOPK))
    sel_j = sf - sel_i * float(PEER_TOPK)
    e1_ref[...] = _pick_rows(i1, sel_i).astype(I32)
    e2_ref[...] = _pick_rows(i2, sel_j).astype(I32)
    p = jnp.exp(sv - sv[0:1, :])
    gw_ref[...] = p / jnp.sum(p, axis=0, keepdims=True)


def _route(h2, wq, k1, k2, tm):
    n = h2.shape[0]
    out = jax.ShapeDtypeStruct((PEER_HK, n), I32)
    ospec = pl.BlockSpec((PEER_TOPK, tm), lambda i, h: (h, i))
    return pl.pallas_call(
        _route_kernel,
        grid=(n // tm, PEER_HEADS),
        in_specs=[pl.BlockSpec((tm, D), lambda i, h: (i, 0)),
                  pl.BlockSpec((D, PEER_DK), lambda i, h: (0, h)),
                  pl.BlockSpec((1, PEER_NK, PEER_DK // 2), lambda i, h: (h, 0, 0)),
                  pl.BlockSpec((1, PEER_NK, PEER_DK // 2), lambda i, h: (h, 0, 0))],
        out_specs=[ospec, ospec, ospec],
        out_shape=[out, out, jax.ShapeDtypeStruct((PEER_HK, n), F32)],
        compiler_params=_cparams(("arbitrary", "arbitrary")),
        name="route",
    )(h2, wq, k1, k2)


EXPERT_AB = 8
W_PAD = 8
SCATTER_UNROLL = 32


def _experts_kernel(h2_ref, e1t_ref, e2t_ref, gwt_ref, u_ref, v_ref, x1_ref, g2_ref, fw_ref, y_ref,
                    e1_ref, e2_ref, gact_ref, coef_ref, w_ref, acc_ref):
    tm = h2_ref.shape[0]
    phase = pl.program_id(2)
    j = pl.program_id(3)
    nj = pl.num_programs(3)

    @pl.when((phase == 0) & (j == 0))
    def _():
        e1_ref[...] = e1t_ref[...].astype(F32).T.astype(I32)
        e2_ref[...] = e2t_ref[...].astype(F32).T.astype(I32)
        gact_ref[...] = jnp.zeros_like(gact_ref)

    @pl.when(phase == 0)
    def _():
        act = _dot_nt(h2_ref[...], u_ref[...])
        e1 = e1_ref[...]
        e2 = e2_ref[...]
        g = gact_ref[...]
        for a in range(EXPERT_AB):
            blk = act[:, a * LANES:(a + 1) * LANES]
            picked = jnp.take_along_axis(blk, e2, axis=1)
            g = g + jnp.where(e1 == j * EXPERT_AB + a, picked, 0.0)
        gact_ref[...] = g

    @pl.when((phase == 1) & (j == 0))
    def _():
        ga = gact_ref[...]
        gelu = 0.5 * ga * (1.0 + lax.erf(ga * (0.5 ** 0.5)))
        coef_ref[...] = gwt_ref[...].T * gelu
        acc_ref[...] = jnp.zeros_like(acc_ref)

    half_steps = nj // 2
    a_half = PEER_NK // 2
    pitch = tm + W_PAD

    @pl.when((phase == 1) & (j % half_steps == 0))
    def _():
        a0 = (j // half_steps) * a_half
        rows_a = lax.broadcasted_iota(I32, (a_half, PEER_HK), 0) + a0
        rows_b = lax.broadcasted_iota(I32, (PEER_NK, PEER_HK), 0)

        def scatter(t, carry):
            ct = jnp.where(rows_a == e1_ref[pl.ds(t, 1), :], coef_ref[pl.ds(t, 1), :], 0.0).astype(BF16)
            bt = (rows_b == e2_ref[pl.ds(t, 1), :]).astype(BF16)
            w_ref[pl.ds(t, a_half, stride=pitch), :] = _dot_nt(ct, bt)
            return carry

        lax.fori_loop(0, tm, scatter, 0, unroll=SCATTER_UNROLL)

    @pl.when(phase == 1)
    def _():
        a_loc = (j % half_steps) * EXPERT_AB
        parts = [w_ref[pl.ds(pl.multiple_of((a_loc + a) * pitch, 8), tm), :].astype(BF16) for a in range(EXPERT_AB)]
        acc_ref[...] += _dot(jnp.concatenate(parts, axis=1), v_ref[...])

    @pl.when((phase == 1) & (j == nj - 1))
    def _():
        bb, tt, _ = x1_ref.shape
        x2 = x1_ref[...] + g2_ref[...] * acc_ref[...].reshape(bb, tt, D)
        y_ref[...] = x2 * lax.rsqrt(jnp.mean(x2 * x2, axis=-1, keepdims=True) + EPS) * fw_ref[...]


def _experts(h2, e1t, e2t, gwt, u, v, x1, mod3, final_w, bb, tt):
    b, t, _ = x1.shape
    tm = bb * tt
    nj = PEER_NK // EXPERT_AB
    eb = EXPERT_AB * PEER_NK
    nt = t // tt
    tspec = pl.BlockSpec((PEER_HK, tm), lambda i, k, p, j: (0, i * nt + k))
    rspec = pl.BlockSpec((bb, tt, D), lambda i, k, p, j: (i, k, 0))
    return pl.pallas_call(
        _experts_kernel,
        grid=(b // bb, nt, 2, nj),
        in_specs=[pl.BlockSpec((tm, D), lambda i, k, p, j: (i * nt + k, 0)), tspec, tspec, tspec,
                  pl.BlockSpec((eb, D), lambda i, k, p, j: (jnp.where(p == 0, j, nj - 1), 0)),
                  pl.BlockSpec((eb, D), lambda i, k, p, j: (jnp.where(p == 1, j, 0), 0)),
                  rspec,
                  pl.BlockSpec((bb, 1, D), lambda i, k, p, j: (i, 0, 5)),
                  pl.BlockSpec((1, D), lambda i, k, p, j: (0, 0))],
        out_specs=rspec,
        out_shape=jax.ShapeDtypeStruct((b, t, D), F32),
        scratch_shapes=[pltpu.VMEM((tm, PEER_HK), I32),
                        pltpu.VMEM((tm, PEER_HK), I32),
                        pltpu.VMEM((tm, PEER_HK), F32),
                        pltpu.VMEM((tm, PEER_HK), F32),
                        pltpu.VMEM(((PEER_NK // 2) * (tm + W_PAD), PEER_NK), F32),
                        pltpu.VMEM((tm, D), F32)],
        compiler_params=_cparams(("arbitrary", "arbitrary", "arbitrary", "arbitrary")),
        name="experts",
    )(h2, e1t, e2t, gwt, u, v, x1, mod3, final_w.reshape(1, D))


def _tile(total, cap):
    t = min(total, cap)
    assert total % t == 0
    return t


def _group(x, mod, s_hg, s_ssm, conv_buf, lb, w, gla_chunk, ssd_chunk):
    b, t, _ = x.shape
    n = b * t
    mod3 = mod.reshape(b, 1, N_MOD * D)
    if t >= 256:
        bb, tt = 1, _tile(t, 512)
    else:
        bb, tt = b, t
    proj = _inproj(x, mod3, w["norm1_w"], w["w_in"], bb, tt)

    o_g, s_hg_new = _gla(proj, lb, w["hgrn_norm_w"], s_hg, _tile(t, 512), gla_chunk)

    conv_buf8 = jnp.pad(conv_buf, ((0, 0), (5, 0), (0, 0)))
    h0t = jnp.swapaxes(s_ssm.reshape(b, SSM_G, SSM_R * SSM_P, SSM_N), 2, 3)
    y_g, h_t = _ssd(proj, w["conv_w"], w["conv_b"], conv_buf8, w["dt_bias"], w["a_log"], w["ssm_d"],
                    w["ssm_norm_w"], h0t, ssd_chunk)
    s_ssm_new = jnp.swapaxes(h_t, 2, 3).reshape(b, SSM_HEADS, SSM_P, SSM_N)
    xbc = proj[:, :, COL_XS:COL_XS + SSM_XBC]
    assert t >= 3
    conv_new = xbc[:, t - 3:]

    if t >= 256:
        bb2, tt2 = 1, _tile(t, 256)
    else:
        bb2, tt2 = b, t
    x1, h2 = _merge(o_g, y_g, proj, x, mod3, w["norm2_w"], w["w_branch_a"], w["w_branch_b"], w["w_out"], bb2, tt2)

    h2f = h2.reshape(n, D)
    e1t, e2t, gwt = _route(h2f, w["peer_wq"], w["peer_keys1"], w["peer_keys2"], _tile(n, 256))
    bb3, tt3 = (1, _tile(t, 512)) if t >= 256 else (b, t)
    y = _experts(h2f, e1t, e2t, gwt, w["peer_u"], w["peer_v"], x1, mod3, w["final_norm_w"], bb3, tt3)
    return y, s_hg_new, s_ssm_new, conv_new


def kernel(x_prompt, x_sample, c_prompt, c_sample, state_hgrn, state_ssm, state_conv, w_ada, b_ada, norm1_w, w_in,
           hgrn_lower_bounds, hgrn_norm_w, conv_w, conv_b, dt_bias, a_log, ssm_d, ssm_norm_w, w_branch_a,
           w_branch_b, w_out, norm2_w, peer_wq, peer_keys1, peer_keys2, peer_u, peer_v, final_norm_w):
    depth = w_ada.shape[0]
    assert depth == 1
    bp, tp, _ = x_prompt.shape
    bs, ts, _ = x_sample.shape
    lb_all = jnp.cumsum(jax.nn.softmax(hgrn_lower_bounds.astype(F32), axis=0), axis=0)

    l = 0
    wi = w_in[l]
    c_hg, c_z, c_xbc, c_dt = 4096, 6144, 9216, 9248
    w_pad = jnp.concatenate(
        [wi[:, :c_xbc], wi[:, c_dt:], wi[:, c_xbc:c_dt],
         jnp.zeros((D, PROJ_COLS - COL_DT - SSM_HEADS), wi.dtype)], axis=1).astype(BF16)
    w = dict(norm1_w=norm1_w[l], w_in=w_pad, hgrn_norm_w=hgrn_norm_w[l], conv_w=conv_w[l], conv_b=conv_b[l],
             dt_bias=dt_bias[l], a_log=a_log[l], ssm_d=ssm_d[l], ssm_norm_w=ssm_norm_w[l],
             w_branch_a=w_branch_a[l].astype(BF16), w_branch_b=w_branch_b[l].astype(BF16),
             w_out=w_out[l].astype(BF16), norm2_w=norm2_w[l], peer_wq=peer_wq[l].astype(BF16),
             peer_keys1=peer_keys1[l].astype(BF16), peer_keys2=peer_keys2[l].astype(BF16),
             peer_u=peer_u[l].astype(BF16), peer_v=peer_v[l].astype(BF16), final_norm_w=final_norm_w)

    mod = _adaln(jnp.concatenate([c_prompt, c_sample], axis=0), w_ada[l], b_ada[l])

    z_hg = jnp.zeros((bp, HG_HEADS, HG_DK, LANES), F32)
    z_ssm = jnp.zeros((bp, SSM_HEADS, SSM_P, SSM_N), F32)
    z_conv = jnp.zeros((bp, 3, SSM_XBC), F32)
    yp, hg_p, ssm_p, conv_p = _group(x_prompt, mod[:bp], z_hg, z_ssm, z_conv, lb_all[l], w,
                                     min(GLA_CHUNK, tp), min(SSD_CHUNK, tp))
    ys, hg_s, ssm_s, conv_s = _group(x_sample, mod[bp:], state_hgrn[l], state_ssm[l], state_conv[l], lb_all[l], w,
                                     min(GLA_CHUNK, ts), min(SSD_CHUNK, ts))
    return (yp, ys, hg_p[None], ssm_p[None], conv_p[None], hg_s[None], ssm_s[None], conv_s[None])
```

```python
import functools

import jax
import jax.numpy as jnp
from jax import lax
from jax.experimental import pallas as pl
from jax.experimental.pallas import tpu as pltpu

F32 = jnp.float32
BF16 = jnp.bfloat16
I32 = jnp.int32

EPS = 1e-6
D = 1024
N_MOD = 6
HG_HEADS = 8
HG_DK = 128
SSM_HEADS = 32
SSM_P = 64
SSM_G = 4
SSM_R = SSM_HEADS // SSM_G
SSM_N = 128
SSM_INNER = SSM_HEADS * SSM_P
SSM_GW = SSM_INNER // SSM_G
SSM_XBC = SSM_INNER + 2 * SSM_G * SSM_N
PEER_HEADS = 8
PEER_NK = 128
PEER_TOPK = 16
PEER_HK = PEER_HEADS * PEER_TOPK
PEER_DK = 256

LANES = 128
GLA_CHUNK = 64
SSD_CHUNK = 256
VMEM_LIMIT = 56 * 1024 * 1024

COL_HG = 0
COL_Z = 4096
COL_XS = 6144
COL_B = COL_XS + SSM_INNER
COL_C = COL_B + SSM_G * SSM_N
COL_GA = 9216
COL_GB = 10240
COL_DT = 11264
PROJ_COLS = 11520
PROJ_TN = 1280


def _cparams(sem):
    return pltpu.CompilerParams(dimension_semantics=sem, vmem_limit_bytes=VMEM_LIMIT)


def _silu(x):
    return x * jax.nn.sigmoid(x)


def _split3(x):
    hi = x.astype(BF16)
    r = x - hi.astype(F32)
    mid = r.astype(BF16)
    lo = (r - mid.astype(F32)).astype(BF16)
    return hi, mid, lo


def _dot(a, b):
    return jnp.dot(a, b, preferred_element_type=F32)


def _dot_nt(a, b):
    return lax.dot_general(a, b, (((1,), (1,)), ((), ())), preferred_element_type=F32)


def _dot_tn(a, b):
    return lax.dot_general(a, b, (((0,), (0,)), ((), ())), preferred_element_type=F32)


def _dot3_left(m_bf16, x):
    hi, mid, lo = _split3(x)
    return _dot(m_bf16, hi) + _dot(m_bf16, mid) + _dot(m_bf16, lo)


def _dot3_right(x, m_bf16):
    hi, mid, lo = _split3(x)
    return _dot(hi, m_bf16) + _dot(mid, m_bf16) + _dot(lo, m_bf16)


def _tril(n):
    r = lax.broadcasted_iota(I32, (n, n), 0)
    c = lax.broadcasted_iota(I32, (n, n), 1)
    return r >= c


def _adaln_kernel(c_ref, w_ref, b_ref, o_ref):
    c = c_ref[...]
    o_ref[...] = _dot(_silu(c).astype(BF16), w_ref[...].astype(BF16)) + b_ref[...]


def _adaln(c, w_ada, b_ada):
    nb = c.shape[0]
    ncol = w_ada.shape[1]
    return pl.pallas_call(
        _adaln_kernel,
        grid=(ncol // D,),
        in_specs=[pl.BlockSpec((nb, D), lambda j: (0, 0)),
                  pl.BlockSpec((D, D), lambda j: (0, j)),
                  pl.BlockSpec((1, D), lambda j: (0, j))],
        out_specs=pl.BlockSpec((nb, D), lambda j: (0, j)),
        out_shape=jax.ShapeDtypeStruct((nb, ncol), F32),
        compiler_params=_cparams(("arbitrary",)),
        name="adaln",
    )(c, w_ada, b_ada.reshape(1, ncol))


def _inproj_kernel(x_ref, sh_ref, sc_ref, nw_ref, w_ref, o_ref, h_ref):
    bb, tt, _ = x_ref.shape

    @pl.when(pl.program_id(2) == 0)
    def _():
        x = x_ref[...]
        xn = x * lax.rsqrt(jnp.mean(x * x, axis=-1, keepdims=True) + EPS)
        h = xn * nw_ref[...] * (1.0 + sc_ref[...]) + sh_ref[...]
        h_ref[...] = h.reshape(bb * tt, D).astype(BF16)

    o_ref[...] = _dot(h_ref[...], w_ref[...]).reshape(bb, tt, -1)


def _inproj(x, mod3, norm_w, w_pad, bb, tt):
    b, t, _ = x.shape
    return pl.pallas_call(
        _inproj_kernel,
        grid=(b // bb, t // tt, PROJ_COLS // PROJ_TN),
        in_specs=[pl.BlockSpec((bb, tt, D), lambda i, j, n: (i, j, 0)),
                  pl.BlockSpec((bb, 1, D), lambda i, j, n: (i, 0, 0)),
                  pl.BlockSpec((bb, 1, D), lambda i, j, n: (i, 0, 1)),
                  pl.BlockSpec((1, D), lambda i, j, n: (0, 0)),
                  pl.BlockSpec((D, PROJ_TN), lambda i, j, n: (0, n))],
        out_specs=pl.BlockSpec((bb, tt, PROJ_TN), lambda i, j, n: (i, j, n)),
        out_shape=jax.ShapeDtypeStruct((b, t, PROJ_COLS), F32),
        scratch_shapes=[pltpu.VMEM((bb * tt, D), BF16)],
        compiler_params=_cparams(("arbitrary", "arbitrary", "arbitrary")),
        name="inproj",
    )(x, mod3, mod3, norm_w.reshape(1, D), w_pad)


def _gla_kernel(q_ref, f_ref, i_ref, g_ref, lb_ref, nw_ref, s0_ref, o_ref, s_out_ref, s_ref, *, chunk):
    tt = q_ref.shape[1]
    nchunk = tt // chunk
    t_idx = pl.program_id(2)

    @pl.when(t_idx == 0)
    def _():
        s_ref[...] = s0_ref[0, 0].T

    lb = lb_ref[...]
    tril = _tril(chunk)
    tril_bf = tril.astype(BF16)
    mid = chunk // 2

    q_all = _silu(q_ref[0])
    ff = f_ref[0]
    logf = jnp.log(lb + (1.0 - lb) * jax.nn.sigmoid(ff))
    k_all = (1.0 - lb) * jax.nn.sigmoid(-ff)
    v_all = i_ref[0].astype(BF16)
    lf3 = jnp.concatenate(_split3(logf), axis=1)

    st = s_ref[...]
    outs = []
    for c in range(nchunk):
        sl = slice(c * chunk, (c + 1) * chunk)
        q, k, v = q_all[sl], k_all[sl], v_all[sl]
        g3 = _dot(tril_bf, lf3[sl])
        g = g3[:, :LANES] + g3[:, LANES:2 * LANES] + g3[:, 2 * LANES:]
        g_mid = g[mid:mid + 1, :]
        g_last = g[chunk - 1:chunk, :]
        qs = (q * jnp.exp(g - g_mid)).astype(BF16)
        ks = (k * jnp.exp(g_mid - g)).astype(BF16)
        att = jnp.where(tril, _dot_nt(qs, ks), 0.0).astype(BF16)
        outs.append(_dot(att, v) + _dot_nt((q * jnp.exp(g)).astype(BF16), st.astype(BF16)))
        kd = (k * jnp.exp(g_last - g)).astype(BF16)
        st = jnp.exp(g_last) * st + _dot_tn(v, kd)
    s_ref[...] = st
    o = jnp.concatenate(outs, axis=0) if nchunk > 1 else outs[0]
    on = o * lax.rsqrt(jnp.mean(o * o, axis=-1, keepdims=True) + EPS) * nw_ref[...]
    o_ref[0] = (on * _silu(g_ref[0])).astype(BF16)

    @pl.when(t_idx == pl.num_programs(2) - 1)
    def _():
        s_out_ref[0, 0] = s_ref[...].T


def _gla(proj, lb, hg_norm_w, s0, tt, chunk):
    b, t, _ = proj.shape
    col = lambda base: (lambda i, h, j: (i, j, base // LANES + h))
    return pl.pallas_call(
        functools.partial(_gla_kernel, chunk=chunk),
        grid=(b, HG_HEADS, t // tt),
        in_specs=[pl.BlockSpec((1, tt, LANES), col(COL_HG)),
                  pl.BlockSpec((1, tt, LANES), col(COL_HG + 1024)),
                  pl.BlockSpec((1, tt, LANES), col(COL_HG + 2048)),
                  pl.BlockSpec((1, tt, LANES), col(COL_HG + 3072)),
                  pl.BlockSpec((1, LANES), lambda i, h, j: (0, h)),
                  pl.BlockSpec((1, LANES), lambda i, h, j: (0, h)),
                  pl.BlockSpec((1, 1, HG_DK, LANES), lambda i, h, j: (i, h, 0, 0))],
        out_specs=[pl.BlockSpec((1, tt, LANES), lambda i, h, j: (i, j, h)),
                   pl.BlockSpec((1, 1, HG_DK, LANES), lambda i, h, j: (i, h, 0, 0))],
        out_shape=[jax.ShapeDtypeStruct((b, t, D), BF16),
                   jax.ShapeDtypeStruct((b, HG_HEADS, HG_DK, LANES), F32)],
        scratch_shapes=[pltpu.VMEM((HG_DK, LANES), F32)],
        compiler_params=_cparams(("arbitrary", "arbitrary", "arbitrary")),
        name="gla",
    )(proj, proj, proj, proj, lb.reshape(1, D), hg_norm_w.reshape(1, D), s0)


def _ssd_kernel(xs_ref, b_ref, c_ref, z_ref, dt_ref, cwx_ref, cwb_ref, cwc_ref, cbx_ref, cbb_ref, cbc_ref,
                bufx_ref, bufb_ref, bufc_ref, dtb_ref, alog_ref, dpar_ref, sel_ref, exp_ref, nw_ref, h0_ref,
                y_ref, h_out_ref, h_ref, xpx_ref, xpb_ref, xpc_ref):
    L = xs_ref.shape[1]
    t_idx = pl.program_id(2)

    @pl.when(t_idx == 0)
    def _():
        h_ref[...] = h0_ref[0, 0]
        xpx_ref[0:8, :] = bufx_ref[0]
        xpb_ref[0:8, :] = bufb_ref[0]
        xpc_ref[0:8, :] = bufc_ref[0]

    def conv(xp_ref, raw, w_ref, bias_ref):
        xp_ref[8:8 + L, :] = raw
        acc = bias_ref[...] + xp_ref[5:5 + L, :] * w_ref[0:1, :]
        for j in range(1, 4):
            acc = acc + xp_ref[5 + j:5 + j + L, :] * w_ref[j:j + 1, :]
        xp_ref[0:8, :] = xp_ref[L:L + 8, :]
        return _silu(acc)

    xs = conv(xpx_ref, xs_ref[0], cwx_ref, cbx_ref)
    bm = conv(xpb_ref, b_ref[0], cwb_ref, cbb_ref)
    cm = conv(xpc_ref, c_ref[0], cwc_ref, cbc_ref)
    bm_bf = bm.astype(BF16)
    cm_bf = cm.astype(BF16)

    sel = sel_ref[0]
    expand = exp_ref[...]
    dt_raw = _dot3_right(dt_ref[0], sel)
    dt = jax.nn.softplus(dt_raw + dtb_ref[0])
    a_neg = -jnp.exp(alog_ref[0])
    tril = _tril(L)
    ac = _dot3_left(tril.astype(BF16), dt * a_neg)
    a_last = ac[L - 1:L, :]

    def expand2(v):
        hi = v.astype(BF16)
        lo = (v - hi.astype(F32)).astype(BF16)
        return _dot(hi, expand) + _dot(lo, expand)

    dt_x = expand2(dt)
    eac_x = expand2(jnp.exp(ac))
    wd_x = expand2(jnp.exp(a_last - ac) * dt)
    row8 = lax.broadcasted_iota(I32, (8, LANES), 0)
    tail_x = expand2(jnp.where(row8 == 0, jnp.exp(a_last), jnp.where(row8 == 1, dpar_ref[0], 0.0)))
    decay_x = tail_x[0:1, :]
    d_x = tail_x[1:2, :]

    xdt = (xs * dt_x).astype(BF16)
    cbm = jnp.where(tril, _dot_nt(cm_bf, bm_bf), 0.0)
    ac_t = ac.T
    lane = lax.broadcasted_iota(I32, (L, LANES), 1)
    zero = jnp.zeros((), BF16)
    ys = []
    for pair in range(SSM_R // 2):
        xq = xdt[:, pair * LANES:(pair + 1) * LANES]
        acc = None
        for half in range(2):
            r = 2 * pair + half
            seg = jnp.exp(jnp.minimum(ac[:, r:r + 1] - ac_t[r:r + 1, :], 0.0))
            w = (cbm * seg).astype(BF16)
            keep = (lane < SSM_P) if half == 0 else (lane >= SSM_P)
            part = _dot(w, jnp.where(keep, xq, zero))
            acc = part if acc is None else acc + part
        ys.append(acc)
    h_old = h_ref[...]
    y = jnp.concatenate(ys, axis=1) + _dot(cm_bf, h_old.astype(BF16)) * eac_x
    xw = (xs * wd_x).astype(BF16)
    h_ref[...] = decay_x * h_old + _dot_tn(bm_bf, xw)

    y = (y + d_x * xs) * _silu(z_ref[0])
    y = y * lax.rsqrt(jnp.mean(y * y, axis=-1, keepdims=True) + EPS) * nw_ref[...]
    y_ref[0] = y.astype(BF16)

    @pl.when(t_idx == pl.num_programs(2) - 1)
    def _():
        h_out_ref[0, 0] = h_ref[...]


def _ssd(proj, conv_w, conv_b, conv_buf8, dt_bias, a_log, ssm_d, ssm_norm_w, h0t, chunk):
    b, t, _ = proj.shape
    L = chunk
    grp = lambda v: jnp.pad(v.reshape(SSM_G, 1, SSM_R), ((0, 0), (0, 0), (0, LANES - SSM_R)))
    hh = jnp.arange(LANES)
    sel = (hh[None, :, None] == (jnp.arange(SSM_G)[:, None, None] * SSM_R + hh[None, None, :])) & (hh[None, None, :] < SSM_R)
    expand = (hh[:, None] == (jnp.arange(SSM_GW)[None, :] // SSM_P))
    cb2 = conv_b.reshape(1, SSM_XBC)
    nb_x, nb_b, nb_c = 0, SSM_INNER // LANES, (SSM_INNER + SSM_G * SSM_N) // LANES
    pcol = lambda base, width: (lambda i, g, j: (i, j, base // width + g))
    return pl.pallas_call(
        _ssd_kernel,
        grid=(b, SSM_G, t // L),
        in_specs=[pl.BlockSpec((1, L, SSM_GW), pcol(COL_XS, SSM_GW)),
                  pl.BlockSpec((1, L, SSM_N), pcol(COL_B, SSM_N)),
                  pl.BlockSpec((1, L, SSM_N), pcol(COL_C, SSM_N)),
                  pl.BlockSpec((1, L, SSM_GW), pcol(COL_Z, SSM_GW)),
                  pl.BlockSpec((1, L, LANES), lambda i, g, j: (i, j, COL_DT // LANES)),
                  pl.BlockSpec((4, SSM_GW), lambda i, g, j: (0, g)),
                  pl.BlockSpec((4, SSM_N), lambda i, g, j: (0, nb_b + g)),
                  pl.BlockSpec((4, SSM_N), lambda i, g, j: (0, nb_c + g)),
                  pl.BlockSpec((1, SSM_GW), lambda i, g, j: (0, g)),
                  pl.BlockSpec((1, SSM_N), lambda i, g, j: (0, nb_b + g)),
                  pl.BlockSpec((1, SSM_N), lambda i, g, j: (0, nb_c + g)),
                  pl.BlockSpec((1, 8, SSM_GW), lambda i, g, j: (i, 0, g)),
                  pl.BlockSpec((1, 8, SSM_N), lambda i, g, j: (i, 0, nb_b + g)),
                  pl.BlockSpec((1, 8, SSM_N), lambda i, g, j: (i, 0, nb_c + g)),
                  pl.BlockSpec((1, 1, LANES), lambda i, g, j: (g, 0, 0)),
                  pl.BlockSpec((1, 1, LANES), lambda i, g, j: (g, 0, 0)),
                  pl.BlockSpec((1, 1, LANES), lambda i, g, j: (g, 0, 0)),
                  pl.BlockSpec((1, LANES, LANES), lambda i, g, j: (g, 0, 0)),
                  pl.BlockSpec((LANES, SSM_GW), lambda i, g, j: (0, 0)),
                  pl.BlockSpec((1, SSM_GW), lambda i, g, j: (0, g)),
                  pl.BlockSpec((1, 1, SSM_N, SSM_GW), lambda i, g, j: (i, g, 0, 0))],
        out_specs=[pl.BlockSpec((1, L, SSM_GW), lambda i, g, j: (i, j, g)),
                   pl.BlockSpec((1, 1, SSM_N, SSM_GW), lambda i, g, j: (i, g, 0, 0))],
        out_shape=[jax.ShapeDtypeStruct((b, t, SSM_INNER), BF16),
                   jax.ShapeDtypeStruct((b, SSM_G, SSM_N, SSM_GW), F32)],
        scratch_shapes=[pltpu.VMEM((SSM_N, SSM_GW), F32),
                        pltpu.VMEM((L + 8, SSM_GW), F32),
                        pltpu.VMEM((L + 8, SSM_N), F32),
                        pltpu.VMEM((L + 8, SSM_N), F32)],
        compiler_params=_cparams(("arbitrary", "arbitrary", "arbitrary")),
        name="ssd",
    )(proj, proj, proj, proj, proj, conv_w, conv_w, conv_w, cb2, cb2, cb2, conv_buf8, conv_buf8, conv_buf8,
      grp(dt_bias), grp(a_log), grp(ssm_d), sel.astype(BF16), expand.astype(BF16),
      ssm_norm_w.reshape(1, SSM_INNER), h0t)


def _merge_kernel(o_ref, y_ref, ga_ref, gb_ref, x_ref, g1_ref, sh2_ref, sc2_ref, nw_ref, wa_ref, wb_ref, wo_ref,
                  x1_ref, h2_ref):
    bb, tt, _ = x_ref.shape
    n = bb * tt
    pa = _dot(o_ref[...].reshape(n, D), wa_ref[...])
    pb = _dot(y_ref[...].reshape(n, SSM_INNER), wb_ref[...])
    merged = (jax.nn.sigmoid(ga_ref[...].reshape(n, D)) * pa + jax.nn.sigmoid(gb_ref[...].reshape(n, D)) * pb)
    mix = _dot(merged.astype(BF16), wo_ref[...]).reshape(bb, tt, D)
    x1 = x_ref[...] + g1_ref[...] * mix
    x1_ref[...] = x1
    xn = x1 * lax.rsqrt(jnp.mean(x1 * x1, axis=-1, keepdims=True) + EPS)
    h2_ref[...] = (xn * nw_ref[...] * (1.0 + sc2_ref[...]) + sh2_ref[...]).astype(BF16)


def _merge(o_g, y_g, proj, x, mod3, norm2_w, wa, wb, wo, bb, tt):
    b, t, _ = x.shape
    row = lambda i, j: (i, j, 0)
    modc = lambda c: (lambda i, j: (i, 0, c))
    const = lambda i, j: (0, 0)
    return pl.pallas_call(
        _merge_kernel,
        grid=(b // bb, t // tt),
        in_specs=[pl.BlockSpec((bb, tt, D), row),
                  pl.BlockSpec((bb, tt, SSM_INNER), row),
                  pl.BlockSpec((bb, tt, D), lambda i, j: (i, j, COL_GA // D)),
                  pl.BlockSpec((bb, tt, D), lambda i, j: (i, j, COL_GB // D)),
                  pl.BlockSpec((bb, tt, D), row),
                  pl.BlockSpec((bb, 1, D), modc(2)),
                  pl.BlockSpec((bb, 1, D), modc(3)),
                  pl.BlockSpec((bb, 1, D), modc(4)),
                  pl.BlockSpec((1, D), const),
                  pl.BlockSpec((D, D), const),
                  pl.BlockSpec((SSM_INNER, D), const),
                  pl.BlockSpec((D, D), const)],
        out_specs=[pl.BlockSpec((bb, tt, D), row), pl.BlockSpec((bb, tt, D), row)],
        out_shape=[jax.ShapeDtypeStruct((b, t, D), F32), jax.ShapeDtypeStruct((b, t, D), BF16)],
        compiler_params=_cparams(("arbitrary", "arbitrary")),
        name="merge",
    )(o_g, y_g, proj, proj, x, mod3, mod3, mod3, norm2_w.reshape(1, D), wa, wb, wo)


_NO_ROW = float(1 << 20)


def _tree(op, xs):
    xs = list(xs)
    while len(xs) > 1:
        xs = [op(xs[i], xs[i + 1]) for i in range(0, len(xs) - 1, 2)] + ([xs[-1]] if len(xs) % 2 else [])
    return xs[0]


def _topk_list(vals, ids, k):
    out_v, out_i = [], []
    for _ in range(k):
        m = _tree(jnp.maximum, vals)
        i = _tree(jnp.minimum, [jnp.where(v == m, c, _NO_ROW) for v, c in zip(vals, ids)])
        out_v.append(m)
        out_i.append(i)
        vals = [jnp.where(i == c, -jnp.inf, v) for v, c in zip(vals, ids)]
    return out_v, out_i


def _pick_list(table, sel):
    out = table[0]
    for i in range(1, len(table)):
        out = jnp.where(sel == float(i), table[i], out)
    return out


def _pair_candidates(v1, v2):
    kk = PEER_TOPK
    vals, ids = [], []
    for i in range(kk):
        for j in range(kk):
            if (i + 1) * (j + 1) <= kk:
                vals.append(v1[i] + v2[j])
                ids.append(float(i * kk + j))
    return vals, ids


ROUTE_PITCH = LANES + 8


def _route_kernel(h2_ref, wq_ref, k1_ref, k2_ref, e1_ref, e2_ref, gw_ref, q_ref, s1_ref, s2_ref):
    tm = h2_ref.shape[0]
    groups = tm // LANES
    half = PEER_DK // 2
    q = _dot(h2_ref[...], wq_ref[...])
    for c in range(2 * PEER_HEADS):
        q_ref[c] = q[:, c * half:(c + 1) * half].astype(BF16)
    key_ids = [float(k) for k in range(PEER_NK)]

    def per_key_scores(keys, qh, s_ref):
        for g in range(groups):
            s_ref[g * ROUTE_PITCH:g * ROUTE_PITCH + PEER_NK, :] = _dot_nt(keys, qh[g * LANES:(g + 1) * LANES, :])
        return [s_ref[pl.ds(k, groups, stride=ROUTE_PITCH), :] for k in range(PEER_NK)]

    def body(h, carry):
        v1, i1 = _topk_list(per_key_scores(k1_ref[h], q_ref[2 * h], s1_ref), key_ids, PEER_TOPK)
        v2, i2 = _topk_list(per_key_scores(k2_ref[h], q_ref[2 * h + 1], s2_ref), key_ids, PEER_TOPK)
        cand, flat = _pair_candidates(v1, v2)
        sv, sf = _topk_list(cand, flat, PEER_TOPK)
        p = [jnp.exp(v - sv[0]) for v in sv]
        inv = 1.0 / _tree(jnp.add, p)
        for k in range(PEER_TOPK):
            sel_i = jnp.floor(sf[k] * (1.0 / PEER_TOPK))
            sel_j = sf[k] - sel_i * float(PEER_TOPK)
            slot = pl.ds(h * PEER_TOPK + k, 1)
            e1_ref[:, slot, :] = _pick_list(i1, sel_i).astype(I32).reshape(groups, 1, LANES)
            e2_ref[:, slot, :] = _pick_list(i2, sel_j).astype(I32).reshape(groups, 1, LANES)
            gw_ref[:, slot, :] = (p[k] * inv).reshape(groups, 1, LANES)
        return carry

    lax.fori_loop(0, PEER_HEADS, body, 0)


def _route(h2, wq, k1, k2, tm):
    n = h2.shape[0]
    groups = tm // LANES
    out = jax.ShapeDtypeStruct((n // LANES, PEER_HK, LANES), I32)
    ospec = pl.BlockSpec((groups, PEER_HK, LANES), lambda i: (i, 0, 0))
    const3 = lambda i: (0, 0, 0)
    return pl.pallas_call(
        _route_kernel,
        grid=(n // tm,),
        in_specs=[pl.BlockSpec((tm, D), lambda i: (i, 0)),
                  pl.BlockSpec((D, PEER_HEADS * PEER_DK), lambda i: (0, 0)),
                  pl.BlockSpec((PEER_HEADS, PEER_NK, PEER_DK // 2), const3),
                  pl.BlockSpec((PEER_HEADS, PEER_NK, PEER_DK // 2), const3)],
        out_specs=[ospec, ospec, ospec],
        out_shape=[out, out, jax.ShapeDtypeStruct((n // LANES, PEER_HK, LANES), F32)],
        scratch_shapes=[pltpu.VMEM((2 * PEER_HEADS, tm, PEER_DK // 2), BF16),
                        pltpu.VMEM((groups * ROUTE_PITCH, LANES), F32),
                        pltpu.VMEM((groups * ROUTE_PITCH, LANES), F32)],
        compiler_params=_cparams(("arbitrary",)),
        name="route",
    )(h2, wq, k1, k2)


EXPERT_AB = 8
W_PAD = 8
SCATTER_UNROLL = 32


def _experts_kernel(h2_ref, e1t_ref, e2t_ref, gwt_ref, u_ref, v_ref, x1_ref, g2_ref, fw_ref, y_ref,
                    e1_ref, e2_ref, gw_ref, gact_ref, coef_ref, w_ref, acc_ref, act_ref):
    tm = h2_ref.shape[0]
    phase = pl.program_id(2)
    j = pl.program_id(3)
    nj = pl.num_programs(3)

    def select_from_act(step):
        e1 = e1_ref[...]
        e2 = e2_ref[...]
        g = jnp.zeros((tm, PEER_HK), F32)
        for a in range(EXPERT_AB):
            picked = jnp.take_along_axis(act_ref[:, a * LANES:(a + 1) * LANES], e2, axis=1,
                                         mode="promise_in_bounds")
            g = jnp.where(e1 == step * EXPERT_AB + a, picked, g)
        return g

    @pl.when((phase == 0) & (j == 0))
    def _():
        for r in range(tm // LANES):
            rows = slice(r * LANES, (r + 1) * LANES)
            e1_ref[rows, :] = e1t_ref[r].astype(F32).T.astype(I32)
            e2_ref[rows, :] = e2t_ref[r].astype(F32).T.astype(I32)
            gw_ref[rows, :] = gwt_ref[r].T
        gact_ref[...] = jnp.zeros_like(gact_ref)
        act_ref[...] = jnp.zeros_like(act_ref)

    @pl.when(phase == 0)
    def _():
        gact_ref[...] += select_from_act(j - 1)
        act_ref[...] = _dot_nt(h2_ref[...], u_ref[...])

    @pl.when((phase == 1) & (j == 0))
    def _():
        ga = gact_ref[...] + select_from_act(nj - 1)
        gelu = 0.5 * ga * (1.0 + lax.erf(ga * (0.5 ** 0.5)))
        coef_ref[...] = gw_ref[...] * gelu
        acc_ref[...] = jnp.zeros_like(acc_ref)

    half_steps = nj // 2
    a_half = PEER_NK // 2
    pitch = tm + W_PAD

    @pl.when((phase == 1) & (j % half_steps == 0))
    def _():
        a0 = (j // half_steps) * a_half
        rows_a = lax.broadcasted_iota(I32, (a_half, PEER_HK), 0) + a0
        rows_b = lax.broadcasted_iota(I32, (PEER_NK, PEER_HK), 0)

        def scatter(t, carry):
            ct = jnp.where(rows_a == e1_ref[pl.ds(t, 1), :], coef_ref[pl.ds(t, 1), :], 0.0).astype(BF16)
            bt = (rows_b == e2_ref[pl.ds(t, 1), :]).astype(BF16)
            w_ref[pl.ds(t, a_half, stride=pitch), :] = _dot_nt(ct, bt)
            return carry

        lax.fori_loop(0, tm, scatter, 0, unroll=SCATTER_UNROLL)

    @pl.when(phase == 1)
    def _():
        a_loc = (j % half_steps) * EXPERT_AB
        parts = [w_ref[pl.ds(pl.multiple_of((a_loc + a) * pitch, 8), tm), :].astype(BF16) for a in range(EXPERT_AB)]
        acc_ref[...] += _dot(jnp.concatenate(parts, axis=1), v_ref[...])

    @pl.when((phase == 1) & (j == nj - 1))
    def _():
        bb, tt, _ = x1_ref.shape
        x2 = x1_ref[...] + g2_ref[...] * acc_ref[...].reshape(bb, tt, D)
        y_ref[...] = x2 * lax.rsqrt(jnp.mean(x2 * x2, axis=-1, keepdims=True) + EPS) * fw_ref[...]


def _experts(h2, e1t, e2t, gwt, u, v, x1, mod3, final_w, bb, tt):
    b, t, _ = x1.shape
    tm = bb * tt
    nj = PEER_NK // EXPERT_AB
    eb = EXPERT_AB * PEER_NK
    nt = t // tt
    tspec = pl.BlockSpec((tm // LANES, PEER_HK, LANES), lambda i, k, p, j: (i * nt + k, 0, 0))
    rspec = pl.BlockSpec((bb, tt, D), lambda i, k, p, j: (i, k, 0))
    return pl.pallas_call(
        _experts_kernel,
        grid=(b // bb, nt, 2, nj),
        in_specs=[pl.BlockSpec((tm, D), lambda i, k, p, j: (i * nt + k, 0)), tspec, tspec, tspec,
                  pl.BlockSpec((eb, D), lambda i, k, p, j: (jnp.where(p == 0, j, nj - 1), 0)),
                  pl.BlockSpec((eb, D), lambda i, k, p, j: (jnp.where(p == 1, j, 0), 0)),
                  rspec,
                  pl.BlockSpec((bb, 1, D), lambda i, k, p, j: (i, 0, 5)),
                  pl.BlockSpec((1, D), lambda i, k, p, j: (0, 0))],
        out_specs=rspec,
        out_shape=jax.ShapeDtypeStruct((b, t, D), F32),
        scratch_shapes=[pltpu.VMEM((tm, PEER_HK), I32),
                        pltpu.VMEM((tm, PEER_HK), I32),
                        pltpu.VMEM((tm, PEER_HK), F32),
                        pltpu.VMEM((tm, PEER_HK), F32),
                        pltpu.VMEM((tm, PEER_HK), F32),
                        pltpu.VMEM(((PEER_NK // 2) * (tm + W_PAD), PEER_NK), F32),
                        pltpu.VMEM((tm, D), F32),
                        pltpu.VMEM((tm, EXPERT_AB * LANES), F32)],
        compiler_params=_cparams(("arbitrary", "arbitrary", "arbitrary", "arbitrary")),
        name="experts",
    )(h2, e1t, e2t, gwt, u, v, x1, mod3, final_w.reshape(1, D))


def _tile(total, cap):
    t = min(total, cap)
    assert total % t == 0
    return t


def _group(x, mod, s_hg, s_ssm, conv_buf, lb, w, gla_chunk, ssd_chunk):
    b, t, _ = x.shape
    n = b * t
    mod3 = mod.reshape(b, 1, N_MOD * D)
    if t >= 256:
        bb, tt = 1, _tile(t, 1024)
    else:
        bb, tt = b, t
    proj = _inproj(x, mod3, w["norm1_w"], w["w_in"], bb, tt)

    o_g, s_hg_new = _gla(proj, lb, w["hgrn_norm_w"], s_hg, _tile(t, 512), gla_chunk)

    conv_buf8 = jnp.pad(conv_buf, ((0, 0), (5, 0), (0, 0)))
    h0t = jnp.swapaxes(s_ssm.reshape(b, SSM_G, SSM_R * SSM_P, SSM_N), 2, 3)
    y_g, h_t = _ssd(proj, w["conv_w"], w["conv_b"], conv_buf8, w["dt_bias"], w["a_log"], w["ssm_d"],
                    w["ssm_norm_w"], h0t, ssd_chunk)
    s_ssm_new = jnp.swapaxes(h_t, 2, 3).reshape(b, SSM_HEADS, SSM_P, SSM_N)
    xbc = proj[:, :, COL_XS:COL_XS + SSM_XBC]
    assert t >= 3
    conv_new = xbc[:, t - 3:]

    if t >= 256:
        bb2, tt2 = 1, _tile(t, 256)
    else:
        bb2, tt2 = b, t
    x1, h2 = _merge(o_g, y_g, proj, x, mod3, w["norm2_w"], w["w_branch_a"], w["w_branch_b"], w["w_out"], bb2, tt2)

    h2f = h2.reshape(n, D)
    e1t, e2t, gwt = _route(h2f, w["peer_wq"], w["peer_keys1"], w["peer_keys2"], _tile(n, 1024))
    bb3, tt3 = (1, _tile(t, 512)) if t >= 256 else (b, t)
    y = _experts(h2f, e1t, e2t, gwt, w["peer_u"], w["peer_v"], x1, mod3, w["final_norm_w"], bb3, tt3)
    return y, s_hg_new, s_ssm_new, conv_new


def kernel(x_prompt, x_sample, c_prompt, c_sample, state_hgrn, state_ssm, state_conv, w_ada, b_ada, norm1_w, w_in,
           hgrn_lower_bounds, hgrn_norm_w, conv_w, conv_b, dt_bias, a_log, ssm_d, ssm_norm_w, w_branch_a,
           w_branch_b, w_out, norm2_w, peer_wq, peer_keys1, peer_keys2, peer_u, peer_v, final_norm_w):
    depth = w_ada.shape[0]
    assert depth == 1
    bp, tp, _ = x_prompt.shape
    bs, ts, _ = x_sample.shape
    lb_all = jnp.cumsum(jax.nn.softmax(hgrn_lower_bounds.astype(F32), axis=0), axis=0)

    l = 0
    wi = w_in[l]
    c_hg, c_z, c_xbc, c_dt = 4096, 6144, 9216, 9248
    w_pad = jnp.concatenate(
        [wi[:, :c_xbc], wi[:, c_dt:], wi[:, c_xbc:c_dt],
         jnp.zeros((D, PROJ_COLS - COL_DT - SSM_HEADS), wi.dtype)], axis=1).astype(BF16)
    w = dict(norm1_w=norm1_w[l], w_in=w_pad, hgrn_norm_w=hgrn_norm_w[l], conv_w=conv_w[l], conv_b=conv_b[l],
             dt_bias=dt_bias[l], a_log=a_log[l], ssm_d=ssm_d[l], ssm_norm_w=ssm_norm_w[l],
             w_branch_a=w_branch_a[l].astype(BF16), w_branch_b=w_branch_b[l].astype(BF16),
             w_out=w_out[l].astype(BF16), norm2_w=norm2_w[l], peer_wq=peer_wq[l].astype(BF16),
             peer_keys1=peer_keys1[l].astype(BF16), peer_keys2=peer_keys2[l].astype(BF16),
             peer_u=peer_u[l].astype(BF16), peer_v=peer_v[l].astype(BF16), final_norm_w=final_norm_w)

    mod = _adaln(jnp.concatenate([c_prompt, c_sample], axis=0), w_ada[l], b_ada[l])

    z_hg = jnp.zeros((bp, HG_HEADS, HG_DK, LANES), F32)
    z_ssm = jnp.zeros((bp, SSM_HEADS, SSM_P, SSM_N), F32)
    z_conv = jnp.zeros((bp, 3, SSM_XBC), F32)
    yp, hg_p, ssm_p, conv_p = _group(x_prompt, mod[:bp], z_hg, z_ssm, z_conv, lb_all[l], w,
                                     min(GLA_CHUNK, tp), min(SSD_CHUNK, tp))
    ys, hg_s, ssm_s, conv_s = _group(x_sample, mod[bp:], state_hgrn[l], state_ssm[l], state_conv[l], lb_all[l], w,
                                     min(GLA_CHUNK, ts), min(SSD_CHUNK, ts))
    return (yp, ys, hg_p[None], ssm_p[None], conv_p[None], hg_s[None], ssm_s[None], conv_s[None])
```

```python
import functools

import jax
import jax.numpy as jnp
from jax import lax
from jax.experimental import pallas as pl
from jax.experimental.pallas import tpu as pltpu

F32 = jnp.float32
BF16 = jnp.bfloat16
I32 = jnp.int32

EPS = 1e-6
D = 1024
N_MOD = 6
HG_HEADS = 8
HG_DK = 128
SSM_HEADS = 32
SSM_P = 64
SSM_G = 4
SSM_R = SSM_HEADS // SSM_G
SSM_N = 128
SSM_INNER = SSM_HEADS * SSM_P
SSM_GW = SSM_INNER // SSM_G
SSM_XBC = SSM_INNER + 2 * SSM_G * SSM_N
PEER_HEADS = 8
PEER_NK = 128
PEER_TOPK = 16
PEER_HK = PEER_HEADS * PEER_TOPK
PEER_DK = 256

LANES = 128
GLA_CHUNK = 64
SSD_CHUNK = 256
VMEM_LIMIT = 56 * 1024 * 1024

COL_HG = 0
COL_Z = 4096
COL_XS = 6144
COL_B = COL_XS + SSM_INNER
COL_C = COL_B + SSM_G * SSM_N
COL_GA = 9216
COL_GB = 10240
COL_DT = 11264
PROJ_COLS = 11520
PROJ_TN = 1280


def _cparams(sem):
    return pltpu.CompilerParams(dimension_semantics=sem, vmem_limit_bytes=VMEM_LIMIT)


def _silu(x):
    return x * jax.nn.sigmoid(x)


def _split3(x):
    hi = x.astype(BF16)
    r = x - hi.astype(F32)
    mid = r.astype(BF16)
    lo = (r - mid.astype(F32)).astype(BF16)
    return hi, mid, lo


def _dot(a, b):
    return jnp.dot(a, b, preferred_element_type=F32)


def _dot_nt(a, b):
    return lax.dot_general(a, b, (((1,), (1,)), ((), ())), preferred_element_type=F32)


def _dot_tn(a, b):
    return lax.dot_general(a, b, (((0,), (0,)), ((), ())), preferred_element_type=F32)


def _dot3_left(m_bf16, x):
    hi, mid, lo = _split3(x)
    return _dot(m_bf16, hi) + _dot(m_bf16, mid) + _dot(m_bf16, lo)


def _dot3_right(x, m_bf16):
    hi, mid, lo = _split3(x)
    return _dot(hi, m_bf16) + _dot(mid, m_bf16) + _dot(lo, m_bf16)


def _chunk_cumsum(x, chunk):
    pos = lax.broadcasted_iota(I32, x.shape, 0) % chunk
    s = 1
    while s < chunk:
        x = x + jnp.where(pos >= s, pltpu.roll(x, shift=s, axis=0), 0.0)
        s *= 2
    return x


def _tril(n):
    r = lax.broadcasted_iota(I32, (n, n), 0)
    c = lax.broadcasted_iota(I32, (n, n), 1)
    return r >= c


def _adaln_kernel(c_ref, w_ref, b_ref, o_ref):
    c = c_ref[...]
    o_ref[...] = _dot(_silu(c).astype(BF16), w_ref[...].astype(BF16)) + b_ref[...]


def _adaln(c, w_ada, b_ada):
    nb = c.shape[0]
    ncol = w_ada.shape[1]
    return pl.pallas_call(
        _adaln_kernel,
        grid=(ncol // D,),
        in_specs=[pl.BlockSpec((nb, D), lambda j: (0, 0)),
                  pl.BlockSpec((D, D), lambda j: (0, j)),
                  pl.BlockSpec((1, D), lambda j: (0, j))],
        out_specs=pl.BlockSpec((nb, D), lambda j: (0, j)),
        out_shape=jax.ShapeDtypeStruct((nb, ncol), F32),
        compiler_params=_cparams(("arbitrary",)),
        name="adaln",
    )(c, w_ada, b_ada.reshape(1, ncol))


def _inproj_kernel(x_ref, sh_ref, sc_ref, nw_ref, w_ref, o_ref, h_ref):
    bb, tt, _ = x_ref.shape

    @pl.when(pl.program_id(2) == 0)
    def _():
        x = x_ref[...]
        xn = x * lax.rsqrt(jnp.mean(x * x, axis=-1, keepdims=True) + EPS)
        h = xn * nw_ref[...] * (1.0 + sc_ref[...]) + sh_ref[...]
        h_ref[...] = h.reshape(bb * tt, D).astype(BF16)

    o_ref[...] = _dot(h_ref[...], w_ref[...]).reshape(bb, tt, -1)


def _inproj(x, mod3, norm_w, w_pad, bb, tt):
    b, t, _ = x.shape
    return pl.pallas_call(
        _inproj_kernel,
        grid=(b // bb, t // tt, PROJ_COLS // PROJ_TN),
        in_specs=[pl.BlockSpec((bb, tt, D), lambda i, j, n: (i, j, 0)),
                  pl.BlockSpec((bb, 1, D), lambda i, j, n: (i, 0, 0)),
                  pl.BlockSpec((bb, 1, D), lambda i, j, n: (i, 0, 1)),
                  pl.BlockSpec((1, D), lambda i, j, n: (0, 0)),
                  pl.BlockSpec((D, PROJ_TN), lambda i, j, n: (0, n))],
        out_specs=pl.BlockSpec((bb, tt, PROJ_TN), lambda i, j, n: (i, j, n)),
        out_shape=jax.ShapeDtypeStruct((b, t, PROJ_COLS), F32),
        scratch_shapes=[pltpu.VMEM((bb * tt, D), BF16)],
        compiler_params=_cparams(("arbitrary", "arbitrary", "arbitrary")),
        name="inproj",
    )(x, mod3, mod3, norm_w.reshape(1, D), w_pad)


def _gla_kernel(q_ref, f_ref, i_ref, g_ref, lb_ref, nw_ref, s0_ref, o_ref, s_out_ref, s_ref, *, chunk):
    tt = q_ref.shape[1]
    nchunk = tt // chunk
    t_idx = pl.program_id(2)

    @pl.when(t_idx == 0)
    def _():
        s_ref[...] = s0_ref[0, 0].T

    lb = lb_ref[...]
    tril = _tril(chunk)
    mid = chunk // 2

    q_all = _silu(q_ref[0])
    ff = f_ref[0]
    logf = jnp.log(lb + (1.0 - lb) * jax.nn.sigmoid(ff))
    k_all = (1.0 - lb) * jax.nn.sigmoid(-ff)
    v_all = i_ref[0].astype(BF16)
    g_all = _chunk_cumsum(logf, chunk)

    st = s_ref[...]
    outs = []
    for c in range(nchunk):
        sl = slice(c * chunk, (c + 1) * chunk)
        q, k, v, g = q_all[sl], k_all[sl], v_all[sl], g_all[sl]
        g_mid = g[mid:mid + 1, :]
        g_last = g[chunk - 1:chunk, :]
        qs = (q * jnp.exp(g - g_mid)).astype(BF16)
        ks = (k * jnp.exp(g_mid - g)).astype(BF16)
        att = jnp.where(tril, _dot_nt(qs, ks), 0.0).astype(BF16)
        outs.append(_dot(att, v) + _dot_nt((q * jnp.exp(g)).astype(BF16), st.astype(BF16)))
        kd = (k * jnp.exp(g_last - g)).astype(BF16)
        st = jnp.exp(g_last) * st + _dot_tn(v, kd)
    s_ref[...] = st
    o = jnp.concatenate(outs, axis=0) if nchunk > 1 else outs[0]
    on = o * lax.rsqrt(jnp.mean(o * o, axis=-1, keepdims=True) + EPS) * nw_ref[...]
    o_ref[0] = (on * _silu(g_ref[0])).astype(BF16)

    @pl.when(t_idx == pl.num_programs(2) - 1)
    def _():
        s_out_ref[0, 0] = s_ref[...].T


def _gla(proj, lb, hg_norm_w, s0, tt, chunk):
    b, t, _ = proj.shape
    col = lambda base: (lambda i, h, j: (i, j, base // LANES + h))
    return pl.pallas_call(
        functools.partial(_gla_kernel, chunk=chunk),
        grid=(b, HG_HEADS, t // tt),
        in_specs=[pl.BlockSpec((1, tt, LANES), col(COL_HG)),
                  pl.BlockSpec((1, tt, LANES), col(COL_HG + 1024)),
                  pl.BlockSpec((1, tt, LANES), col(COL_HG + 2048)),
                  pl.BlockSpec((1, tt, LANES), col(COL_HG + 3072)),
                  pl.BlockSpec((1, LANES), lambda i, h, j: (0, h)),
                  pl.BlockSpec((1, LANES), lambda i, h, j: (0, h)),
                  pl.BlockSpec((1, 1, HG_DK, LANES), lambda i, h, j: (i, h, 0, 0))],
        out_specs=[pl.BlockSpec((1, tt, LANES), lambda i, h, j: (i, j, h)),
                   pl.BlockSpec((1, 1, HG_DK, LANES), lambda i, h, j: (i, h, 0, 0))],
        out_shape=[jax.ShapeDtypeStruct((b, t, D), BF16),
                   jax.ShapeDtypeStruct((b, HG_HEADS, HG_DK, LANES), F32)],
        scratch_shapes=[pltpu.VMEM((HG_DK, LANES), F32)],
        compiler_params=_cparams(("arbitrary", "arbitrary", "arbitrary")),
        name="gla",
    )(proj, proj, proj, proj, lb.reshape(1, D), hg_norm_w.reshape(1, D), s0)


def _ssd_kernel(xs_ref, b_ref, c_ref, z_ref, dt_ref, cwx_ref, cwb_ref, cwc_ref, cbx_ref, cbb_ref, cbc_ref,
                bufx_ref, bufb_ref, bufc_ref, dtb_ref, alog_ref, dpar_ref, sel_ref, exp_ref, nw_ref, h0_ref,
                y_ref, h_out_ref, h_ref, xpx_ref, xpb_ref, xpc_ref):
    L = xs_ref.shape[1]
    t_idx = pl.program_id(2)

    @pl.when(t_idx == 0)
    def _():
        h_ref[...] = h0_ref[0, 0]
        xpx_ref[0:8, :] = bufx_ref[0]
        xpb_ref[0:8, :] = bufb_ref[0]
        xpc_ref[0:8, :] = bufc_ref[0]

    def conv(xp_ref, raw, w_ref, bias_ref):
        xp_ref[8:8 + L, :] = raw
        acc = bias_ref[...] + xp_ref[5:5 + L, :] * w_ref[0:1, :]
        for j in range(1, 4):
            acc = acc + xp_ref[5 + j:5 + j + L, :] * w_ref[j:j + 1, :]
        xp_ref[0:8, :] = xp_ref[L:L + 8, :]
        return _silu(acc)

    xs = conv(xpx_ref, xs_ref[0], cwx_ref, cbx_ref)
    bm = conv(xpb_ref, b_ref[0], cwb_ref, cbb_ref)
    cm = conv(xpc_ref, c_ref[0], cwc_ref, cbc_ref)
    bm_bf = bm.astype(BF16)
    cm_bf = cm.astype(BF16)

    sel = sel_ref[0]
    expand = exp_ref[...]
    dt_raw = _dot3_right(dt_ref[0], sel)
    dt = jax.nn.softplus(dt_raw + dtb_ref[0])
    a_neg = -jnp.exp(alog_ref[0])
    tril = _tril(L)
    ac = _dot3_left(tril.astype(BF16), dt * a_neg)
    a_last = ac[L - 1:L, :]

    def expand2(v):
        hi = v.astype(BF16)
        lo = (v - hi.astype(F32)).astype(BF16)
        return _dot(hi, expand) + _dot(lo, expand)

    dt_x = expand2(dt)
    eac_x = expand2(jnp.exp(ac))
    wd_x = expand2(jnp.exp(a_last - ac) * dt)
    row8 = lax.broadcasted_iota(I32, (8, LANES), 0)
    tail_x = expand2(jnp.where(row8 == 0, jnp.exp(a_last), jnp.where(row8 == 1, dpar_ref[0], 0.0)))
    decay_x = tail_x[0:1, :]
    d_x = tail_x[1:2, :]

    xdt = (xs * dt_x).astype(BF16)
    cbm = jnp.where(tril, _dot_nt(cm_bf, bm_bf), 0.0)
    ac_t = ac.T
    lane = lax.broadcasted_iota(I32, (L, LANES), 1)
    zero = jnp.zeros((), BF16)
    ys = []
    for pair in range(SSM_R // 2):
        xq = xdt[:, pair * LANES:(pair + 1) * LANES]
        acc = None
        for half in range(2):
            r = 2 * pair + half
            seg = jnp.exp(jnp.minimum(ac[:, r:r + 1] - ac_t[r:r + 1, :], 0.0))
            w = (cbm * seg).astype(BF16)
            keep = (lane < SSM_P) if half == 0 else (lane >= SSM_P)
            part = _dot(w, jnp.where(keep, xq, zero))
            acc = part if acc is None else acc + part
        ys.append(acc)
    h_old = h_ref[...]
    y = jnp.concatenate(ys, axis=1) + _dot(cm_bf, h_old.astype(BF16)) * eac_x
    xw = (xs * wd_x).astype(BF16)
    h_ref[...] = decay_x * h_old + _dot_tn(bm_bf, xw)

    y = (y + d_x * xs) * _silu(z_ref[0])
    y = y * lax.rsqrt(jnp.mean(y * y, axis=-1, keepdims=True) + EPS) * nw_ref[...]
    y_ref[0] = y.astype(BF16)

    @pl.when(t_idx == pl.num_programs(2) - 1)
    def _():
        h_out_ref[0, 0] = h_ref[...]


def _ssd(proj, conv_w, conv_b, conv_buf8, dt_bias, a_log, ssm_d, ssm_norm_w, h0t, chunk):
    b, t, _ = proj.shape
    L = chunk
    grp = lambda v: jnp.pad(v.reshape(SSM_G, 1, SSM_R), ((0, 0), (0, 0), (0, LANES - SSM_R)))
    hh = jnp.arange(LANES)
    sel = (hh[None, :, None] == (jnp.arange(SSM_G)[:, None, None] * SSM_R + hh[None, None, :])) & (hh[None, None, :] < SSM_R)
    expand = (hh[:, None] == (jnp.arange(SSM_GW)[None, :] // SSM_P))
    cb2 = conv_b.reshape(1, SSM_XBC)
    nb_x, nb_b, nb_c = 0, SSM_INNER // LANES, (SSM_INNER + SSM_G * SSM_N) // LANES
    pcol = lambda base, width: (lambda i, g, j: (i, j, base // width + g))
    return pl.pallas_call(
        _ssd_kernel,
        grid=(b, SSM_G, t // L),
        in_specs=[pl.BlockSpec((1, L, SSM_GW), pcol(COL_XS, SSM_GW)),
                  pl.BlockSpec((1, L, SSM_N), pcol(COL_B, SSM_N)),
                  pl.BlockSpec((1, L, SSM_N), pcol(COL_C, SSM_N)),
                  pl.BlockSpec((1, L, SSM_GW), pcol(COL_Z, SSM_GW)),
                  pl.BlockSpec((1, L, LANES), lambda i, g, j: (i, j, COL_DT // LANES)),
                  pl.BlockSpec((4, SSM_GW), lambda i, g, j: (0, g)),
                  pl.BlockSpec((4, SSM_N), lambda i, g, j: (0, nb_b + g)),
                  pl.BlockSpec((4, SSM_N), lambda i, g, j: (0, nb_c + g)),
                  pl.BlockSpec((1, SSM_GW), lambda i, g, j: (0, g)),
                  pl.BlockSpec((1, SSM_N), lambda i, g, j: (0, nb_b + g)),
                  pl.BlockSpec((1, SSM_N), lambda i, g, j: (0, nb_c + g)),
                  pl.BlockSpec((1, 8, SSM_GW), lambda i, g, j: (i, 0, g)),
                  pl.BlockSpec((1, 8, SSM_N), lambda i, g, j: (i, 0, nb_b + g)),
                  pl.BlockSpec((1, 8, SSM_N), lambda i, g, j: (i, 0, nb_c + g)),
                  pl.BlockSpec((1, 1, LANES), lambda i, g, j: (g, 0, 0)),
                  pl.BlockSpec((1, 1, LANES), lambda i, g, j: (g, 0, 0)),
                  pl.BlockSpec((1, 1, LANES), lambda i, g, j: (g, 0, 0)),
                  pl.BlockSpec((1, LANES, LANES), lambda i, g, j: (g, 0, 0)),
                  pl.BlockSpec((LANES, SSM_GW), lambda i, g, j: (0, 0)),
                  pl.BlockSpec((1, SSM_GW), lambda i, g, j: (0, g)),
                  pl.BlockSpec((1, 1, SSM_N, SSM_GW), lambda i, g, j: (i, g, 0, 0))],
        out_specs=[pl.BlockSpec((1, L, SSM_GW), lambda i, g, j: (i, j, g)),
                   pl.BlockSpec((1, 1, SSM_N, SSM_GW), lambda i, g, j: (i, g, 0, 0))],
        out_shape=[jax.ShapeDtypeStruct((b, t, SSM_INNER), BF16),
                   jax.ShapeDtypeStruct((b, SSM_G, SSM_N, SSM_GW), F32)],
        scratch_shapes=[pltpu.VMEM((SSM_N, SSM_GW), F32),
                        pltpu.VMEM((L + 8, SSM_GW), F32),
                        pltpu.VMEM((L + 8, SSM_N), F32),
                        pltpu.VMEM((L + 8, SSM_N), F32)],
        compiler_params=_cparams(("arbitrary", "arbitrary", "arbitrary")),
        name="ssd",
    )(proj, proj, proj, proj, proj, conv_w, conv_w, conv_w, cb2, cb2, cb2, conv_buf8, conv_buf8, conv_buf8,
      grp(dt_bias), grp(a_log), grp(ssm_d), sel.astype(BF16), expand.astype(BF16),
      ssm_norm_w.reshape(1, SSM_INNER), h0t)


def _merge_kernel(o_ref, y_ref, ga_ref, gb_ref, x_ref, g1_ref, sh2_ref, sc2_ref, nw_ref, wa_ref, wb_ref, wo_ref,
                  x1_ref, h2_ref):
    bb, tt, _ = x_ref.shape
    n = bb * tt
    pa = _dot(o_ref[...].reshape(n, D), wa_ref[...])
    pb = _dot(y_ref[...].reshape(n, SSM_INNER), wb_ref[...])
    merged = (jax.nn.sigmoid(ga_ref[...].reshape(n, D)) * pa + jax.nn.sigmoid(gb_ref[...].reshape(n, D)) * pb)
    mix = _dot(merged.astype(BF16), wo_ref[...]).reshape(bb, tt, D)
    x1 = x_ref[...] + g1_ref[...] * mix
    x1_ref[...] = x1
    xn = x1 * lax.rsqrt(jnp.mean(x1 * x1, axis=-1, keepdims=True) + EPS)
    h2_ref[...] = (xn * nw_ref[...] * (1.0 + sc2_ref[...]) + sh2_ref[...]).astype(BF16)


def _merge(o_g, y_g, proj, x, mod3, norm2_w, wa, wb, wo, bb, tt):
    b, t, _ = x.shape
    row = lambda i, j: (i, j, 0)
    modc = lambda c: (lambda i, j: (i, 0, c))
    const = lambda i, j: (0, 0)
    return pl.pallas_call(
        _merge_kernel,
        grid=(b // bb, t // tt),
        in_specs=[pl.BlockSpec((bb, tt, D), row),
                  pl.BlockSpec((bb, tt, SSM_INNER), row),
                  pl.BlockSpec((bb, tt, D), lambda i, j: (i, j, COL_GA // D)),
                  pl.BlockSpec((bb, tt, D), lambda i, j: (i, j, COL_GB // D)),
                  pl.BlockSpec((bb, tt, D), row),
                  pl.BlockSpec((bb, 1, D), modc(2)),
                  pl.BlockSpec((bb, 1, D), modc(3)),
                  pl.BlockSpec((bb, 1, D), modc(4)),
                  pl.BlockSpec((1, D), const),
                  pl.BlockSpec((D, D), const),
                  pl.BlockSpec((SSM_INNER, D), const),
                  pl.BlockSpec((D, D), const)],
        out_specs=[pl.BlockSpec((bb, tt, D), row), pl.BlockSpec((bb, tt, D), row)],
        out_shape=[jax.ShapeDtypeStruct((b, t, D), F32), jax.ShapeDtypeStruct((b, t, D), BF16)],
        compiler_params=_cparams(("arbitrary", "arbitrary")),
        name="merge",
    )(o_g, y_g, proj, proj, x, mod3, mod3, mod3, norm2_w.reshape(1, D), wa, wb, wo)


_NO_ROW = float(1 << 20)


def _tree(op, xs):
    xs = list(xs)
    while len(xs) > 1:
        xs = [op(xs[i], xs[i + 1]) for i in range(0, len(xs) - 1, 2)] + ([xs[-1]] if len(xs) % 2 else [])
    return xs[0]


def _topk_list(vals, ids, k):
    out_v, out_i = [], []
    for _ in range(k):
        m = _tree(jnp.maximum, vals)
        i = _tree(jnp.minimum, [jnp.where(v == m, c, _NO_ROW) for v, c in zip(vals, ids)])
        out_v.append(m)
        out_i.append(i)
        vals = [jnp.where(i == c, -jnp.inf, v) for v, c in zip(vals, ids)]
    return out_v, out_i


def _pick_list(table, sel):
    out = table[0]
    for i in range(1, len(table)):
        out = jnp.where(sel == float(i), table[i], out)
    return out


def _pair_candidates(v1, v2):
    kk = PEER_TOPK
    vals, ids = [], []
    for i in range(kk):
        for j in range(kk):
            if (i + 1) * (j + 1) <= kk:
                vals.append(v1[i] + v2[j])
                ids.append(float(i * kk + j))
    return vals, ids


ROUTE_PITCH = LANES + 8


def _route_kernel(h2_ref, wq_ref, k1_ref, k2_ref, e1_ref, e2_ref, gw_ref, q_ref, s1_ref, s2_ref):
    tm = h2_ref.shape[0]
    groups = tm // LANES
    half = PEER_DK // 2
    q = _dot(h2_ref[...], wq_ref[...])
    for c in range(2 * PEER_HEADS):
        q_ref[c] = q[:, c * half:(c + 1) * half].astype(BF16)
    key_ids = [float(k) for k in range(PEER_NK)]

    def per_key_scores(keys, qh, s_ref):
        for g in range(groups):
            s_ref[g * ROUTE_PITCH:g * ROUTE_PITCH + PEER_NK, :] = _dot_nt(keys, qh[g * LANES:(g + 1) * LANES, :])
        return [s_ref[pl.ds(k, groups, stride=ROUTE_PITCH), :] for k in range(PEER_NK)]

    def body(h, carry):
        v1, i1 = _topk_list(per_key_scores(k1_ref[h], q_ref[2 * h], s1_ref), key_ids, PEER_TOPK)
        v2, i2 = _topk_list(per_key_scores(k2_ref[h], q_ref[2 * h + 1], s2_ref), key_ids, PEER_TOPK)
        cand, flat = _pair_candidates(v1, v2)
        sv, sf = _topk_list(cand, flat, PEER_TOPK)
        p = [jnp.exp(v - sv[0]) for v in sv]
        inv = 1.0 / _tree(jnp.add, p)
        for k in range(PEER_TOPK):
            sel_i = jnp.floor(sf[k] * (1.0 / PEER_TOPK))
            sel_j = sf[k] - sel_i * float(PEER_TOPK)
            slot = pl.ds(h * PEER_TOPK + k, 1)
            e1_ref[:, slot, :] = _pick_list(i1, sel_i).astype(I32).reshape(groups, 1, LANES)
            e2_ref[:, slot, :] = _pick_list(i2, sel_j).astype(I32).reshape(groups, 1, LANES)
            gw_ref[:, slot, :] = (p[k] * inv).reshape(groups, 1, LANES)
        return carry

    lax.fori_loop(0, PEER_HEADS, body, 0)


def _route(h2, wq, k1, k2, tm):
    n = h2.shape[0]
    groups = tm // LANES
    out = jax.ShapeDtypeStruct((n // LANES, PEER_HK, LANES), I32)
    ospec = pl.BlockSpec((groups, PEER_HK, LANES), lambda i: (i, 0, 0))
    const3 = lambda i: (0, 0, 0)
    return pl.pallas_call(
        _route_kernel,
        grid=(n // tm,),
        in_specs=[pl.BlockSpec((tm, D), lambda i: (i, 0)),
                  pl.BlockSpec((D, PEER_HEADS * PEER_DK), lambda i: (0, 0)),
                  pl.BlockSpec((PEER_HEADS, PEER_NK, PEER_DK // 2), const3),
                  pl.BlockSpec((PEER_HEADS, PEER_NK, PEER_DK // 2), const3)],
        out_specs=[ospec, ospec, ospec],
        out_shape=[out, out, jax.ShapeDtypeStruct((n // LANES, PEER_HK, LANES), F32)],
        scratch_shapes=[pltpu.VMEM((2 * PEER_HEADS, tm, PEER_DK // 2), BF16),
                        pltpu.VMEM((groups * ROUTE_PITCH, LANES), F32),
                        pltpu.VMEM((groups * ROUTE_PITCH, LANES), F32)],
        compiler_params=_cparams(("arbitrary",)),
        name="route",
    )(h2, wq, k1, k2)


EXPERT_AB = 16
W_PAD = 8
SCATTER_UNROLL = 32


def _experts_kernel(h2_ref, e1t_ref, e2t_ref, gwt_ref, u_ref, v_ref, x1_ref, g2_ref, fw_ref, y_ref,
                    e1_ref, e2_ref, gw_ref, gact_ref, coef_ref, w_ref, acc_ref, act_ref):
    tm = h2_ref.shape[0]
    phase = pl.program_id(2)
    j = pl.program_id(3)
    nj = pl.num_programs(3)

    def select_from_act(step):
        e1 = e1_ref[...]
        e2 = e2_ref[...]
        g = jnp.zeros((tm, PEER_HK), F32)
        for a in range(EXPERT_AB):
            picked = jnp.take_along_axis(act_ref[:, a * LANES:(a + 1) * LANES], e2, axis=1,
                                         mode="promise_in_bounds")
            g = jnp.where(e1 == step * EXPERT_AB + a, picked, g)
        return g

    @pl.when((phase == 0) & (j == 0))
    def _():
        for r in range(tm // LANES):
            rows = slice(r * LANES, (r + 1) * LANES)
            e1_ref[rows, :] = e1t_ref[r].astype(F32).T.astype(I32)
            e2_ref[rows, :] = e2t_ref[r].astype(F32).T.astype(I32)
            gw_ref[rows, :] = gwt_ref[r].T
        gact_ref[...] = jnp.zeros_like(gact_ref)
        act_ref[...] = jnp.zeros_like(act_ref)

    @pl.when(phase == 0)
    def _():
        gact_ref[...] += select_from_act(j - 1)
        act_ref[...] = _dot_nt(h2_ref[...], u_ref[...])

    @pl.when((phase == 1) & (j == 0))
    def _():
        ga = gact_ref[...] + select_from_act(nj - 1)
        gelu = 0.5 * ga * (1.0 + lax.erf(ga * (0.5 ** 0.5)))
        coef_ref[...] = gw_ref[...] * gelu
        acc_ref[...] = jnp.zeros_like(acc_ref)

    half_steps = nj // 2
    a_half = PEER_NK // 2
    pitch = tm + W_PAD

    @pl.when((phase == 1) & (j % half_steps == 0))
    def _():
        a0 = (j // half_steps) * a_half
        rows_a = lax.broadcasted_iota(I32, (a_half, PEER_HK), 0) + a0
        rows_b = lax.broadcasted_iota(I32, (PEER_NK, PEER_HK), 0)

        def scatter(t, carry):
            ct = jnp.where(rows_a == e1_ref[pl.ds(t, 1), :], coef_ref[pl.ds(t, 1), :], 0.0).astype(BF16)
            bt = (rows_b == e2_ref[pl.ds(t, 1), :]).astype(BF16)
            w_ref[pl.ds(t, a_half, stride=pitch), :] = _dot_nt(ct, bt)
            return carry

        lax.fori_loop(0, tm, scatter, 0, unroll=SCATTER_UNROLL)

    @pl.when(phase == 1)
    def _():
        a_loc = (j % half_steps) * EXPERT_AB
        parts = [w_ref[pl.ds(pl.multiple_of((a_loc + a) * pitch, 8), tm), :].astype(BF16) for a in range(EXPERT_AB)]
        acc_ref[...] += _dot(jnp.concatenate(parts, axis=1), v_ref[...])

    @pl.when((phase == 1) & (j == nj - 1))
    def _():
        bb, tt, _ = x1_ref.shape
        x2 = x1_ref[...] + g2_ref[...] * acc_ref[...].reshape(bb, tt, D)
        y_ref[...] = x2 * lax.rsqrt(jnp.mean(x2 * x2, axis=-1, keepdims=True) + EPS) * fw_ref[...]


def _experts(h2, e1t, e2t, gwt, u, v, x1, mod3, final_w, bb, tt):
    b, t, _ = x1.shape
    tm = bb * tt
    nj = PEER_NK // EXPERT_AB
    eb = EXPERT_AB * PEER_NK
    nt = t // tt
    tspec = pl.BlockSpec((tm // LANES, PEER_HK, LANES), lambda i, k, p, j: (i * nt + k, 0, 0))
    rspec = pl.BlockSpec((bb, tt, D), lambda i, k, p, j: (i, k, 0))
    return pl.pallas_call(
        _experts_kernel,
        grid=(b // bb, nt, 2, nj),
        in_specs=[pl.BlockSpec((tm, D), lambda i, k, p, j: (i * nt + k, 0)), tspec, tspec, tspec,
                  pl.BlockSpec((eb, D), lambda i, k, p, j: (jnp.where(p == 0, j, nj - 1), 0)),
                  pl.BlockSpec((eb, D), lambda i, k, p, j: (jnp.where(p == 1, j, 0), 0)),
                  rspec,
                  pl.BlockSpec((bb, 1, D), lambda i, k, p, j: (i, 0, 5)),
                  pl.BlockSpec((1, D), lambda i, k, p, j: (0, 0))],
        out_specs=rspec,
        out_shape=jax.ShapeDtypeStruct((b, t, D), F32),
        scratch_shapes=[pltpu.VMEM((tm, PEER_HK), I32),
                        pltpu.VMEM((tm, PEER_HK), I32),
                        pltpu.VMEM((tm, PEER_HK), F32),
                        pltpu.VMEM((tm, PEER_HK), F32),
                        pltpu.VMEM((tm, PEER_HK), F32),
                        pltpu.VMEM(((PEER_NK // 2) * (tm + W_PAD), PEER_NK), F32),
                        pltpu.VMEM((tm, D), F32),
                        pltpu.VMEM((tm, EXPERT_AB * LANES), F32)],
        compiler_params=_cparams(("arbitrary", "arbitrary", "arbitrary", "arbitrary")),
        name="experts",
    )(h2, e1t, e2t, gwt, u, v, x1, mod3, final_w.reshape(1, D))


def _tile(total, cap):
    t = min(total, cap)
    assert total % t == 0
    return t


def _group(x, mod, s_hg, s_ssm, conv_buf, lb, w, gla_chunk, ssd_chunk):
    b, t, _ = x.shape
    n = b * t
    mod3 = mod.reshape(b, 1, N_MOD * D)
    if t >= 256:
        bb, tt = 1, _tile(t, 1024)
    else:
        bb, tt = b, t
    proj = _inproj(x, mod3, w["norm1_w"], w["w_in"], bb, tt)

    o_g, s_hg_new = _gla(proj, lb, w["hgrn_norm_w"], s_hg, _tile(t, 1024), gla_chunk)

    conv_buf8 = jnp.pad(conv_buf, ((0, 0), (5, 0), (0, 0)))
    h0t = jnp.swapaxes(s_ssm.reshape(b, SSM_G, SSM_R * SSM_P, SSM_N), 2, 3)
    y_g, h_t = _ssd(proj, w["conv_w"], w["conv_b"], conv_buf8, w["dt_bias"], w["a_log"], w["ssm_d"],
                    w["ssm_norm_w"], h0t, ssd_chunk)
    s_ssm_new = jnp.swapaxes(h_t, 2, 3).reshape(b, SSM_HEADS, SSM_P, SSM_N)
    xbc = proj[:, :, COL_XS:COL_XS + SSM_XBC]
    assert t >= 3
    conv_new = xbc[:, t - 3:]

    if t >= 256:
        bb2, tt2 = 1, _tile(t, 256)
    else:
        bb2, tt2 = b, t
    x1, h2 = _merge(o_g, y_g, proj, x, mod3, w["norm2_w"], w["w_branch_a"], w["w_branch_b"], w["w_out"], bb2, tt2)

    h2f = h2.reshape(n, D)
    e1t, e2t, gwt = _route(h2f, w["peer_wq"], w["peer_keys1"], w["peer_keys2"], _tile(n, 1024))
    bb3, tt3 = (1, _tile(t, 512)) if t >= 256 else (b, t)
    y = _experts(h2f, e1t, e2t, gwt, w["peer_u"], w["peer_v"], x1, mod3, w["final_norm_w"], bb3, tt3)
    return y, s_hg_new, s_ssm_new, conv_new


def kernel(x_prompt, x_sample, c_prompt, c_sample, state_hgrn, state_ssm, state_conv, w_ada, b_ada, norm1_w, w_in,
           hgrn_lower_bounds, hgrn_norm_w, conv_w, conv_b, dt_bias, a_log, ssm_d, ssm_norm_w, w_branch_a,
           w_branch_b, w_out, norm2_w, peer_wq, peer_keys1, peer_keys2, peer_u, peer_v, final_norm_w):
    depth = w_ada.shape[0]
    assert depth == 1
    bp, tp, _ = x_prompt.shape
    bs, ts, _ = x_sample.shape
    lb_all = jnp.cumsum(jax.nn.softmax(hgrn_lower_bounds.astype(F32), axis=0), axis=0)

    l = 0
    wi = w_in[l]
    c_hg, c_z, c_xbc, c_dt = 4096, 6144, 9216, 9248
    w_pad = jnp.concatenate(
        [wi[:, :c_xbc], wi[:, c_dt:], wi[:, c_xbc:c_dt],
         jnp.zeros((D, PROJ_COLS - COL_DT - SSM_HEADS), wi.dtype)], axis=1).astype(BF16)
    w = dict(norm1_w=norm1_w[l], w_in=w_pad, hgrn_norm_w=hgrn_norm_w[l], conv_w=conv_w[l], conv_b=conv_b[l],
             dt_bias=dt_bias[l], a_log=a_log[l], ssm_d=ssm_d[l], ssm_norm_w=ssm_norm_w[l],
             w_branch_a=w_branch_a[l].astype(BF16), w_branch_b=w_branch_b[l].astype(BF16),
             w_out=w_out[l].astype(BF16), norm2_w=norm2_w[l], peer_wq=peer_wq[l].astype(BF16),
             peer_keys1=peer_keys1[l].astype(BF16), peer_keys2=peer_keys2[l].astype(BF16),
             peer_u=peer_u[l].astype(BF16), peer_v=peer_v[l].astype(BF16), final_norm_w=final_norm_w)

    mod = _adaln(jnp.concatenate([c_prompt, c_sample], axis=0), w_ada[l], b_ada[l])

    z_hg = jnp.zeros((bp, HG_HEADS, HG_DK, LANES), F32)
    z_ssm = jnp.zeros((bp, SSM_HEADS, SSM_P, SSM_N), F32)
    z_conv = jnp.zeros((bp, 3, SSM_XBC), F32)
    yp, hg_p, ssm_p, conv_p = _group(x_prompt, mod[:bp], z_hg, z_ssm, z_conv, lb_all[l], w,
                                     min(GLA_CHUNK, tp), min(SSD_CHUNK, tp))
    ys, hg_s, ssm_s, conv_s = _group(x_sample, mod[bp:], state_hgrn[l], state_ssm[l], state_conv[l], lb_all[l], w,
                                     min(GLA_CHUNK, ts), min(SSD_CHUNK, ts))
    return (yp, ys, hg_p[None], ssm_p[None], conv_p[None], hg_s[None], ssm_s[None], conv_s[None])
```

```python
import functools

import jax
import jax.numpy as jnp
from jax import lax
from jax.experimental import pallas as pl
from jax.experimental.pallas import tpu as pltpu

F32 = jnp.float32
BF16 = jnp.bfloat16
I32 = jnp.int32

EPS = 1e-6
D = 1024
N_MOD = 6
HG_HEADS = 8
HG_DK = 128
SSM_HEADS = 32
SSM_P = 64
SSM_G = 4
SSM_R = SSM_HEADS // SSM_G
SSM_N = 128
SSM_INNER = SSM_HEADS * SSM_P
SSM_GW = SSM_INNER // SSM_G
SSM_XBC = SSM_INNER + 2 * SSM_G * SSM_N
PEER_HEADS = 8
PEER_NK = 128
PEER_TOPK = 16
PEER_HK = PEER_HEADS * PEER_TOPK
PEER_DK = 256

LANES = 128
GLA_CHUNK = 64
SSD_CHUNK = 256
VMEM_LIMIT = 56 * 1024 * 1024

COL_Q = 0
COL_I = 1024
COL_G = 2048
COL_Z = 3072
COL_XS = 5120
COL_B = COL_XS + SSM_INNER
COL_C = COL_B + SSM_G * SSM_N
COL_GA = 8192
COL_GB = 9216
PROJ_BF16_COLS = 10240
COL_F = 0
COL_DT = 1024
PROJ_F32_COLS = 1280
PROJ_TN = 1280
PROJ_COLS = PROJ_BF16_COLS + PROJ_F32_COLS


def _cparams(sem):
    return pltpu.CompilerParams(dimension_semantics=sem, vmem_limit_bytes=VMEM_LIMIT)


def _silu(x):
    return x * jax.nn.sigmoid(x)


def _split3(x):
    hi = x.astype(BF16)
    r = x - hi.astype(F32)
    mid = r.astype(BF16)
    lo = (r - mid.astype(F32)).astype(BF16)
    return hi, mid, lo


def _dot(a, b):
    return jnp.dot(a, b, preferred_element_type=F32)


def _dot_nt(a, b):
    return lax.dot_general(a, b, (((1,), (1,)), ((), ())), preferred_element_type=F32)


def _dot_tn(a, b):
    return lax.dot_general(a, b, (((0,), (0,)), ((), ())), preferred_element_type=F32)


def _dot3_left(m_bf16, x):
    hi, mid, lo = _split3(x)
    return _dot(m_bf16, hi) + _dot(m_bf16, mid) + _dot(m_bf16, lo)


def _dot3_right(x, m_bf16):
    hi, mid, lo = _split3(x)
    return _dot(hi, m_bf16) + _dot(mid, m_bf16) + _dot(lo, m_bf16)


def _chunk_cumsum(x, chunk):
    pos = lax.broadcasted_iota(I32, x.shape, 0) % chunk
    s = 1
    while s < chunk:
        x = x + jnp.where(pos >= s, pltpu.roll(x, shift=s, axis=0), 0.0)
        s *= 2
    return x


def _tril(n):
    r = lax.broadcasted_iota(I32, (n, n), 0)
    c = lax.broadcasted_iota(I32, (n, n), 1)
    return r >= c


def _adaln_kernel(c_ref, w_ref, b_ref, o_ref):
    c = c_ref[...]
    o_ref[...] = _dot(_silu(c).astype(BF16), w_ref[...].astype(BF16)) + b_ref[...]


def _adaln(c, w_ada, b_ada):
    nb = c.shape[0]
    ncol = w_ada.shape[1]
    return pl.pallas_call(
        _adaln_kernel,
        grid=(ncol // D,),
        in_specs=[pl.BlockSpec((nb, D), lambda j: (0, 0)),
                  pl.BlockSpec((D, D), lambda j: (0, j)),
                  pl.BlockSpec((1, D), lambda j: (0, j))],
        out_specs=pl.BlockSpec((nb, D), lambda j: (0, j)),
        out_shape=jax.ShapeDtypeStruct((nb, ncol), F32),
        compiler_params=_cparams(("arbitrary",)),
        name="adaln",
    )(c, w_ada, b_ada.reshape(1, ncol))


def _inproj_kernel(x_ref, sh_ref, sc_ref, nw_ref, w_ref, ob_ref, of_ref, h_ref):
    bb, tt, _ = x_ref.shape
    n = pl.program_id(2)
    n_bf16 = PROJ_BF16_COLS // PROJ_TN

    @pl.when(n == 0)
    def _():
        x = x_ref[...]
        xn = x * lax.rsqrt(jnp.mean(x * x, axis=-1, keepdims=True) + EPS)
        h = xn * nw_ref[...] * (1.0 + sc_ref[...]) + sh_ref[...]
        h_ref[...] = h.reshape(bb * tt, D).astype(BF16)

    res = _dot(h_ref[...], w_ref[...]).reshape(bb, tt, -1)

    @pl.when(n < n_bf16)
    def _():
        ob_ref[...] = res.astype(BF16)

    @pl.when(n == n_bf16)
    def _():
        of_ref[...] = res


def _inproj(x, mod3, norm_w, w_pad, bb, tt):
    b, t, _ = x.shape
    n_bf16 = PROJ_BF16_COLS // PROJ_TN
    assert PROJ_F32_COLS == PROJ_TN
    return pl.pallas_call(
        _inproj_kernel,
        grid=(b // bb, t // tt, PROJ_COLS // PROJ_TN),
        in_specs=[pl.BlockSpec((bb, tt, D), lambda i, j, n: (i, j, 0)),
                  pl.BlockSpec((bb, 1, D), lambda i, j, n: (i, 0, 0)),
                  pl.BlockSpec((bb, 1, D), lambda i, j, n: (i, 0, 1)),
                  pl.BlockSpec((1, D), lambda i, j, n: (0, 0)),
                  pl.BlockSpec((D, PROJ_TN), lambda i, j, n: (0, n))],
        out_specs=[pl.BlockSpec((bb, tt, PROJ_TN), lambda i, j, n: (i, j, jnp.minimum(n, n_bf16 - 1))),
                   pl.BlockSpec((bb, tt, PROJ_TN), lambda i, j, n: (i, j, 0))],
        out_shape=[jax.ShapeDtypeStruct((b, t, PROJ_BF16_COLS), BF16),
                   jax.ShapeDtypeStruct((b, t, PROJ_F32_COLS), F32)],
        scratch_shapes=[pltpu.VMEM((bb * tt, D), BF16)],
        compiler_params=_cparams(("arbitrary", "arbitrary", "arbitrary")),
        name="inproj",
    )(x, mod3, mod3, norm_w.reshape(1, D), w_pad)


def _gla_kernel(q_ref, f_ref, i_ref, g_ref, lb_ref, nw_ref, s0_ref, o_ref, s_out_ref, s_ref, *, chunk):
    tt = q_ref.shape[1]
    nchunk = tt // chunk
    t_idx = pl.program_id(2)

    @pl.when(t_idx == 0)
    def _():
        s_ref[...] = s0_ref[0, 0].T

    lb = lb_ref[...]
    tril = _tril(chunk)
    mid = chunk // 2

    q_all = _silu(q_ref[0].astype(F32))
    ff = f_ref[0]
    logf = jnp.log(lb + (1.0 - lb) * jax.nn.sigmoid(ff))
    k_all = (1.0 - lb) * jax.nn.sigmoid(-ff)
    v_all = i_ref[0]
    g_all = _chunk_cumsum(logf, chunk)

    st = s_ref[...]
    outs = []
    for c in range(nchunk):
        sl = slice(c * chunk, (c + 1) * chunk)
        q, k, v, g = q_all[sl], k_all[sl], v_all[sl], g_all[sl]
        g_mid = g[mid:mid + 1, :]
        g_last = g[chunk - 1:chunk, :]
        qs = (q * jnp.exp(g - g_mid)).astype(BF16)
        ks = (k * jnp.exp(g_mid - g)).astype(BF16)
        att = jnp.where(tril, _dot_nt(qs, ks), 0.0).astype(BF16)
        outs.append(_dot(att, v) + _dot_nt((q * jnp.exp(g)).astype(BF16), st.astype(BF16)))
        kd = (k * jnp.exp(g_last - g)).astype(BF16)
        st = jnp.exp(g_last) * st + _dot_tn(v, kd)
    s_ref[...] = st
    o = jnp.concatenate(outs, axis=0) if nchunk > 1 else outs[0]
    on = o * lax.rsqrt(jnp.mean(o * o, axis=-1, keepdims=True) + EPS) * nw_ref[...]
    o_ref[0] = (on * _silu(g_ref[0].astype(F32))).astype(BF16)

    @pl.when(t_idx == pl.num_programs(2) - 1)
    def _():
        s_out_ref[0, 0] = s_ref[...].T


def _gla(proj, proj_f, lb, hg_norm_w, s0, tt, chunk):
    b, t, _ = proj.shape
    col = lambda base: (lambda i, h, j: (i, j, base // LANES + h))
    return pl.pallas_call(
        functools.partial(_gla_kernel, chunk=chunk),
        grid=(b, HG_HEADS, t // tt),
        in_specs=[pl.BlockSpec((1, tt, LANES), col(COL_Q)),
                  pl.BlockSpec((1, tt, LANES), col(COL_F)),
                  pl.BlockSpec((1, tt, LANES), col(COL_I)),
                  pl.BlockSpec((1, tt, LANES), col(COL_G)),
                  pl.BlockSpec((1, LANES), lambda i, h, j: (0, h)),
                  pl.BlockSpec((1, LANES), lambda i, h, j: (0, h)),
                  pl.BlockSpec((1, 1, HG_DK, LANES), lambda i, h, j: (i, h, 0, 0))],
        out_specs=[pl.BlockSpec((1, tt, LANES), lambda i, h, j: (i, j, h)),
                   pl.BlockSpec((1, 1, HG_DK, LANES), lambda i, h, j: (i, h, 0, 0))],
        out_shape=[jax.ShapeDtypeStruct((b, t, D), BF16),
                   jax.ShapeDtypeStruct((b, HG_HEADS, HG_DK, LANES), F32)],
        scratch_shapes=[pltpu.VMEM((HG_DK, LANES), F32)],
        compiler_params=_cparams(("arbitrary", "arbitrary", "arbitrary")),
        name="gla",
    )(proj, proj_f, proj, proj, lb.reshape(1, D), hg_norm_w.reshape(1, D), s0)


def _ssd_kernel(xs_ref, b_ref, c_ref, z_ref, dt_ref, cwx_ref, cwb_ref, cwc_ref, cbx_ref, cbb_ref, cbc_ref,
                bufx_ref, bufb_ref, bufc_ref, dtb_ref, alog_ref, dpar_ref, sel_ref, exp_ref, nw_ref, h0_ref,
                y_ref, h_out_ref, h_ref, xpx_ref, xpb_ref, xpc_ref):
    L = xs_ref.shape[1]
    t_idx = pl.program_id(2)

    @pl.when(t_idx == 0)
    def _():
        h_ref[...] = h0_ref[0, 0]
        xpx_ref[0:8, :] = bufx_ref[0]
        xpb_ref[0:8, :] = bufb_ref[0]
        xpc_ref[0:8, :] = bufc_ref[0]

    def conv(xp_ref, raw, w_ref, bias_ref):
        xp_ref[8:8 + L, :] = raw.astype(F32)
        acc = bias_ref[...] + xp_ref[5:5 + L, :] * w_ref[0:1, :]
        for j in range(1, 4):
            acc = acc + xp_ref[5 + j:5 + j + L, :] * w_ref[j:j + 1, :]
        xp_ref[0:8, :] = xp_ref[L:L + 8, :]
        return _silu(acc)

    xs = conv(xpx_ref, xs_ref[0], cwx_ref, cbx_ref)
    bm = conv(xpb_ref, b_ref[0], cwb_ref, cbb_ref)
    cm = conv(xpc_ref, c_ref[0], cwc_ref, cbc_ref)
    bm_bf = bm.astype(BF16)
    cm_bf = cm.astype(BF16)

    sel = sel_ref[0]
    expand = exp_ref[...]
    dt_raw = _dot3_right(dt_ref[0], sel)
    dt = jax.nn.softplus(dt_raw + dtb_ref[0])
    a_neg = -jnp.exp(alog_ref[0])
    tril = _tril(L)
    ac = _dot3_left(tril.astype(BF16), dt * a_neg)
    a_last = ac[L - 1:L, :]

    def expand2(v):
        hi = v.astype(BF16)
        lo = (v - hi.astype(F32)).astype(BF16)
        return _dot(hi, expand) + _dot(lo, expand)

    dt_x = expand2(dt)
    eac_x = expand2(jnp.exp(ac))
    wd_x = expand2(jnp.exp(a_last - ac) * dt)
    row8 = lax.broadcasted_iota(I32, (8, LANES), 0)
    tail_x = expand2(jnp.where(row8 == 0, jnp.exp(a_last), jnp.where(row8 == 1, dpar_ref[0], 0.0)))
    decay_x = tail_x[0:1, :]
    d_x = tail_x[1:2, :]

    xdt = (xs * dt_x).astype(BF16)
    cbm = jnp.where(tril, _dot_nt(cm_bf, bm_bf), 0.0)
    ac_t = ac.T
    lane = lax.broadcasted_iota(I32, (L, LANES), 1)
    zero = jnp.zeros((), BF16)
    ys = []
    for pair in range(SSM_R // 2):
        xq = xdt[:, pair * LANES:(pair + 1) * LANES]
        acc = None
        for half in range(2):
            r = 2 * pair + half
            seg = jnp.exp(jnp.minimum(ac[:, r:r + 1] - ac_t[r:r + 1, :], 0.0))
            w = (cbm * seg).astype(BF16)
            keep = (lane < SSM_P) if half == 0 else (lane >= SSM_P)
            part = _dot(w, jnp.where(keep, xq, zero))
            acc = part if acc is None else acc + part
        ys.append(acc)
    h_old = h_ref[...]
    y = jnp.concatenate(ys, axis=1) + _dot(cm_bf, h_old.astype(BF16)) * eac_x
    xw = (xs * wd_x).astype(BF16)
    h_ref[...] = decay_x * h_old + _dot_tn(bm_bf, xw)

    y = (y + d_x * xs) * _silu(z_ref[0].astype(F32))
    y = y * lax.rsqrt(jnp.mean(y * y, axis=-1, keepdims=True) + EPS) * nw_ref[...]
    y_ref[0] = y.astype(BF16)

    @pl.when(t_idx == pl.num_programs(2) - 1)
    def _():
        h_out_ref[0, 0] = h_ref[...]


def _ssd(proj, proj_f, conv_w, conv_b, conv_buf8, dt_bias, a_log, ssm_d, ssm_norm_w, h0t, chunk):
    b, t, _ = proj.shape
    L = chunk
    grp = lambda v: jnp.pad(v.reshape(SSM_G, 1, SSM_R), ((0, 0), (0, 0), (0, LANES - SSM_R)))
    hh = jnp.arange(LANES)
    sel = (hh[None, :, None] == (jnp.arange(SSM_G)[:, None, None] * SSM_R + hh[None, None, :])) & (hh[None, None, :] < SSM_R)
    expand = (hh[:, None] == (jnp.arange(SSM_GW)[None, :] // SSM_P))
    cb2 = conv_b.reshape(1, SSM_XBC)
    nb_x, nb_b, nb_c = 0, SSM_INNER // LANES, (SSM_INNER + SSM_G * SSM_N) // LANES
    pcol = lambda base, width: (lambda i, g, j: (i, j, base // width + g))
    return pl.pallas_call(
        _ssd_kernel,
        grid=(b, SSM_G, t // L),
        in_specs=[pl.BlockSpec((1, L, SSM_GW), pcol(COL_XS, SSM_GW)),
                  pl.BlockSpec((1, L, SSM_N), pcol(COL_B, SSM_N)),
                  pl.BlockSpec((1, L, SSM_N), pcol(COL_C, SSM_N)),
                  pl.BlockSpec((1, L, SSM_GW), pcol(COL_Z, SSM_GW)),
                  pl.BlockSpec((1, L, LANES), lambda i, g, j: (i, j, COL_DT // LANES)),
                  pl.BlockSpec((4, SSM_GW), lambda i, g, j: (0, g)),
                  pl.BlockSpec((4, SSM_N), lambda i, g, j: (0, nb_b + g)),
                  pl.BlockSpec((4, SSM_N), lambda i, g, j: (0, nb_c + g)),
                  pl.BlockSpec((1, SSM_GW), lambda i, g, j: (0, g)),
                  pl.BlockSpec((1, SSM_N), lambda i, g, j: (0, nb_b + g)),
                  pl.BlockSpec((1, SSM_N), lambda i, g, j: (0, nb_c + g)),
                  pl.BlockSpec((1, 8, SSM_GW), lambda i, g, j: (i, 0, g)),
                  pl.BlockSpec((1, 8, SSM_N), lambda i, g, j: (i, 0, nb_b + g)),
                  pl.BlockSpec((1, 8, SSM_N), lambda i, g, j: (i, 0, nb_c + g)),
                  pl.BlockSpec((1, 1, LANES), lambda i, g, j: (g, 0, 0)),
                  pl.BlockSpec((1, 1, LANES), lambda i, g, j: (g, 0, 0)),
                  pl.BlockSpec((1, 1, LANES), lambda i, g, j: (g, 0, 0)),
                  pl.BlockSpec((1, LANES, LANES), lambda i, g, j: (g, 0, 0)),
                  pl.BlockSpec((LANES, SSM_GW), lambda i, g, j: (0, 0)),
                  pl.BlockSpec((1, SSM_GW), lambda i, g, j: (0, g)),
                  pl.BlockSpec((1, 1, SSM_N, SSM_GW), lambda i, g, j: (i, g, 0, 0))],
        out_specs=[pl.BlockSpec((1, L, SSM_GW), lambda i, g, j: (i, j, g)),
                   pl.BlockSpec((1, 1, SSM_N, SSM_GW), lambda i, g, j: (i, g, 0, 0))],
        out_shape=[jax.ShapeDtypeStruct((b, t, SSM_INNER), BF16),
                   jax.ShapeDtypeStruct((b, SSM_G, SSM_N, SSM_GW), F32)],
        scratch_shapes=[pltpu.VMEM((SSM_N, SSM_GW), F32),
                        pltpu.VMEM((L + 8, SSM_GW), F32),
                        pltpu.VMEM((L + 8, SSM_N), F32),
                        pltpu.VMEM((L + 8, SSM_N), F32)],
        compiler_params=_cparams(("arbitrary", "arbitrary", "arbitrary")),
        name="ssd",
    )(proj, proj, proj, proj, proj_f, conv_w, conv_w, conv_w, cb2, cb2, cb2, conv_buf8, conv_buf8, conv_buf8,
      grp(dt_bias), grp(a_log), grp(ssm_d), sel.astype(BF16), expand.astype(BF16),
      ssm_norm_w.reshape(1, SSM_INNER), h0t)


def _merge_kernel(o_ref, y_ref, ga_ref, gb_ref, x_ref, g1_ref, sh2_ref, sc2_ref, nw_ref, wa_ref, wb_ref, wo_ref,
                  x1_ref, h2_ref):
    bb, tt, _ = x_ref.shape
    n = bb * tt
    pa = _dot(o_ref[...].reshape(n, D), wa_ref[...])
    pb = _dot(y_ref[...].reshape(n, SSM_INNER), wb_ref[...])
    ga = ga_ref[...].reshape(n, D).astype(F32)
    gb = gb_ref[...].reshape(n, D).astype(F32)
    merged = jax.nn.sigmoid(ga) * pa + jax.nn.sigmoid(gb) * pb
    mix = _dot(merged.astype(BF16), wo_ref[...]).reshape(bb, tt, D)
    x1 = x_ref[...] + g1_ref[...] * mix
    x1_ref[...] = x1
    xn = x1 * lax.rsqrt(jnp.mean(x1 * x1, axis=-1, keepdims=True) + EPS)
    h2_ref[...] = (xn * nw_ref[...] * (1.0 + sc2_ref[...]) + sh2_ref[...]).astype(BF16)


def _merge(o_g, y_g, proj, x, mod3, norm2_w, wa, wb, wo, bb, tt):
    b, t, _ = x.shape
    row = lambda i, j: (i, j, 0)
    modc = lambda c: (lambda i, j: (i, 0, c))
    const = lambda i, j: (0, 0)
    return pl.pallas_call(
        _merge_kernel,
        grid=(b // bb, t // tt),
        in_specs=[pl.BlockSpec((bb, tt, D), row),
                  pl.BlockSpec((bb, tt, SSM_INNER), row),
                  pl.BlockSpec((bb, tt, D), lambda i, j: (i, j, COL_GA // D)),
                  pl.BlockSpec((bb, tt, D), lambda i, j: (i, j, COL_GB // D)),
                  pl.BlockSpec((bb, tt, D), row),
                  pl.BlockSpec((bb, 1, D), modc(2)),
                  pl.BlockSpec((bb, 1, D), modc(3)),
                  pl.BlockSpec((bb, 1, D), modc(4)),
                  pl.BlockSpec((1, D), const),
                  pl.BlockSpec((D, D), const),
                  pl.BlockSpec((SSM_INNER, D), const),
                  pl.BlockSpec((D, D), const)],
        out_specs=[pl.BlockSpec((bb, tt, D), row), pl.BlockSpec((bb, tt, D), row)],
        out_shape=[jax.ShapeDtypeStruct((b, t, D), F32), jax.ShapeDtypeStruct((b, t, D), BF16)],
        compiler_params=_cparams(("arbitrary", "arbitrary")),
        name="merge",
    )(o_g, y_g, proj, proj, x, mod3, mod3, mod3, norm2_w.reshape(1, D), wa, wb, wo)


_NO_ROW = float(1 << 20)


def _tree(op, xs):
    xs = list(xs)
    while len(xs) > 1:
        xs = [op(xs[i], xs[i + 1]) for i in range(0, len(xs) - 1, 2)] + ([xs[-1]] if len(xs) % 2 else [])
    return xs[0]


def _topk_list(vals, ids, k):
    out_v, out_i = [], []
    for _ in range(k):
        m = _tree(jnp.maximum, vals)
        i = _tree(jnp.minimum, [jnp.where(v == m, c, _NO_ROW) for v, c in zip(vals, ids)])
        out_v.append(m)
        out_i.append(i)
        vals = [jnp.where(i == c, -jnp.inf, v) for v, c in zip(vals, ids)]
    return out_v, out_i


def _pick_list(table, sel):
    out = table[0]
    for i in range(1, len(table)):
        out = jnp.where(sel == float(i), table[i], out)
    return out


def _pair_candidates(v1, v2):
    kk = PEER_TOPK
    vals, ids = [], []
    for i in range(kk):
        for j in range(kk):
            if (i + 1) * (j + 1) <= kk:
                vals.append(v1[i] + v2[j])
                ids.append(float(i * kk + j))
    return vals, ids


ROUTE_PITCH = LANES + 8


def _route_kernel(h2_ref, wq_ref, k1_ref, k2_ref, e1_ref, e2_ref, gw_ref, q_ref, s1_ref, s2_ref):
    tm = h2_ref.shape[0]
    groups = tm // LANES
    half = PEER_DK // 2
    q = _dot(h2_ref[...], wq_ref[...])
    for c in range(2 * PEER_HEADS):
        q_ref[c] = q[:, c * half:(c + 1) * half].astype(BF16)
    key_ids = [float(k) for k in range(PEER_NK)]

    def per_key_scores(keys, qh, s_ref):
        for g in range(groups):
            s_ref[g * ROUTE_PITCH:g * ROUTE_PITCH + PEER_NK, :] = _dot_nt(keys, qh[g * LANES:(g + 1) * LANES, :])
        return [s_ref[pl.ds(k, groups, stride=ROUTE_PITCH), :] for k in range(PEER_NK)]

    def body(h, carry):
        v1, i1 = _topk_list(per_key_scores(k1_ref[h], q_ref[2 * h], s1_ref), key_ids, PEER_TOPK)
        v2, i2 = _topk_list(per_key_scores(k2_ref[h], q_ref[2 * h + 1], s2_ref), key_ids, PEER_TOPK)
        cand, flat = _pair_candidates(v1, v2)
        sv, sf = _topk_list(cand, flat, PEER_TOPK)
        p = [jnp.exp(v - sv[0]) for v in sv]
        inv = 1.0 / _tree(jnp.add, p)
        for k in range(PEER_TOPK):
            sel_i = jnp.floor(sf[k] * (1.0 / PEER_TOPK))
            sel_j = sf[k] - sel_i * float(PEER_TOPK)
            slot = pl.ds(h * PEER_TOPK + k, 1)
            e1_ref[:, slot, :] = _pick_list(i1, sel_i).astype(I32).reshape(groups, 1, LANES)
            e2_ref[:, slot, :] = _pick_list(i2, sel_j).astype(I32).reshape(groups, 1, LANES)
            gw_ref[:, slot, :] = (p[k] * inv).reshape(groups, 1, LANES)
        return carry

    lax.fori_loop(0, PEER_HEADS, body, 0)


def _route(h2, wq, k1, k2, tm):
    n = h2.shape[0]
    groups = tm // LANES
    out = jax.ShapeDtypeStruct((n // LANES, PEER_HK, LANES), I32)
    ospec = pl.BlockSpec((groups, PEER_HK, LANES), lambda i: (i, 0, 0))
    const3 = lambda i: (0, 0, 0)
    return pl.pallas_call(
        _route_kernel,
        grid=(n // tm,),
        in_specs=[pl.BlockSpec((tm, D), lambda i: (i, 0)),
                  pl.BlockSpec((D, PEER_HEADS * PEER_DK), lambda i: (0, 0)),
                  pl.BlockSpec((PEER_HEADS, PEER_NK, PEER_DK // 2), const3),
                  pl.BlockSpec((PEER_HEADS, PEER_NK, PEER_DK // 2), const3)],
        out_specs=[ospec, ospec, ospec],
        out_shape=[out, out, jax.ShapeDtypeStruct((n // LANES, PEER_HK, LANES), F32)],
        scratch_shapes=[pltpu.VMEM((2 * PEER_HEADS, tm, PEER_DK // 2), BF16),
                        pltpu.VMEM((groups * ROUTE_PITCH, LANES), F32),
                        pltpu.VMEM((groups * ROUTE_PITCH, LANES), F32)],
        compiler_params=_cparams(("arbitrary",)),
        name="route",
    )(h2, wq, k1, k2)


EXPERT_AB = 16
W_PAD = 8
SCATTER_UNROLL = 32


def _experts_kernel(h2_ref, e1t_ref, e2t_ref, gwt_ref, u_ref, v_ref, x1_ref, g2_ref, fw_ref, y_ref,
                    e1_ref, e2_ref, gw_ref, gact_ref, coef_ref, w_ref, acc_ref, act_ref):
    tm = h2_ref.shape[0]
    phase = pl.program_id(2)
    j = pl.program_id(3)
    nj = pl.num_programs(3)

    def select_from_act(step):
        e1 = e1_ref[...]
        e2 = e2_ref[...]
        g = jnp.zeros((tm, PEER_HK), F32)
        for a in range(EXPERT_AB):
            picked = jnp.take_along_axis(act_ref[:, a * LANES:(a + 1) * LANES], e2, axis=1,
                                         mode="promise_in_bounds")
            g = jnp.where(e1 == step * EXPERT_AB + a, picked, g)
        return g

    @pl.when((phase == 0) & (j == 0))
    def _():
        for r in range(tm // LANES):
            rows = slice(r * LANES, (r + 1) * LANES)
            e1_ref[rows, :] = e1t_ref[r].astype(F32).T.astype(I32)
            e2_ref[rows, :] = e2t_ref[r].astype(F32).T.astype(I32)
            gw_ref[rows, :] = gwt_ref[r].T
        gact_ref[...] = jnp.zeros_like(gact_ref)
        act_ref[...] = jnp.zeros_like(act_ref)

    @pl.when(phase == 0)
    def _():
        gact_ref[...] += select_from_act(j - 1)
        act_ref[...] = _dot_nt(h2_ref[...], u_ref[...])

    @pl.when((phase == 1) & (j == 0))
    def _():
        ga = gact_ref[...] + select_from_act(nj - 1)
        gelu = 0.5 * ga * (1.0 + lax.erf(ga * (0.5 ** 0.5)))
        coef_ref[...] = gw_ref[...] * gelu
        acc_ref[...] = jnp.zeros_like(acc_ref)

    half_steps = nj // 2
    a_half = PEER_NK // 2
    pitch = tm + W_PAD

    @pl.when((phase == 1) & (j % half_steps == 0))
    def _():
        a0 = (j // half_steps) * a_half
        rows_a = lax.broadcasted_iota(I32, (a_half, PEER_HK), 0) + a0
        rows_b = lax.broadcasted_iota(I32, (PEER_NK, PEER_HK), 0)

        def scatter(t, carry):
            ct = jnp.where(rows_a == e1_ref[pl.ds(t, 1), :], coef_ref[pl.ds(t, 1), :], 0.0).astype(BF16)
            bt = (rows_b == e2_ref[pl.ds(t, 1), :]).astype(BF16)
            w_ref[pl.ds(t, a_half, stride=pitch), :] = _dot_nt(ct, bt)
            return carry

        lax.fori_loop(0, tm, scatter, 0, unroll=SCATTER_UNROLL)

    @pl.when(phase == 1)
    def _():
        a_loc = (j % half_steps) * EXPERT_AB
        parts = [w_ref[pl.ds(pl.multiple_of((a_loc + a) * pitch, 8), tm), :].astype(BF16) for a in range(EXPERT_AB)]
        acc_ref[...] += _dot(jnp.concatenate(parts, axis=1), v_ref[...])

    @pl.when((phase == 1) & (j == nj - 1))
    def _():
        bb, tt, _ = x1_ref.shape
        x2 = x1_ref[...] + g2_ref[...] * acc_ref[...].reshape(bb, tt, D)
        y_ref[...] = x2 * lax.rsqrt(jnp.mean(x2 * x2, axis=-1, keepdims=True) + EPS) * fw_ref[...]


def _experts(h2, e1t, e2t, gwt, u, v, x1, mod3, final_w, bb, tt):
    b, t, _ = x1.shape
    tm = bb * tt
    nj = PEER_NK // EXPERT_AB
    eb = EXPERT_AB * PEER_NK
    nt = t // tt
    tspec = pl.BlockSpec((tm // LANES, PEER_HK, LANES), lambda i, k, p, j: (i * nt + k, 0, 0))
    rspec = pl.BlockSpec((bb, tt, D), lambda i, k, p, j: (i, k, 0))
    return pl.pallas_call(
        _experts_kernel,
        grid=(b // bb, nt, 2, nj),
        in_specs=[pl.BlockSpec((tm, D), lambda i, k, p, j: (i * nt + k, 0)), tspec, tspec, tspec,
                  pl.BlockSpec((eb, D), lambda i, k, p, j: (jnp.where(p == 0, j, nj - 1), 0)),
                  pl.BlockSpec((eb, D), lambda i, k, p, j: (jnp.where(p == 1, j, 0), 0)),
                  rspec,
                  pl.BlockSpec((bb, 1, D), lambda i, k, p, j: (i, 0, 5)),
                  pl.BlockSpec((1, D), lambda i, k, p, j: (0, 0))],
        out_specs=rspec,
        out_shape=jax.ShapeDtypeStruct((b, t, D), F32),
        scratch_shapes=[pltpu.VMEM((tm, PEER_HK), I32),
                        pltpu.VMEM((tm, PEER_HK), I32),
                        pltpu.VMEM((tm, PEER_HK), F32),
                        pltpu.VMEM((tm, PEER_HK), F32),
                        pltpu.VMEM((tm, PEER_HK), F32),
                        pltpu.VMEM(((PEER_NK // 2) * (tm + W_PAD), PEER_NK), F32),
                        pltpu.VMEM((tm, D), F32),
                        pltpu.VMEM((tm, EXPERT_AB * LANES), F32)],
        compiler_params=_cparams(("arbitrary", "arbitrary", "arbitrary", "arbitrary")),
        name="experts",
    )(h2, e1t, e2t, gwt, u, v, x1, mod3, final_w.reshape(1, D))


def _tile(total, cap):
    t = min(total, cap)
    assert total % t == 0
    return t


def _group(x, mod, s_hg, s_ssm, conv_buf, lb, w, gla_chunk, ssd_chunk):
    b, t, _ = x.shape
    n = b * t
    mod3 = mod.reshape(b, 1, N_MOD * D)
    if t >= 256:
        bb, tt = 1, _tile(t, 1024)
    else:
        bb, tt = b, t
    proj, proj_f = _inproj(x, mod3, w["norm1_w"], w["w_in"], bb, tt)

    o_g, s_hg_new = _gla(proj, proj_f, lb, w["hgrn_norm_w"], s_hg, _tile(t, 1024), gla_chunk)

    conv_buf8 = jnp.pad(conv_buf, ((0, 0), (5, 0), (0, 0)))
    h0t = jnp.swapaxes(s_ssm.reshape(b, SSM_G, SSM_R * SSM_P, SSM_N), 2, 3)
    y_g, h_t = _ssd(proj, proj_f, w["conv_w"], w["conv_b"], conv_buf8, w["dt_bias"], w["a_log"], w["ssm_d"],
                    w["ssm_norm_w"], h0t, ssd_chunk)
    s_ssm_new = jnp.swapaxes(h_t, 2, 3).reshape(b, SSM_HEADS, SSM_P, SSM_N)
    assert t >= 3
    conv_new = proj[:, t - 3:, COL_XS:COL_XS + SSM_XBC].astype(F32)

    if t >= 256:
        bb2, tt2 = 1, _tile(t, 256)
    else:
        bb2, tt2 = b, t
    x1, h2 = _merge(o_g, y_g, proj, x, mod3, w["norm2_w"], w["w_branch_a"], w["w_branch_b"], w["w_out"], bb2, tt2)

    h2f = h2.reshape(n, D)
    e1t, e2t, gwt = _route(h2f, w["peer_wq"], w["peer_keys1"], w["peer_keys2"], _tile(n, 1024))
    bb3, tt3 = (1, _tile(t, 512)) if t >= 256 else (b, t)
    y = _experts(h2f, e1t, e2t, gwt, w["peer_u"], w["peer_v"], x1, mod3, w["final_norm_w"], bb3, tt3)
    return y, s_hg_new, s_ssm_new, conv_new


def kernel(x_prompt, x_sample, c_prompt, c_sample, state_hgrn, state_ssm, state_conv, w_ada, b_ada, norm1_w, w_in,
           hgrn_lower_bounds, hgrn_norm_w, conv_w, conv_b, dt_bias, a_log, ssm_d, ssm_norm_w, w_branch_a,
           w_branch_b, w_out, norm2_w, peer_wq, peer_keys1, peer_keys2, peer_u, peer_v, final_norm_w):
    depth = w_ada.shape[0]
    assert depth == 1
    bp, tp, _ = x_prompt.shape
    bs, ts, _ = x_sample.shape
    lb_all = jnp.cumsum(jax.nn.softmax(hgrn_lower_bounds.astype(F32), axis=0), axis=0)

    l = 0
    wi = w_in[l]
    c_f, c_i, c_dt, c_ga = 1024, 2048, 9216, 9248
    w_pad = jnp.concatenate(
        [wi[:, :c_f], wi[:, c_i:c_dt], wi[:, c_ga:],
         wi[:, c_f:c_i], wi[:, c_dt:c_ga],
         jnp.zeros((D, PROJ_F32_COLS - (c_i - c_f) - SSM_HEADS), wi.dtype)], axis=1).astype(BF16)
    assert w_pad.shape[1] == PROJ_COLS
    w = dict(norm1_w=norm1_w[l], w_in=w_pad, hgrn_norm_w=hgrn_norm_w[l], conv_w=conv_w[l], conv_b=conv_b[l],
             dt_bias=dt_bias[l], a_log=a_log[l], ssm_d=ssm_d[l], ssm_norm_w=ssm_norm_w[l],
             w_branch_a=w_branch_a[l].astype(BF16), w_branch_b=w_branch_b[l].astype(BF16),
             w_out=w_out[l].astype(BF16), norm2_w=norm2_w[l], peer_wq=peer_wq[l].astype(BF16),
             peer_keys1=peer_keys1[l].astype(BF16), peer_keys2=peer_keys2[l].astype(BF16),
             peer_u=peer_u[l].astype(BF16), peer_v=peer_v[l].astype(BF16), final_norm_w=final_norm_w)

    mod = _adaln(jnp.concatenate([c_prompt, c_sample], axis=0), w_ada[l], b_ada[l])

    z_hg = jnp.zeros((bp, HG_HEADS, HG_DK, LANES), F32)
    z_ssm = jnp.zeros((bp, SSM_HEADS, SSM_P, SSM_N), F32)
    z_conv = jnp.zeros((bp, 3, SSM_XBC), F32)
    yp, hg_p, ssm_p, conv_p = _group(x_prompt, mod[:bp], z_hg, z_ssm, z_conv, lb_all[l], w,
                                     min(GLA_CHUNK, tp), min(SSD_CHUNK, tp))
    ys, hg_s, ssm_s, conv_s = _group(x_sample, mod[bp:], state_hgrn[l], state_ssm[l], state_conv[l], lb_all[l], w,
                                     min(GLA_CHUNK, ts), min(SSD_CHUNK, ts))
    return (yp, ys, hg_p[None], ssm_p[None], conv_p[None], hg_s[None], ssm_s[None], conv_s[None])
```

```python
import functools

import jax
import jax.numpy as jnp
from jax import lax
from jax.experimental import pallas as pl
from jax.experimental.pallas import tpu as pltpu

F32 = jnp.float32
BF16 = jnp.bfloat16
I32 = jnp.int32

EPS = 1e-6
D = 1024
N_MOD = 6
HG_HEADS = 8
HG_DK = 128
SSM_HEADS = 32
SSM_P = 64
SSM_G = 4
SSM_R = SSM_HEADS // SSM_G
SSM_N = 128
SSM_INNER = SSM_HEADS * SSM_P
SSM_GW = SSM_INNER // SSM_G
SSM_XBC = SSM_INNER + 2 * SSM_G * SSM_N
PEER_HEADS = 8
PEER_NK = 128
PEER_TOPK = 16
PEER_HK = PEER_HEADS * PEER_TOPK
PEER_DK = 256

LANES = 128
GLA_CHUNK = 64
SSD_CHUNK = 256
VMEM_LIMIT = 56 * 1024 * 1024

COL_Q = 0
COL_I = 1024
COL_G = 2048
COL_Z = 3072
COL_XS = 5120
COL_B = COL_XS + SSM_INNER
COL_C = COL_B + SSM_G * SSM_N
COL_GA = 8192
COL_GB = 9216
PROJ_BF16_COLS = 10240
COL_F = 0
COL_DT = 1024
PROJ_F32_COLS = 1280
PROJ_TN = 1280
PROJ_COLS = PROJ_BF16_COLS + PROJ_F32_COLS


def _cparams(sem):
    return pltpu.CompilerParams(dimension_semantics=sem, vmem_limit_bytes=VMEM_LIMIT)


def _silu(x):
    return x * jax.nn.sigmoid(x)


def _split3(x):
    hi = x.astype(BF16)
    r = x - hi.astype(F32)
    mid = r.astype(BF16)
    lo = (r - mid.astype(F32)).astype(BF16)
    return hi, mid, lo


def _dot(a, b):
    return jnp.dot(a, b, preferred_element_type=F32)


def _dot_nt(a, b):
    return lax.dot_general(a, b, (((1,), (1,)), ((), ())), preferred_element_type=F32)


def _dot_tn(a, b):
    return lax.dot_general(a, b, (((0,), (0,)), ((), ())), preferred_element_type=F32)


def _dot3_left(m_bf16, x):
    hi, mid, lo = _split3(x)
    return _dot(m_bf16, hi) + _dot(m_bf16, mid) + _dot(m_bf16, lo)


def _dot3_right(x, m_bf16):
    hi, mid, lo = _split3(x)
    return _dot(hi, m_bf16) + _dot(mid, m_bf16) + _dot(lo, m_bf16)


def _chunk_cumsum(x, chunk):
    pos = lax.broadcasted_iota(I32, x.shape, 0) % chunk
    s = 1
    while s < chunk:
        x = x + jnp.where(pos >= s, pltpu.roll(x, shift=s, axis=0), 0.0)
        s *= 2
    return x


def _tril(n):
    r = lax.broadcasted_iota(I32, (n, n), 0)
    c = lax.broadcasted_iota(I32, (n, n), 1)
    return r >= c


def _adaln_kernel(c_ref, w_ref, b_ref, o_ref):
    c = c_ref[...]
    o_ref[...] = _dot(_silu(c).astype(BF16), w_ref[...].astype(BF16)) + b_ref[...]


def _adaln(c, w_ada, b_ada):
    nb = c.shape[0]
    ncol = w_ada.shape[1]
    return pl.pallas_call(
        _adaln_kernel,
        grid=(ncol // D,),
        in_specs=[pl.BlockSpec((nb, D), lambda j: (0, 0)),
                  pl.BlockSpec((D, D), lambda j: (0, j)),
                  pl.BlockSpec((1, D), lambda j: (0, j))],
        out_specs=pl.BlockSpec((nb, D), lambda j: (0, j)),
        out_shape=jax.ShapeDtypeStruct((nb, ncol), F32),
        compiler_params=_cparams(("arbitrary",)),
        name="adaln",
    )(c, w_ada, b_ada.reshape(1, ncol))


def _inproj_kernel(x_ref, sh_ref, sc_ref, nw_ref, w_ref, ob_ref, of_ref, h_ref):
    bb, tt, _ = x_ref.shape
    n = pl.program_id(2)
    n_bf16 = PROJ_BF16_COLS // PROJ_TN

    @pl.when(n == 0)
    def _():
        x = x_ref[...]
        xn = x * lax.rsqrt(jnp.mean(x * x, axis=-1, keepdims=True) + EPS)
        h = xn * nw_ref[...] * (1.0 + sc_ref[...]) + sh_ref[...]
        h_ref[...] = h.reshape(bb * tt, D).astype(BF16)

    res = _dot(h_ref[...], w_ref[...]).reshape(bb, tt, -1)

    @pl.when(n < n_bf16)
    def _():
        ob_ref[...] = res.astype(BF16)

    @pl.when(n == n_bf16)
    def _():
        of_ref[...] = res


def _inproj(x, mod3, norm_w, w_pad, bb, tt):
    b, t, _ = x.shape
    n_bf16 = PROJ_BF16_COLS // PROJ_TN
    assert PROJ_F32_COLS == PROJ_TN
    return pl.pallas_call(
        _inproj_kernel,
        grid=(b // bb, t // tt, PROJ_COLS // PROJ_TN),
        in_specs=[pl.BlockSpec((bb, tt, D), lambda i, j, n: (i, j, 0)),
                  pl.BlockSpec((bb, 1, D), lambda i, j, n: (i, 0, 0)),
                  pl.BlockSpec((bb, 1, D), lambda i, j, n: (i, 0, 1)),
                  pl.BlockSpec((1, D), lambda i, j, n: (0, 0)),
                  pl.BlockSpec((D, PROJ_TN), lambda i, j, n: (0, n))],
        out_specs=[pl.BlockSpec((bb, tt, PROJ_TN), lambda i, j, n: (i, j, jnp.minimum(n, n_bf16 - 1))),
                   pl.BlockSpec((bb, tt, PROJ_TN), lambda i, j, n: (i, j, 0))],
        out_shape=[jax.ShapeDtypeStruct((b, t, PROJ_BF16_COLS), BF16),
                   jax.ShapeDtypeStruct((b, t, PROJ_F32_COLS), F32)],
        scratch_shapes=[pltpu.VMEM((bb * tt, D), BF16)],
        compiler_params=_cparams(("arbitrary", "arbitrary", "arbitrary")),
        name="inproj",
    )(x, mod3, mod3, norm_w.reshape(1, D), w_pad)


def _gla_kernel(q_ref, f_ref, i_ref, g_ref, lb_ref, nw_ref, s0_ref, o_ref, s_out_ref, s_ref, *, chunk):
    tt = q_ref.shape[1]
    nchunk = tt // chunk
    t_idx = pl.program_id(2)

    @pl.when(t_idx == 0)
    def _():
        s_ref[...] = s0_ref[0, 0].T

    lb = lb_ref[...]
    tril = _tril(chunk)
    mid = chunk // 2

    q_all = _silu(q_ref[0].astype(F32))
    ff = f_ref[0]
    logf = jnp.log(lb + (1.0 - lb) * jax.nn.sigmoid(ff))
    k_all = (1.0 - lb) * jax.nn.sigmoid(-ff)
    v_all = i_ref[0]
    g_all = _chunk_cumsum(logf, chunk)

    st = s_ref[...]
    outs = []
    for c in range(nchunk):
        sl = slice(c * chunk, (c + 1) * chunk)
        q, k, v, g = q_all[sl], k_all[sl], v_all[sl], g_all[sl]
        g_mid = g[mid:mid + 1, :]
        g_last = g[chunk - 1:chunk, :]
        qs = (q * jnp.exp(g - g_mid)).astype(BF16)
        ks = (k * jnp.exp(g_mid - g)).astype(BF16)
        att = jnp.where(tril, _dot_nt(qs, ks), 0.0).astype(BF16)
        outs.append(_dot(att, v) + _dot_nt((q * jnp.exp(g)).astype(BF16), st.astype(BF16)))
        kd = (k * jnp.exp(g_last - g)).astype(BF16)
        st = jnp.exp(g_last) * st + _dot_tn(v, kd)
    s_ref[...] = st
    o = jnp.concatenate(outs, axis=0) if nchunk > 1 else outs[0]
    on = o * lax.rsqrt(jnp.mean(o * o, axis=-1, keepdims=True) + EPS) * nw_ref[...]
    o_ref[0] = (on * _silu(g_ref[0].astype(F32))).astype(BF16)

    @pl.when(t_idx == pl.num_programs(2) - 1)
    def _():
        s_out_ref[0, 0] = s_ref[...].T


def _gla(proj, proj_f, lb, hg_norm_w, s0, tt, chunk):
    b, t, _ = proj.shape
    col = lambda base: (lambda i, h, j: (i, j, base // LANES + h))
    return pl.pallas_call(
        functools.partial(_gla_kernel, chunk=chunk),
        grid=(b, HG_HEADS, t // tt),
        in_specs=[pl.BlockSpec((1, tt, LANES), col(COL_Q)),
                  pl.BlockSpec((1, tt, LANES), col(COL_F)),
                  pl.BlockSpec((1, tt, LANES), col(COL_I)),
                  pl.BlockSpec((1, tt, LANES), col(COL_G)),
                  pl.BlockSpec((1, LANES), lambda i, h, j: (0, h)),
                  pl.BlockSpec((1, LANES), lambda i, h, j: (0, h)),
                  pl.BlockSpec((1, 1, HG_DK, LANES), lambda i, h, j: (i, h, 0, 0))],
        out_specs=[pl.BlockSpec((1, tt, LANES), lambda i, h, j: (i, j, h)),
                   pl.BlockSpec((1, 1, HG_DK, LANES), lambda i, h, j: (i, h, 0, 0))],
        out_shape=[jax.ShapeDtypeStruct((b, t, D), BF16),
                   jax.ShapeDtypeStruct((b, HG_HEADS, HG_DK, LANES), F32)],
        scratch_shapes=[pltpu.VMEM((HG_DK, LANES), F32)],
        compiler_params=_cparams(("arbitrary", "arbitrary", "arbitrary")),
        name="gla",
    )(proj, proj_f, proj, proj, lb.reshape(1, D), hg_norm_w.reshape(1, D), s0)


def _ssd_kernel(xs_ref, b_ref, c_ref, z_ref, dt_ref, cwx_ref, cwb_ref, cwc_ref, cbx_ref, cbb_ref, cbc_ref,
                bufx_ref, bufb_ref, bufc_ref, dtb_ref, alog_ref, dpar_ref, sel_ref, exp_ref, nw_ref, h0_ref,
                y_ref, h_out_ref, h_ref, xpx_ref, xpb_ref, xpc_ref):
    L = xs_ref.shape[1]
    t_idx = pl.program_id(2)

    @pl.when(t_idx == 0)
    def _():
        h_ref[...] = h0_ref[0, 0]
        xpx_ref[0:8, :] = bufx_ref[0]
        xpb_ref[0:8, :] = bufb_ref[0]
        xpc_ref[0:8, :] = bufc_ref[0]

    def conv(xp_ref, raw, w_ref, bias_ref):
        xp_ref[8:8 + L, :] = raw.astype(F32)
        acc = bias_ref[...] + xp_ref[5:5 + L, :] * w_ref[0:1, :]
        for j in range(1, 4):
            acc = acc + xp_ref[5 + j:5 + j + L, :] * w_ref[j:j + 1, :]
        xp_ref[0:8, :] = xp_ref[L:L + 8, :]
        return _silu(acc)

    xs = conv(xpx_ref, xs_ref[0], cwx_ref, cbx_ref)
    bm = conv(xpb_ref, b_ref[0], cwb_ref, cbb_ref)
    cm = conv(xpc_ref, c_ref[0], cwc_ref, cbc_ref)
    bm_bf = bm.astype(BF16)
    cm_bf = cm.astype(BF16)

    sel = sel_ref[0]
    expand = exp_ref[...]
    dt_raw = _dot3_right(dt_ref[0], sel)
    dt = jax.nn.softplus(dt_raw + dtb_ref[0])
    a_neg = -jnp.exp(alog_ref[0])
    tril = _tril(L)
    ac = _dot3_left(tril.astype(BF16), dt * a_neg)
    a_last = ac[L - 1:L, :]

    def expand2(v):
        hi = v.astype(BF16)
        lo = (v - hi.astype(F32)).astype(BF16)
        return _dot(hi, expand) + _dot(lo, expand)

    dt_x = expand2(dt)
    eac_x = expand2(jnp.exp(ac))
    wd_x = expand2(jnp.exp(a_last - ac) * dt)
    row8 = lax.broadcasted_iota(I32, (8, LANES), 0)
    tail_x = expand2(jnp.where(row8 == 0, jnp.exp(a_last), jnp.where(row8 == 1, dpar_ref[0], 0.0)))
    decay_x = tail_x[0:1, :]
    d_x = tail_x[1:2, :]

    xdt = (xs * dt_x).astype(BF16)
    cbm = jnp.where(tril, _dot_nt(cm_bf, bm_bf), 0.0)
    ac_t = ac.T
    lane = lax.broadcasted_iota(I32, (L, LANES), 1)
    zero = jnp.zeros((), BF16)
    ys = []
    for pair in range(SSM_R // 2):
        xq = xdt[:, pair * LANES:(pair + 1) * LANES]
        acc = None
        for half in range(2):
            r = 2 * pair + half
            seg = jnp.exp(jnp.minimum(ac[:, r:r + 1] - ac_t[r:r + 1, :], 0.0))
            w = (cbm * seg).astype(BF16)
            keep = (lane < SSM_P) if half == 0 else (lane >= SSM_P)
            part = _dot(w, jnp.where(keep, xq, zero))
            acc = part if acc is None else acc + part
        ys.append(acc)
    h_old = h_ref[...]
    y = jnp.concatenate(ys, axis=1) + _dot(cm_bf, h_old.astype(BF16)) * eac_x
    xw = (xs * wd_x).astype(BF16)
    h_ref[...] = decay_x * h_old + _dot_tn(bm_bf, xw)

    y = (y + d_x * xs) * _silu(z_ref[0].astype(F32))
    y = y * lax.rsqrt(jnp.mean(y * y, axis=-1, keepdims=True) + EPS) * nw_ref[...]
    y_ref[0] = y.astype(BF16)

    @pl.when(t_idx == pl.num_programs(2) - 1)
    def _():
        h_out_ref[0, 0] = h_ref[...]


def _ssd(proj, proj_f, conv_w, conv_b, conv_buf8, dt_bias, a_log, ssm_d, ssm_norm_w, h0t, chunk):
    b, t, _ = proj.shape
    L = chunk
    grp = lambda v: jnp.pad(v.reshape(SSM_G, 1, SSM_R), ((0, 0), (0, 0), (0, LANES - SSM_R)))
    hh = jnp.arange(LANES)
    sel = (hh[None, :, None] == (jnp.arange(SSM_G)[:, None, None] * SSM_R + hh[None, None, :])) & (hh[None, None, :] < SSM_R)
    expand = (hh[:, None] == (jnp.arange(SSM_GW)[None, :] // SSM_P))
    cb2 = conv_b.reshape(1, SSM_XBC)
    nb_x, nb_b, nb_c = 0, SSM_INNER // LANES, (SSM_INNER + SSM_G * SSM_N) // LANES
    pcol = lambda base, width: (lambda i, g, j: (i, j, base // width + g))
    return pl.pallas_call(
        _ssd_kernel,
        grid=(b, SSM_G, t // L),
        in_specs=[pl.BlockSpec((1, L, SSM_GW), pcol(COL_XS, SSM_GW)),
                  pl.BlockSpec((1, L, SSM_N), pcol(COL_B, SSM_N)),
                  pl.BlockSpec((1, L, SSM_N), pcol(COL_C, SSM_N)),
                  pl.BlockSpec((1, L, SSM_GW), pcol(COL_Z, SSM_GW)),
                  pl.BlockSpec((1, L, LANES), lambda i, g, j: (i, j, COL_DT // LANES)),
                  pl.BlockSpec((4, SSM_GW), lambda i, g, j: (0, g)),
                  pl.BlockSpec((4, SSM_N), lambda i, g, j: (0, nb_b + g)),
                  pl.BlockSpec((4, SSM_N), lambda i, g, j: (0, nb_c + g)),
                  pl.BlockSpec((1, SSM_GW), lambda i, g, j: (0, g)),
                  pl.BlockSpec((1, SSM_N), lambda i, g, j: (0, nb_b + g)),
                  pl.BlockSpec((1, SSM_N), lambda i, g, j: (0, nb_c + g)),
                  pl.BlockSpec((1, 8, SSM_GW), lambda i, g, j: (i, 0, g)),
                  pl.BlockSpec((1, 8, SSM_N), lambda i, g, j: (i, 0, nb_b + g)),
                  pl.BlockSpec((1, 8, SSM_N), lambda i, g, j: (i, 0, nb_c + g)),
                  pl.BlockSpec((1, 1, LANES), lambda i, g, j: (g, 0, 0)),
                  pl.BlockSpec((1, 1, LANES), lambda i, g, j: (g, 0, 0)),
                  pl.BlockSpec((1, 1, LANES), lambda i, g, j: (g, 0, 0)),
                  pl.BlockSpec((1, LANES, LANES), lambda i, g, j: (g, 0, 0)),
                  pl.BlockSpec((LANES, SSM_GW), lambda i, g, j: (0, 0)),
                  pl.BlockSpec((1, SSM_GW), lambda i, g, j: (0, g)),
                  pl.BlockSpec((1, 1, SSM_N, SSM_GW), lambda i, g, j: (i, g, 0, 0))],
        out_specs=[pl.BlockSpec((1, L, SSM_GW), lambda i, g, j: (i, j, g)),
                   pl.BlockSpec((1, 1, SSM_N, SSM_GW), lambda i, g, j: (i, g, 0, 0))],
        out_shape=[jax.ShapeDtypeStruct((b, t, SSM_INNER), BF16),
                   jax.ShapeDtypeStruct((b, SSM_G, SSM_N, SSM_GW), F32)],
        scratch_shapes=[pltpu.VMEM((SSM_N, SSM_GW), F32),
                        pltpu.VMEM((L + 8, SSM_GW), F32),
                        pltpu.VMEM((L + 8, SSM_N), F32),
                        pltpu.VMEM((L + 8, SSM_N), F32)],
        compiler_params=_cparams(("arbitrary", "arbitrary", "arbitrary")),
        name="ssd",
    )(proj, proj, proj, proj, proj_f, conv_w, conv_w, conv_w, cb2, cb2, cb2, conv_buf8, conv_buf8, conv_buf8,
      grp(dt_bias), grp(a_log), grp(ssm_d), sel.astype(BF16), expand.astype(BF16),
      ssm_norm_w.reshape(1, SSM_INNER), h0t)


def _merge_kernel(o_ref, y_ref, ga_ref, gb_ref, x_ref, g1_ref, sh2_ref, sc2_ref, nw_ref, wa_ref, wb_ref, wo_ref,
                  x1_ref, h2_ref):
    bb, tt, _ = x_ref.shape
    n = bb * tt
    pa = _dot(o_ref[...].reshape(n, D), wa_ref[...])
    pb = _dot(y_ref[...].reshape(n, SSM_INNER), wb_ref[...])
    ga = ga_ref[...].reshape(n, D).astype(F32)
    gb = gb_ref[...].reshape(n, D).astype(F32)
    merged = jax.nn.sigmoid(ga) * pa + jax.nn.sigmoid(gb) * pb
    mix = _dot(merged.astype(BF16), wo_ref[...]).reshape(bb, tt, D)
    x1 = x_ref[...] + g1_ref[...] * mix
    x1_ref[...] = x1
    xn = x1 * lax.rsqrt(jnp.mean(x1 * x1, axis=-1, keepdims=True) + EPS)
    h2_ref[...] = (xn * nw_ref[...] * (1.0 + sc2_ref[...]) + sh2_ref[...]).astype(BF16)


def _merge(o_g, y_g, proj, x, mod3, norm2_w, wa, wb, wo, bb, tt):
    b, t, _ = x.shape
    row = lambda i, j: (i, j, 0)
    modc = lambda c: (lambda i, j: (i, 0, c))
    const = lambda i, j: (0, 0)
    return pl.pallas_call(
        _merge_kernel,
        grid=(b // bb, t // tt),
        in_specs=[pl.BlockSpec((bb, tt, D), row),
                  pl.BlockSpec((bb, tt, SSM_INNER), row),
                  pl.BlockSpec((bb, tt, D), lambda i, j: (i, j, COL_GA // D)),
                  pl.BlockSpec((bb, tt, D), lambda i, j: (i, j, COL_GB // D)),
                  pl.BlockSpec((bb, tt, D), row),
                  pl.BlockSpec((bb, 1, D), modc(2)),
                  pl.BlockSpec((bb, 1, D), modc(3)),
                  pl.BlockSpec((bb, 1, D), modc(4)),
                  pl.BlockSpec((1, D), const),
                  pl.BlockSpec((D, D), const),
                  pl.BlockSpec((SSM_INNER, D), const),
                  pl.BlockSpec((D, D), const)],
        out_specs=[pl.BlockSpec((bb, tt, D), row), pl.BlockSpec((bb, tt, D), row)],
        out_shape=[jax.ShapeDtypeStruct((b, t, D), F32), jax.ShapeDtypeStruct((b, t, D), BF16)],
        compiler_params=_cparams(("arbitrary", "arbitrary")),
        name="merge",
    )(o_g, y_g, proj, proj, x, mod3, mod3, mod3, norm2_w.reshape(1, D), wa, wb, wo)


_NO_ROW = float(1 << 20)


def _tree(op, xs):
    xs = list(xs)
    while len(xs) > 1:
        xs = [op(xs[i], xs[i + 1]) for i in range(0, len(xs) - 1, 2)] + ([xs[-1]] if len(xs) % 2 else [])
    return xs[0]


def _topk_list(vals, ids, k):
    out_v, out_i = [], []
    for _ in range(k):
        m = _tree(jnp.maximum, vals)
        i = _tree(jnp.minimum, [jnp.where(v == m, c, _NO_ROW) for v, c in zip(vals, ids)])
        out_v.append(m)
        out_i.append(i)
        vals = [jnp.where(i == c, -jnp.inf, v) for v, c in zip(vals, ids)]
    return out_v, out_i


def _sort_pairs(n):
    pairs = []
    p = 1
    while p < n:
        k = p
        while k >= 1:
            for j in range(k % p, n - k, 2 * k):
                for i in range(min(k, n - j - k)):
                    if (i + j) // (2 * p) == (i + j + k) // (2 * p):
                        pairs.append((i + j, i + j + k))
            k //= 2
        p *= 2
    return pairs


def _first(a, b):
    (va, ia), (vb, ib) = a, b
    eq = va == vb
    return jnp.where(eq, ib, va) > jnp.where(eq, ia, vb)


def _cex(a, b):
    take_a = _first(a, b)
    return ((jnp.maximum(a[0], b[0]), jnp.where(take_a, a[1], b[1])),
            (jnp.minimum(a[0], b[0]), jnp.where(take_a, b[1], a[1])))


def _merge_top(a, b):
    k = len(a)
    c = []
    for i in range(k):
        x, y = a[i], b[k - 1 - i]
        c.append((jnp.maximum(x[0], y[0]), jnp.where(_first(x, y), x[1], y[1])))
    s = k // 2
    while s >= 1:
        for i in range(k):
            if (i & s) == 0:
                c[i], c[i + s] = _cex(c[i], c[i + s])
        s //= 2
    return c


def _topk_sorted(vals, ids, k, presorted=()):
    pairs = _sort_pairs(k)
    items = list(zip(vals, ids))
    blocks = list(presorted)
    for i in range(0, len(items), k):
        blk = items[i:i + k]
        for lo, hi in pairs:
            blk[lo], blk[hi] = _cex(blk[lo], blk[hi])
        blocks.append(blk)
    while len(blocks) > 1:
        nxt = [_merge_top(blocks[i], blocks[i + 1]) for i in range(0, len(blocks) - 1, 2)]
        blocks = nxt + ([blocks[-1]] if len(blocks) % 2 else [])
    return [v for v, _ in blocks[0]], [i for _, i in blocks[0]]


def _pick_list(table, sel):
    out = table[0]
    for i in range(1, len(table)):
        out = jnp.where(sel == float(i), table[i], out)
    return out


def _pair_candidates(v1, v2):
    kk = PEER_TOPK
    first_row = [(v1[0] + v2[j], float(j)) for j in range(kk)]
    vals, ids = [], []
    for i in range(1, kk):
        for j in range(kk):
            if (i + 1) * (j + 1) <= kk:
                vals.append(v1[i] + v2[j])
                ids.append(float(i * kk + j))
    pad = (-len(vals)) % kk
    vals += [jnp.full_like(v1[0], -jnp.inf)] * pad
    ids += [_NO_ROW + float(p) for p in range(pad)]
    return first_row, vals, ids


ROUTE_PITCH = LANES + 8


def _route_kernel(h2_ref, wq_ref, k1_ref, k2_ref, e1_ref, e2_ref, gw_ref, q_ref, s1_ref, s2_ref):
    tm = h2_ref.shape[0]
    groups = tm // LANES
    half = PEER_DK // 2
    q = _dot(h2_ref[...], wq_ref[...])
    for c in range(2 * PEER_HEADS):
        q_ref[c] = q[:, c * half:(c + 1) * half].astype(BF16)
    key_ids = [float(k) for k in range(PEER_NK)]

    def per_key_scores(keys, qh, s_ref):
        for g in range(groups):
            s_ref[g * ROUTE_PITCH:g * ROUTE_PITCH + PEER_NK, :] = _dot_nt(keys, qh[g * LANES:(g + 1) * LANES, :])
        return [s_ref[pl.ds(k, groups, stride=ROUTE_PITCH), :] for k in range(PEER_NK)]

    def body(h, carry):
        v1, i1 = _topk_sorted(per_key_scores(k1_ref[h], q_ref[2 * h], s1_ref), key_ids, PEER_TOPK)
        v2, i2 = _topk_sorted(per_key_scores(k2_ref[h], q_ref[2 * h + 1], s2_ref), key_ids, PEER_TOPK)
        first_row, cand, flat = _pair_candidates(v1, v2)
        sv, sf = _topk_sorted(cand, flat, PEER_TOPK, presorted=[first_row])
        p = [jnp.exp(v - sv[0]) for v in sv]
        inv = 1.0 / _tree(jnp.add, p)
        for k in range(PEER_TOPK):
            sel_i = jnp.floor(sf[k] * (1.0 / PEER_TOPK))
            sel_j = sf[k] - sel_i * float(PEER_TOPK)
            slot = pl.ds(h * PEER_TOPK + k, 1)
            e1_ref[:, slot, :] = _pick_list(i1, sel_i).astype(I32).reshape(groups, 1, LANES)
            e2_ref[:, slot, :] = _pick_list(i2, sel_j).astype(I32).reshape(groups, 1, LANES)
            gw_ref[:, slot, :] = (p[k] * inv).reshape(groups, 1, LANES)
        return carry

    lax.fori_loop(0, PEER_HEADS, body, 0)


def _route(h2, wq, k1, k2, tm):
    n = h2.shape[0]
    groups = tm // LANES
    out = jax.ShapeDtypeStruct((n // LANES, PEER_HK, LANES), I32)
    ospec = pl.BlockSpec((groups, PEER_HK, LANES), lambda i: (i, 0, 0))
    const3 = lambda i: (0, 0, 0)
    return pl.pallas_call(
        _route_kernel,
        grid=(n // tm,),
        in_specs=[pl.BlockSpec((tm, D), lambda i: (i, 0)),
                  pl.BlockSpec((D, PEER_HEADS * PEER_DK), lambda i: (0, 0)),
                  pl.BlockSpec((PEER_HEADS, PEER_NK, PEER_DK // 2), const3),
                  pl.BlockSpec((PEER_HEADS, PEER_NK, PEER_DK // 2), const3)],
        out_specs=[ospec, ospec, ospec],
        out_shape=[out, out, jax.ShapeDtypeStruct((n // LANES, PEER_HK, LANES), F32)],
        scratch_shapes=[pltpu.VMEM((2 * PEER_HEADS, tm, PEER_DK // 2), BF16),
                        pltpu.VMEM((groups * ROUTE_PITCH, LANES), F32),
                        pltpu.VMEM((groups * ROUTE_PITCH, LANES), F32)],
        compiler_params=_cparams(("arbitrary",)),
        name="route",
    )(h2, wq, k1, k2)


EXPERT_AB = 16
W_PAD = 8
SCATTER_UNROLL = 32


def _experts_kernel(h2_ref, e1t_ref, e2t_ref, gwt_ref, u_ref, v_ref, x1_ref, g2_ref, fw_ref, y_ref,
                    e1_ref, e2_ref, gw_ref, gact_ref, coef_ref, w_ref, acc_ref, act_ref):
    tm = h2_ref.shape[0]
    phase = pl.program_id(2)
    j = pl.program_id(3)
    nj = pl.num_programs(3)

    def select_from_act(step):
        e1 = e1_ref[...]
        e2 = e2_ref[...]
        g = jnp.zeros((tm, PEER_HK), F32)
        for a in range(EXPERT_AB):
            picked = jnp.take_along_axis(act_ref[:, a * LANES:(a + 1) * LANES], e2, axis=1,
                                         mode="promise_in_bounds")
            g = jnp.where(e1 == step * EXPERT_AB + a, picked, g)
        return g

    @pl.when((phase == 0) & (j == 0))
    def _():
        for r in range(tm // LANES):
            rows = slice(r * LANES, (r + 1) * LANES)
            e1_ref[rows, :] = e1t_ref[r].astype(F32).T.astype(I32)
            e2_ref[rows, :] = e2t_ref[r].astype(F32).T.astype(I32)
            gw_ref[rows, :] = gwt_ref[r].T
        gact_ref[...] = jnp.zeros_like(gact_ref)
        act_ref[...] = jnp.zeros_like(act_ref)

    @pl.when(phase == 0)
    def _():
        gact_ref[...] += select_from_act(j - 1)
        act_ref[...] = _dot_nt(h2_ref[...], u_ref[...])

    @pl.when((phase == 1) & (j == 0))
    def _():
        ga = gact_ref[...] + select_from_act(nj - 1)
        gelu = 0.5 * ga * (1.0 + lax.erf(ga * (0.5 ** 0.5)))
        coef_ref[...] = gw_ref[...] * gelu
        acc_ref[...] = jnp.zeros_like(acc_ref)

    half_steps = nj // 2
    a_half = PEER_NK // 2
    pitch = tm + W_PAD

    @pl.when((phase == 1) & (j % half_steps == 0))
    def _():
        a0 = (j // half_steps) * a_half
        rows_a = lax.broadcasted_iota(I32, (a_half, PEER_HK), 0) + a0
        rows_b = lax.broadcasted_iota(I32, (PEER_NK, PEER_HK), 0)

        def scatter(t, carry):
            ct = jnp.where(rows_a == e1_ref[pl.ds(t, 1), :], coef_ref[pl.ds(t, 1), :], 0.0).astype(BF16)
            bt = (rows_b == e2_ref[pl.ds(t, 1), :]).astype(BF16)
            w_ref[pl.ds(t, a_half, stride=pitch), :] = _dot_nt(ct, bt)
            return carry

        lax.fori_loop(0, tm, scatter, 0, unroll=SCATTER_UNROLL)

    @pl.when(phase == 1)
    def _():
        a_loc = (j % half_steps) * EXPERT_AB
        parts = [w_ref[pl.ds(pl.multiple_of((a_loc + a) * pitch, 8), tm), :].astype(BF16) for a in range(EXPERT_AB)]
        acc_ref[...] += _dot(jnp.concatenate(parts, axis=1), v_ref[...])

    @pl.when((phase == 1) & (j == nj - 1))
    def _():
        bb, tt, _ = x1_ref.shape
        x2 = x1_ref[...] + g2_ref[...] * acc_ref[...].reshape(bb, tt, D)
        y_ref[...] = x2 * lax.rsqrt(jnp.mean(x2 * x2, axis=-1, keepdims=True) + EPS) * fw_ref[...]


def _experts(h2, e1t, e2t, gwt, u, v, x1, mod3, final_w, bb, tt):
    b, t, _ = x1.shape
    tm = bb * tt
    nj = PEER_NK // EXPERT_AB
    eb = EXPERT_AB * PEER_NK
    nt = t // tt
    tspec = pl.BlockSpec((tm // LANES, PEER_HK, LANES), lambda i, k, p, j: (i * nt + k, 0, 0))
    rspec = pl.BlockSpec((bb, tt, D), lambda i, k, p, j: (i, k, 0))
    return pl.pallas_call(
        _experts_kernel,
        grid=(b // bb, nt, 2, nj),
        in_specs=[pl.BlockSpec((tm, D), lambda i, k, p, j: (i * nt + k, 0)), tspec, tspec, tspec,
                  pl.BlockSpec((eb, D), lambda i, k, p, j: (jnp.where(p == 0, j, nj - 1), 0)),
                  pl.BlockSpec((eb, D), lambda i, k, p, j: (jnp.where(p == 1, j, 0), 0)),
                  rspec,
                  pl.BlockSpec((bb, 1, D), lambda i, k, p, j: (i, 0, 5)),
                  pl.BlockSpec((1, D), lambda i, k, p, j: (0, 0))],
        out_specs=rspec,
        out_shape=jax.ShapeDtypeStruct((b, t, D), F32),
        scratch_shapes=[pltpu.VMEM((tm, PEER_HK), I32),
                        pltpu.VMEM((tm, PEER_HK), I32),
                        pltpu.VMEM((tm, PEER_HK), F32),
                        pltpu.VMEM((tm, PEER_HK), F32),
                        pltpu.VMEM((tm, PEER_HK), F32),
                        pltpu.VMEM(((PEER_NK // 2) * (tm + W_PAD), PEER_NK), F32),
                        pltpu.VMEM((tm, D), F32),
                        pltpu.VMEM((tm, EXPERT_AB * LANES), F32)],
        compiler_params=_cparams(("arbitrary", "arbitrary", "arbitrary", "arbitrary")),
        name="experts",
    )(h2, e1t, e2t, gwt, u, v, x1, mod3, final_w.reshape(1, D))


def _tile(total, cap):
    t = min(total, cap)
    assert total % t == 0
    return t


def _group(x, mod, s_hg, s_ssm, conv_buf, lb, w, gla_chunk, ssd_chunk):
    b, t, _ = x.shape
    n = b * t
    mod3 = mod.reshape(b, 1, N_MOD * D)
    if t >= 256:
        bb, tt = 1, _tile(t, 1024)
    else:
        bb, tt = b, t
    proj, proj_f = _inproj(x, mod3, w["norm1_w"], w["w_in"], bb, tt)

    o_g, s_hg_new = _gla(proj, proj_f, lb, w["hgrn_norm_w"], s_hg, _tile(t, 1024), gla_chunk)

    conv_buf8 = jnp.pad(conv_buf, ((0, 0), (5, 0), (0, 0)))
    h0t = jnp.swapaxes(s_ssm.reshape(b, SSM_G, SSM_R * SSM_P, SSM_N), 2, 3)
    y_g, h_t = _ssd(proj, proj_f, w["conv_w"], w["conv_b"], conv_buf8, w["dt_bias"], w["a_log"], w["ssm_d"],
                    w["ssm_norm_w"], h0t, ssd_chunk)
    s_ssm_new = jnp.swapaxes(h_t, 2, 3).reshape(b, SSM_HEADS, SSM_P, SSM_N)
    assert t >= 3
    conv_new = proj[:, t - 3:, COL_XS:COL_XS + SSM_XBC].astype(F32)

    if t >= 256:
        bb2, tt2 = 1, _tile(t, 256)
    else:
        bb2, tt2 = b, t
    x1, h2 = _merge(o_g, y_g, proj, x, mod3, w["norm2_w"], w["w_branch_a"], w["w_branch_b"], w["w_out"], bb2, tt2)

    h2f = h2.reshape(n, D)
    e1t, e2t, gwt = _route(h2f, w["peer_wq"], w["peer_keys1"], w["peer_keys2"], _tile(n, 1024))
    bb3, tt3 = (1, _tile(t, 512)) if t >= 256 else (b, t)
    y = _experts(h2f, e1t, e2t, gwt, w["peer_u"], w["peer_v"], x1, mod3, w["final_norm_w"], bb3, tt3)
    return y, s_hg_new, s_ssm_new, conv_new


def kernel(x_prompt, x_sample, c_prompt, c_sample, state_hgrn, state_ssm, state_conv, w_ada, b_ada, norm1_w, w_in,
           hgrn_lower_bounds, hgrn_norm_w, conv_w, conv_b, dt_bias, a_log, ssm_d, ssm_norm_w, w_branch_a,
           w_branch_b, w_out, norm2_w, peer_wq, peer_keys1, peer_keys2, peer_u, peer_v, final_norm_w):
    depth = w_ada.shape[0]
    assert depth == 1
    bp, tp, _ = x_prompt.shape
    bs, ts, _ = x_sample.shape
    lb_all = jnp.cumsum(jax.nn.softmax(hgrn_lower_bounds.astype(F32), axis=0), axis=0)

    l = 0
    wi = w_in[l]
    c_f, c_i, c_dt, c_ga = 1024, 2048, 9216, 9248
    w_pad = jnp.concatenate(
        [wi[:, :c_f], wi[:, c_i:c_dt], wi[:, c_ga:],
         wi[:, c_f:c_i], wi[:, c_dt:c_ga],
         jnp.zeros((D, PROJ_F32_COLS - (c_i - c_f) - SSM_HEADS), wi.dtype)], axis=1).astype(BF16)
    assert w_pad.shape[1] == PROJ_COLS
    w = dict(norm1_w=norm1_w[l], w_in=w_pad, hgrn_norm_w=hgrn_norm_w[l], conv_w=conv_w[l], conv_b=conv_b[l],
             dt_bias=dt_bias[l], a_log=a_log[l], ssm_d=ssm_d[l], ssm_norm_w=ssm_norm_w[l],
             w_branch_a=w_branch_a[l].astype(BF16), w_branch_b=w_branch_b[l].astype(BF16),
             w_out=w_out[l].astype(BF16), norm2_w=norm2_w[l], peer_wq=peer_wq[l].astype(BF16),
             peer_keys1=peer_keys1[l].astype(BF16), peer_keys2=peer_keys2[l].astype(BF16),
             peer_u=peer_u[l].astype(BF16), peer_v=peer_v[l].astype(BF16), final_norm_w=final_norm_w)

    mod = _adaln(jnp.concatenate([c_prompt, c_sample], axis=0), w_ada[l], b_ada[l])

    z_hg = jnp.zeros((bp, HG_HEADS, HG_DK, LANES), F32)
    z_ssm = jnp.zeros((bp, SSM_HEADS, SSM_P, SSM_N), F32)
    z_conv = jnp.zeros((bp, 3, SSM_XBC), F32)
    yp, hg_p, ssm_p, conv_p = _group(x_prompt, mod[:bp], z_hg, z_ssm, z_conv, lb_all[l], w,
                                     min(GLA_CHUNK, tp), min(SSD_CHUNK, tp))
    ys, hg_s, ssm_s, conv_s = _group(x_sample, mod[bp:], state_hgrn[l], state_ssm[l], state_conv[l], lb_all[l], w,
                                     min(GLA_CHUNK, ts), min(SSD_CHUNK, ts))
    return (yp, ys, hg_p[None], ssm_p[None], conv_p[None], hg_s[None], ssm_s[None], conv_s[None])
```

```python
import functools

import jax
import jax.numpy as jnp
from jax import lax
from jax.experimental import pallas as pl
from jax.experimental.pallas import tpu as pltpu

F32 = jnp.float32
BF16 = jnp.bfloat16
I32 = jnp.int32

EPS = 1e-6
LOG2E = 1.4426950408889634
D = 1024
N_MOD = 6
HG_HEADS = 8
HG_DK = 128
SSM_HEADS = 32
SSM_P = 64
SSM_G = 4
SSM_R = SSM_HEADS // SSM_G
SSM_N = 128
SSM_INNER = SSM_HEADS * SSM_P
SSM_GW = SSM_INNER // SSM_G
SSM_XBC = SSM_INNER + 2 * SSM_G * SSM_N
PEER_HEADS = 8
PEER_NK = 128
PEER_TOPK = 16
PEER_HK = PEER_HEADS * PEER_TOPK
PEER_DK = 256

LANES = 128
GLA_CHUNK = 64
SSD_CHUNK = 256
VMEM_LIMIT = 56 * 1024 * 1024

COL_Q = 0
COL_I = 1024
COL_G = 2048
COL_Z = 3072
COL_XS = 5120
COL_B = COL_XS + SSM_INNER
COL_C = COL_B + SSM_G * SSM_N
COL_GA = 8192
COL_GB = 9216
PROJ_BF16_COLS = 10240
COL_F = 0
COL_DT = 1024
PROJ_F32_COLS = 1280
PROJ_TN = 1280
PROJ_COLS = PROJ_BF16_COLS + PROJ_F32_COLS


def _cparams(sem):
    return pltpu.CompilerParams(dimension_semantics=sem, vmem_limit_bytes=VMEM_LIMIT)


def _silu(x):
    return x * jax.nn.sigmoid(x)


def _split3(x):
    hi = x.astype(BF16)
    r = x - hi.astype(F32)
    mid = r.astype(BF16)
    lo = (r - mid.astype(F32)).astype(BF16)
    return hi, mid, lo


def _dot(a, b):
    return jnp.dot(a, b, preferred_element_type=F32)


def _dot_nt(a, b):
    return lax.dot_general(a, b, (((1,), (1,)), ((), ())), preferred_element_type=F32)


def _dot_tn(a, b):
    return lax.dot_general(a, b, (((0,), (0,)), ((), ())), preferred_element_type=F32)


def _dot3_left(m_bf16, x):
    hi, mid, lo = _split3(x)
    return _dot(m_bf16, hi) + _dot(m_bf16, mid) + _dot(m_bf16, lo)


def _dot3_right(x, m_bf16):
    hi, mid, lo = _split3(x)
    return _dot(hi, m_bf16) + _dot(mid, m_bf16) + _dot(lo, m_bf16)


def _chunk_cumsum(x, chunk):
    pos = lax.broadcasted_iota(I32, x.shape, 0) % chunk
    s = 1
    while s < chunk:
        x = x + jnp.where(pos >= s, pltpu.roll(x, shift=s, axis=0), 0.0)
        s *= 2
    return x


def _tril(n):
    r = lax.broadcasted_iota(I32, (n, n), 0)
    c = lax.broadcasted_iota(I32, (n, n), 1)
    return r >= c


def _adaln_kernel(c_ref, w_ref, b_ref, o_ref):
    c = c_ref[...]
    o_ref[...] = _dot(_silu(c).astype(BF16), w_ref[...].astype(BF16)) + b_ref[...]


def _adaln(c, w_ada, b_ada):
    nb = c.shape[0]
    ncol = w_ada.shape[1]
    return pl.pallas_call(
        _adaln_kernel,
        grid=(ncol // D,),
        in_specs=[pl.BlockSpec((nb, D), lambda j: (0, 0)),
                  pl.BlockSpec((D, D), lambda j: (0, j)),
                  pl.BlockSpec((1, D), lambda j: (0, j))],
        out_specs=pl.BlockSpec((nb, D), lambda j: (0, j)),
        out_shape=jax.ShapeDtypeStruct((nb, ncol), F32),
        compiler_params=_cparams(("arbitrary",)),
        name="adaln",
    )(c, w_ada, b_ada.reshape(1, ncol))


def _inproj_kernel(x_ref, sh_ref, sc_ref, nw_ref, w_ref, ob_ref, of_ref, h_ref):
    bb, tt, _ = x_ref.shape
    n = pl.program_id(2)
    n_bf16 = PROJ_BF16_COLS // PROJ_TN

    @pl.when(n == 0)
    def _():
        x = x_ref[...]
        xn = x * lax.rsqrt(jnp.mean(x * x, axis=-1, keepdims=True) + EPS)
        h = xn * nw_ref[...] * (1.0 + sc_ref[...]) + sh_ref[...]
        h_ref[...] = h.reshape(bb * tt, D).astype(BF16)

    res = _dot(h_ref[...], w_ref[...]).reshape(bb, tt, -1)

    @pl.when(n < n_bf16)
    def _():
        ob_ref[...] = res.astype(BF16)

    @pl.when(n == n_bf16)
    def _():
        of_ref[...] = res


def _inproj(x, mod3, norm_w, w_pad, bb, tt):
    b, t, _ = x.shape
    n_bf16 = PROJ_BF16_COLS // PROJ_TN
    assert PROJ_F32_COLS == PROJ_TN
    return pl.pallas_call(
        _inproj_kernel,
        grid=(b // bb, t // tt, PROJ_COLS // PROJ_TN),
        in_specs=[pl.BlockSpec((bb, tt, D), lambda i, j, n: (i, j, 0)),
                  pl.BlockSpec((bb, 1, D), lambda i, j, n: (i, 0, 0)),
                  pl.BlockSpec((bb, 1, D), lambda i, j, n: (i, 0, 1)),
                  pl.BlockSpec((1, D), lambda i, j, n: (0, 0)),
                  pl.BlockSpec((D, PROJ_TN), lambda i, j, n: (0, n))],
        out_specs=[pl.BlockSpec((bb, tt, PROJ_TN), lambda i, j, n: (i, j, jnp.minimum(n, n_bf16 - 1))),
                   pl.BlockSpec((bb, tt, PROJ_TN), lambda i, j, n: (i, j, 0))],
        out_shape=[jax.ShapeDtypeStruct((b, t, PROJ_BF16_COLS), BF16),
                   jax.ShapeDtypeStruct((b, t, PROJ_F32_COLS), F32)],
        scratch_shapes=[pltpu.VMEM((bb * tt, D), BF16)],
        compiler_params=_cparams(("arbitrary", "arbitrary", "arbitrary")),
        name="inproj",
    )(x, mod3, mod3, norm_w.reshape(1, D), w_pad)


def _gla_kernel(q_ref, f_ref, i_ref, g_ref, lb_ref, nw_ref, s0_ref, o_ref, s_out_ref, s_ref, *, chunk):
    tt = q_ref.shape[1]
    nchunk = tt // chunk
    t_idx = pl.program_id(2)

    @pl.when(t_idx == 0)
    def _():
        s_ref[...] = s0_ref[0, 0].T

    lb = lb_ref[...]
    tril = _tril(chunk)
    mid = chunk // 2

    q_all = _silu(q_ref[0].astype(F32))
    ff = f_ref[0]
    logf = jnp.log(lb + (1.0 - lb) * jax.nn.sigmoid(ff))
    k_all = (1.0 - lb) * jax.nn.sigmoid(-ff)
    v_all = i_ref[0]
    g_all = _chunk_cumsum(logf, chunk)

    st = s_ref[...]
    outs = []
    for c in range(nchunk):
        sl = slice(c * chunk, (c + 1) * chunk)
        q, k, v, g = q_all[sl], k_all[sl], v_all[sl], g_all[sl]
        g_mid = g[mid:mid + 1, :]
        g_last = g[chunk - 1:chunk, :]
        qs = (q * jnp.exp(g - g_mid)).astype(BF16)
        ks = (k * jnp.exp(g_mid - g)).astype(BF16)
        att = jnp.where(tril, _dot_nt(qs, ks), 0.0).astype(BF16)
        outs.append(_dot(att, v) + _dot_nt((q * jnp.exp(g)).astype(BF16), st.astype(BF16)))
        kd = (k * jnp.exp(g_last - g)).astype(BF16)
        st = jnp.exp(g_last) * st + _dot_tn(v, kd)
    s_ref[...] = st
    o = jnp.concatenate(outs, axis=0) if nchunk > 1 else outs[0]
    on = o * lax.rsqrt(jnp.mean(o * o, axis=-1, keepdims=True) + EPS) * nw_ref[...]
    o_ref[0] = (on * _silu(g_ref[0].astype(F32))).astype(BF16)

    @pl.when(t_idx == pl.num_programs(2) - 1)
    def _():
        s_out_ref[0, 0] = s_ref[...].T


def _gla(proj, proj_f, lb, hg_norm_w, s0, tt, chunk):
    b, t, _ = proj.shape
    col = lambda base: (lambda i, h, j: (i, j, base // LANES + h))
    return pl.pallas_call(
        functools.partial(_gla_kernel, chunk=chunk),
        grid=(b, HG_HEADS, t // tt),
        in_specs=[pl.BlockSpec((1, tt, LANES), col(COL_Q)),
                  pl.BlockSpec((1, tt, LANES), col(COL_F)),
                  pl.BlockSpec((1, tt, LANES), col(COL_I)),
                  pl.BlockSpec((1, tt, LANES), col(COL_G)),
                  pl.BlockSpec((1, LANES), lambda i, h, j: (0, h)),
                  pl.BlockSpec((1, LANES), lambda i, h, j: (0, h)),
                  pl.BlockSpec((1, 1, HG_DK, LANES), lambda i, h, j: (i, h, 0, 0))],
        out_specs=[pl.BlockSpec((1, tt, LANES), lambda i, h, j: (i, j, h)),
                   pl.BlockSpec((1, 1, HG_DK, LANES), lambda i, h, j: (i, h, 0, 0))],
        out_shape=[jax.ShapeDtypeStruct((b, t, D), BF16),
                   jax.ShapeDtypeStruct((b, HG_HEADS, HG_DK, LANES), F32)],
        scratch_shapes=[pltpu.VMEM((HG_DK, LANES), F32)],
        compiler_params=_cparams(("arbitrary", "arbitrary", "arbitrary")),
        name="gla",
    )(proj, proj_f, proj, proj, lb.reshape(1, D), hg_norm_w.reshape(1, D), s0)


def _ssd_kernel(xs_ref, b_ref, c_ref, z_ref, dt_ref, cwx_ref, cwb_ref, cwc_ref, cbx_ref, cbb_ref, cbc_ref,
                bufx_ref, bufb_ref, bufc_ref, dtb_ref, alog_ref, dpar_ref, sel_ref, exp_ref, shift_ref, nw_ref, h0_ref,
                y_ref, h_out_ref, h_ref, xpx_ref, xpb_ref, xpc_ref):
    L = xs_ref.shape[1]
    t_idx = pl.program_id(2)

    @pl.when(t_idx == 0)
    def _():
        h_ref[...] = h0_ref[0, 0]
        xpx_ref[0:8, :] = bufx_ref[0]
        xpb_ref[0:8, :] = bufb_ref[0]
        xpc_ref[0:8, :] = bufc_ref[0]

    def conv(xp_ref, raw, w_ref, bias_ref):
        cur = raw.astype(F32)
        xp_ref[8:16, :] = cur[0:8, :]
        acc = bias_ref[...] + cur * w_ref[3:4, :]
        for j in range(3):
            delayed = _dot(shift_ref[j], raw)
            head = xp_ref[5 + j:13 + j, :]
            if L > 8:
                delayed = jnp.concatenate([head, delayed[8:, :]], axis=0)
            else:
                delayed = head
            acc = acc + delayed * w_ref[j:j + 1, :]
        xp_ref[0:8, :] = cur[L - 8:L, :]
        return _silu(acc)

    xs = conv(xpx_ref, xs_ref[0], cwx_ref, cbx_ref)
    bm = conv(xpb_ref, b_ref[0], cwb_ref, cbb_ref)
    cm = conv(xpc_ref, c_ref[0], cwc_ref, cbc_ref)
    bm_bf = bm.astype(BF16)
    cm_bf = cm.astype(BF16)

    sel = sel_ref[0]
    expand = exp_ref[...]
    dt_raw = _dot3_right(dt_ref[0], sel)
    dt = jax.nn.softplus(dt_raw + dtb_ref[0])
    a_neg = -jnp.exp(alog_ref[0])
    tril = _tril(L)
    ac = _dot3_left(tril.astype(BF16), dt * a_neg)
    a_last = ac[L - 1:L, :]

    def expand2(v):
        hi = v.astype(BF16)
        lo = (v - hi.astype(F32)).astype(BF16)
        return _dot(hi, expand) + _dot(lo, expand)

    dt_x = expand2(dt)
    eac_x = expand2(jnp.exp(ac))
    wd_x = expand2(jnp.exp(a_last - ac) * dt)
    row8 = lax.broadcasted_iota(I32, (8, LANES), 0)
    tail_x = expand2(jnp.where(row8 == 0, jnp.exp(a_last), jnp.where(row8 == 1, dpar_ref[0], 0.0)))
    decay_x = tail_x[0:1, :]
    d_x = tail_x[1:2, :]

    xdt = (xs * dt_x).astype(BF16)
    cbm = jnp.where(tril, _dot_nt(cm_bf, bm_bf), 0.0)
    ac2 = ac * LOG2E
    ac2_t = ac2.T
    lane = lax.broadcasted_iota(I32, (L, LANES), 1)
    zero = jnp.zeros((), BF16)
    ys = []
    for pair in range(SSM_R // 2):
        xq = xdt[:, pair * LANES:(pair + 1) * LANES]
        acc = None
        for half in range(2):
            r = 2 * pair + half
            seg = jnp.exp2(jnp.minimum(ac2[:, r:r + 1] - ac2_t[r:r + 1, :], 0.0))
            w = (cbm * seg).astype(BF16)
            keep = (lane < SSM_P) if half == 0 else (lane >= SSM_P)
            part = _dot(w, jnp.where(keep, xq, zero))
            acc = part if acc is None else acc + part
        ys.append(acc)
    h_old = h_ref[...]
    y = jnp.concatenate(ys, axis=1) + _dot(cm_bf, h_old.astype(BF16)) * eac_x
    xw = (xs * wd_x).astype(BF16)
    h_ref[...] = decay_x * h_old + _dot_tn(bm_bf, xw)

    y = (y + d_x * xs) * _silu(z_ref[0].astype(F32))
    y = y * lax.rsqrt(jnp.mean(y * y, axis=-1, keepdims=True) + EPS) * nw_ref[...]
    y_ref[0] = y.astype(BF16)

    @pl.when(t_idx == pl.num_programs(2) - 1)
    def _():
        h_out_ref[0, 0] = h_ref[...]


def _ssd(proj, proj_f, conv_w, conv_b, conv_buf8, dt_bias, a_log, ssm_d, ssm_norm_w, h0t, chunk):
    b, t, _ = proj.shape
    L = chunk
    grp = lambda v: jnp.pad(v.reshape(SSM_G, 1, SSM_R), ((0, 0), (0, 0), (0, LANES - SSM_R)))
    hh = jnp.arange(LANES)
    sel = (hh[None, :, None] == (jnp.arange(SSM_G)[:, None, None] * SSM_R + hh[None, None, :])) & (hh[None, None, :] < SSM_R)
    expand = (hh[:, None] == (jnp.arange(SSM_GW)[None, :] // SSM_P))
    tpos = jnp.arange(L)
    shift = jnp.stack([tpos[None, :] == tpos[:, None] - (3 - j) for j in range(3)])
    cb2 = conv_b.reshape(1, SSM_XBC)
    nb_x, nb_b, nb_c = 0, SSM_INNER // LANES, (SSM_INNER + SSM_G * SSM_N) // LANES
    pcol = lambda base, width: (lambda i, g, j: (i, j, base // width + g))
    return pl.pallas_call(
        _ssd_kernel,
        grid=(b, SSM_G, t // L),
        in_specs=[pl.BlockSpec((1, L, SSM_GW), pcol(COL_XS, SSM_GW)),
                  pl.BlockSpec((1, L, SSM_N), pcol(COL_B, SSM_N)),
                  pl.BlockSpec((1, L, SSM_N), pcol(COL_C, SSM_N)),
                  pl.BlockSpec((1, L, SSM_GW), pcol(COL_Z, SSM_GW)),
                  pl.BlockSpec((1, L, LANES), lambda i, g, j: (i, j, COL_DT // LANES)),
                  pl.BlockSpec((4, SSM_GW), lambda i, g, j: (0, g)),
                  pl.BlockSpec((4, SSM_N), lambda i, g, j: (0, nb_b + g)),
                  pl.BlockSpec((4, SSM_N), lambda i, g, j: (0, nb_c + g)),
                  pl.BlockSpec((1, SSM_GW), lambda i, g, j: (0, g)),
                  pl.BlockSpec((1, SSM_N), lambda i, g, j: (0, nb_b + g)),
                  pl.BlockSpec((1, SSM_N), lambda i, g, j: (0, nb_c + g)),
                  pl.BlockSpec((1, 8, SSM_GW), lambda i, g, j: (i, 0, g)),
                  pl.BlockSpec((1, 8, SSM_N), lambda i, g, j: (i, 0, nb_b + g)),
                  pl.BlockSpec((1, 8, SSM_N), lambda i, g, j: (i, 0, nb_c + g)),
                  pl.BlockSpec((1, 1, LANES), lambda i, g, j: (g, 0, 0)),
                  pl.BlockSpec((1, 1, LANES), lambda i, g, j: (g, 0, 0)),
                  pl.BlockSpec((1, 1, LANES), lambda i, g, j: (g, 0, 0)),
                  pl.BlockSpec((1, LANES, LANES), lambda i, g, j: (g, 0, 0)),
                  pl.BlockSpec((LANES, SSM_GW), lambda i, g, j: (0, 0)),
                  pl.BlockSpec((3, L, L), lambda i, g, j: (0, 0, 0)),
                  pl.BlockSpec((1, SSM_GW), lambda i, g, j: (0, g)),
                  pl.BlockSpec((1, 1, SSM_N, SSM_GW), lambda i, g, j: (i, g, 0, 0))],
        out_specs=[pl.BlockSpec((1, L, SSM_GW), lambda i, g, j: (i, j, g)),
                   pl.BlockSpec((1, 1, SSM_N, SSM_GW), lambda i, g, j: (i, g, 0, 0))],
        out_shape=[jax.ShapeDtypeStruct((b, t, SSM_INNER), BF16),
                   jax.ShapeDtypeStruct((b, SSM_G, SSM_N, SSM_GW), F32)],
        scratch_shapes=[pltpu.VMEM((SSM_N, SSM_GW), F32),
                        pltpu.VMEM((L + 8, SSM_GW), F32),
                        pltpu.VMEM((L + 8, SSM_N), F32),
                        pltpu.VMEM((L + 8, SSM_N), F32)],
        compiler_params=_cparams(("arbitrary", "arbitrary", "arbitrary")),
        name="ssd",
    )(proj, proj, proj, proj, proj_f, conv_w, conv_w, conv_w, cb2, cb2, cb2, conv_buf8, conv_buf8, conv_buf8,
      grp(dt_bias), grp(a_log), grp(ssm_d), sel.astype(BF16), expand.astype(BF16), shift.astype(BF16),
      ssm_norm_w.reshape(1, SSM_INNER), h0t)


def _merge_kernel(o_ref, y_ref, ga_ref, gb_ref, x_ref, g1_ref, sh2_ref, sc2_ref, nw_ref, wa_ref, wb_ref, wo_ref,
                  x1_ref, h2_ref):
    bb, tt, _ = x_ref.shape
    n = bb * tt
    pa = _dot(o_ref[...].reshape(n, D), wa_ref[...])
    pb = _dot(y_ref[...].reshape(n, SSM_INNER), wb_ref[...])
    ga = ga_ref[...].reshape(n, D).astype(F32)
    gb = gb_ref[...].reshape(n, D).astype(F32)
    merged = jax.nn.sigmoid(ga) * pa + jax.nn.sigmoid(gb) * pb
    mix = _dot(merged.astype(BF16), wo_ref[...]).reshape(bb, tt, D)
    x1 = x_ref[...] + g1_ref[...] * mix
    x1_ref[...] = x1
    xn = x1 * lax.rsqrt(jnp.mean(x1 * x1, axis=-1, keepdims=True) + EPS)
    h2_ref[...] = (xn * nw_ref[...] * (1.0 + sc2_ref[...]) + sh2_ref[...]).astype(BF16)


def _merge(o_g, y_g, proj, x, mod3, norm2_w, wa, wb, wo, bb, tt):
    b, t, _ = x.shape
    row = lambda i, j: (i, j, 0)
    modc = lambda c: (lambda i, j: (i, 0, c))
    const = lambda i, j: (0, 0)
    return pl.pallas_call(
        _merge_kernel,
        grid=(b // bb, t // tt),
        in_specs=[pl.BlockSpec((bb, tt, D), row),
                  pl.BlockSpec((bb, tt, SSM_INNER), row),
                  pl.BlockSpec((bb, tt, D), lambda i, j: (i, j, COL_GA // D)),
                  pl.BlockSpec((bb, tt, D), lambda i, j: (i, j, COL_GB // D)),
                  pl.BlockSpec((bb, tt, D), row),
                  pl.BlockSpec((bb, 1, D), modc(2)),
                  pl.BlockSpec((bb, 1, D), modc(3)),
                  pl.BlockSpec((bb, 1, D), modc(4)),
                  pl.BlockSpec((1, D), const),
                  pl.BlockSpec((D, D), const),
                  pl.BlockSpec((SSM_INNER, D), const),
                  pl.BlockSpec((D, D), const)],
        out_specs=[pl.BlockSpec((bb, tt, D), row), pl.BlockSpec((bb, tt, D), row)],
        out_shape=[jax.ShapeDtypeStruct((b, t, D), F32), jax.ShapeDtypeStruct((b, t, D), BF16)],
        compiler_params=_cparams(("arbitrary", "arbitrary")),
        name="merge",
    )(o_g, y_g, proj, proj, x, mod3, mod3, mod3, norm2_w.reshape(1, D), wa, wb, wo)


_NO_ROW = float(1 << 20)


def _tree(op, xs):
    xs = list(xs)
    while len(xs) > 1:
        xs = [op(xs[i], xs[i + 1]) for i in range(0, len(xs) - 1, 2)] + ([xs[-1]] if len(xs) % 2 else [])
    return xs[0]


def _topk_list(vals, ids, k):
    out_v, out_i = [], []
    for _ in range(k):
        m = _tree(jnp.maximum, vals)
        i = _tree(jnp.minimum, [jnp.where(v == m, c, _NO_ROW) for v, c in zip(vals, ids)])
        out_v.append(m)
        out_i.append(i)
        vals = [jnp.where(i == c, -jnp.inf, v) for v, c in zip(vals, ids)]
    return out_v, out_i


def _sort_pairs(n):
    pairs = []
    p = 1
    while p < n:
        k = p
        while k >= 1:
            for j in range(k % p, n - k, 2 * k):
                for i in range(min(k, n - j - k)):
                    if (i + j) // (2 * p) == (i + j + k) // (2 * p):
                        pairs.append((i + j, i + j + k))
            k //= 2
        p *= 2
    return pairs


def _first(a, b):
    (va, ia), (vb, ib) = a, b
    eq = va == vb
    return jnp.where(eq, ib, va) > jnp.where(eq, ia, vb)


def _cex(a, b):
    take_a = _first(a, b)
    return ((jnp.maximum(a[0], b[0]), jnp.where(take_a, a[1], b[1])),
            (jnp.minimum(a[0], b[0]), jnp.where(take_a, b[1], a[1])))


def _merge_top(a, b):
    k = len(a)
    c = []
    for i in range(k):
        x, y = a[i], b[k - 1 - i]
        c.append((jnp.maximum(x[0], y[0]), jnp.where(_first(x, y), x[1], y[1])))
    s = k // 2
    while s >= 1:
        for i in range(k):
            if (i & s) == 0:
                c[i], c[i + s] = _cex(c[i], c[i + s])
        s //= 2
    return c


def _topk_sorted(vals, ids, k, presorted=()):
    pairs = _sort_pairs(k)
    items = list(zip(vals, ids))
    blocks = list(presorted)
    for i in range(0, len(items), k):
        blk = items[i:i + k]
        for lo, hi in pairs:
            blk[lo], blk[hi] = _cex(blk[lo], blk[hi])
        blocks.append(blk)
    while len(blocks) > 1:
        nxt = [_merge_top(blocks[i], blocks[i + 1]) for i in range(0, len(blocks) - 1, 2)]
        blocks = nxt + ([blocks[-1]] if len(blocks) % 2 else [])
    return [v for v, _ in blocks[0]], [i for _, i in blocks[0]]


def _pick_list(table, sel):
    out = table[0]
    for i in range(1, len(table)):
        out = jnp.where(sel == float(i), table[i], out)
    return out


def _pair_candidates(v1, v2):
    kk = PEER_TOPK
    first_row = [(v1[0] + v2[j], float(j)) for j in range(kk)]
    vals, ids = [], []
    for i in range(1, kk):
        for j in range(kk):
            if (i + 1) * (j + 1) <= kk:
                vals.append(v1[i] + v2[j])
                ids.append(float(i * kk + j))
    pad = (-len(vals)) % kk
    vals += [jnp.full_like(v1[0], -jnp.inf)] * pad
    ids += [_NO_ROW + float(p) for p in range(pad)]
    return first_row, vals, ids


ROUTE_PITCH = LANES + 8


def _route_kernel(h2_ref, wq_ref, k1_ref, k2_ref, e1_ref, e2_ref, gw_ref, q_ref, s1_ref, s2_ref):
    tm = h2_ref.shape[0]
    groups = tm // LANES
    half = PEER_DK // 2
    q = _dot(h2_ref[...], wq_ref[...])
    for c in range(2 * PEER_HEADS):
        q_ref[c] = q[:, c * half:(c + 1) * half].astype(BF16)
    key_ids = [float(k) for k in range(PEER_NK)]

    def per_key_scores(keys, qh, s_ref):
        for g in range(groups):
            s_ref[g * ROUTE_PITCH:g * ROUTE_PITCH + PEER_NK, :] = _dot_nt(keys, qh[g * LANES:(g + 1) * LANES, :])
        return [s_ref[pl.ds(k, groups, stride=ROUTE_PITCH), :] for k in range(PEER_NK)]

    def body(h, carry):
        v1, i1 = _topk_sorted(per_key_scores(k1_ref[h], q_ref[2 * h], s1_ref), key_ids, PEER_TOPK)
        v2, i2 = _topk_sorted(per_key_scores(k2_ref[h], q_ref[2 * h + 1], s2_ref), key_ids, PEER_TOPK)
        first_row, cand, flat = _pair_candidates(v1, v2)
        sv, sf = _topk_sorted(cand, flat, PEER_TOPK, presorted=[first_row])
        p = [jnp.exp(v - sv[0]) for v in sv]
        inv = 1.0 / _tree(jnp.add, p)
        for k in range(PEER_TOPK):
            sel_i = jnp.floor(sf[k] * (1.0 / PEER_TOPK))
            sel_j = sf[k] - sel_i * float(PEER_TOPK)
            slot = pl.ds(h * PEER_TOPK + k, 1)
            e1_ref[:, slot, :] = _pick_list(i1, sel_i).astype(I32).reshape(groups, 1, LANES)
            e2_ref[:, slot, :] = _pick_list(i2, sel_j).astype(I32).reshape(groups, 1, LANES)
            gw_ref[:, slot, :] = (p[k] * inv).reshape(groups, 1, LANES)
        return carry

    lax.fori_loop(0, PEER_HEADS, body, 0)


def _route(h2, wq, k1, k2, tm):
    n = h2.shape[0]
    groups = tm // LANES
    out = jax.ShapeDtypeStruct((n // LANES, PEER_HK, LANES), I32)
    ospec = pl.BlockSpec((groups, PEER_HK, LANES), lambda i: (i, 0, 0))
    const3 = lambda i: (0, 0, 0)
    return pl.pallas_call(
        _route_kernel,
        grid=(n // tm,),
        in_specs=[pl.BlockSpec((tm, D), lambda i: (i, 0)),
                  pl.BlockSpec((D, PEER_HEADS * PEER_DK), lambda i: (0, 0)),
                  pl.BlockSpec((PEER_HEADS, PEER_NK, PEER_DK // 2), const3),
                  pl.BlockSpec((PEER_HEADS, PEER_NK, PEER_DK // 2), const3)],
        out_specs=[ospec, ospec, ospec],
        out_shape=[out, out, jax.ShapeDtypeStruct((n // LANES, PEER_HK, LANES), F32)],
        scratch_shapes=[pltpu.VMEM((2 * PEER_HEADS, tm, PEER_DK // 2), BF16),
                        pltpu.VMEM((groups * ROUTE_PITCH, LANES), F32),
                        pltpu.VMEM((groups * ROUTE_PITCH, LANES), F32)],
        compiler_params=_cparams(("arbitrary",)),
        name="route",
    )(h2, wq, k1, k2)


EXPERT_AB = 16
W_PAD = 8
SCATTER_UNROLL = 32


def _experts_kernel(h2_ref, e1t_ref, e2t_ref, gwt_ref, u_ref, v_ref, x1_ref, g2_ref, fw_ref, y_ref,
                    e1_ref, e2_ref, gw_ref, gact_ref, coef_ref, w_ref, acc_ref, act_ref):
    tm = h2_ref.shape[0]
    phase = pl.program_id(2)
    j = pl.program_id(3)
    nj = pl.num_programs(3)

    def select_from_act(step):
        e1 = e1_ref[...]
        e2 = e2_ref[...]
        g = jnp.zeros((tm, PEER_HK), F32)
        for a in range(EXPERT_AB):
            picked = jnp.take_along_axis(act_ref[:, a * LANES:(a + 1) * LANES], e2, axis=1,
                                         mode="promise_in_bounds")
            g = jnp.where(e1 == step * EXPERT_AB + a, picked, g)
        return g

    @pl.when((phase == 0) & (j == 0))
    def _():
        for r in range(tm // LANES):
            rows = slice(r * LANES, (r + 1) * LANES)
            e1_ref[rows, :] = e1t_ref[r].astype(F32).T.astype(I32)
            e2_ref[rows, :] = e2t_ref[r].astype(F32).T.astype(I32)
            gw_ref[rows, :] = gwt_ref[r].T
        gact_ref[...] = jnp.zeros_like(gact_ref)
        act_ref[...] = jnp.zeros_like(act_ref)

    @pl.when(phase == 0)
    def _():
        gact_ref[...] += select_from_act(j - 1)
        act_ref[...] = _dot_nt(h2_ref[...], u_ref[...])

    @pl.when((phase == 1) & (j == 0))
    def _():
        ga = gact_ref[...] + select_from_act(nj - 1)
        gelu = 0.5 * ga * (1.0 + lax.erf(ga * (0.5 ** 0.5)))
        coef_ref[...] = gw_ref[...] * gelu
        acc_ref[...] = jnp.zeros_like(acc_ref)

    half_steps = nj // 2
    a_half = PEER_NK // 2
    pitch = tm + W_PAD

    @pl.when((phase == 1) & (j % half_steps == 0))
    def _():
        a0 = (j // half_steps) * a_half
        rows_a = lax.broadcasted_iota(I32, (a_half, PEER_HK), 0) + a0
        rows_b = lax.broadcasted_iota(I32, (PEER_NK, PEER_HK), 0)

        def scatter(t, carry):
            ct = jnp.where(rows_a == e1_ref[pl.ds(t, 1), :], coef_ref[pl.ds(t, 1), :], 0.0).astype(BF16)
            bt = (rows_b == e2_ref[pl.ds(t, 1), :]).astype(BF16)
            w_ref[pl.ds(t, a_half, stride=pitch), :] = _dot_nt(ct, bt)
            return carry

        lax.fori_loop(0, tm, scatter, 0, unroll=SCATTER_UNROLL)

    @pl.when(phase == 1)
    def _():
        a_loc = (j % half_steps) * EXPERT_AB
        parts = [w_ref[pl.ds(pl.multiple_of((a_loc + a) * pitch, 8), tm), :].astype(BF16) for a in range(EXPERT_AB)]
        acc_ref[...] += _dot(jnp.concatenate(parts, axis=1), v_ref[...])

    @pl.when((phase == 1) & (j == nj - 1))
    def _():
        bb, tt, _ = x1_ref.shape
        x2 = x1_ref[...] + g2_ref[...] * acc_ref[...].reshape(bb, tt, D)
        y_ref[...] = x2 * lax.rsqrt(jnp.mean(x2 * x2, axis=-1, keepdims=True) + EPS) * fw_ref[...]


def _experts(h2, e1t, e2t, gwt, u, v, x1, mod3, final_w, bb, tt):
    b, t, _ = x1.shape
    tm = bb * tt
    nj = PEER_NK // EXPERT_AB
    eb = EXPERT_AB * PEER_NK
    nt = t // tt
    tspec = pl.BlockSpec((tm // LANES, PEER_HK, LANES), lambda i, k, p, j: (i * nt + k, 0, 0))
    rspec = pl.BlockSpec((bb, tt, D), lambda i, k, p, j: (i, k, 0))
    return pl.pallas_call(
        _experts_kernel,
        grid=(b // bb, nt, 2, nj),
        in_specs=[pl.BlockSpec((tm, D), lambda i, k, p, j: (i * nt + k, 0)), tspec, tspec, tspec,
                  pl.BlockSpec((eb, D), lambda i, k, p, j: (jnp.where(p == 0, j, nj - 1), 0)),
                  pl.BlockSpec((eb, D), lambda i, k, p, j: (jnp.where(p == 1, j, 0), 0)),
                  rspec,
                  pl.BlockSpec((bb, 1, D), lambda i, k, p, j: (i, 0, 5)),
                  pl.BlockSpec((1, D), lambda i, k, p, j: (0, 0))],
        out_specs=rspec,
        out_shape=jax.ShapeDtypeStruct((b, t, D), F32),
        scratch_shapes=[pltpu.VMEM((tm, PEER_HK), I32),
                        pltpu.VMEM((tm, PEER_HK), I32),
                        pltpu.VMEM((tm, PEER_HK), F32),
                        pltpu.VMEM((tm, PEER_HK), F32),
                        pltpu.VMEM((tm, PEER_HK), F32),
                        pltpu.VMEM(((PEER_NK // 2) * (tm + W_PAD), PEER_NK), F32),
                        pltpu.VMEM((tm, D), F32),
                        pltpu.VMEM((tm, EXPERT_AB * LANES), F32)],
        compiler_params=_cparams(("arbitrary", "arbitrary", "arbitrary", "arbitrary")),
        name="experts",
    )(h2, e1t, e2t, gwt, u, v, x1, mod3, final_w.reshape(1, D))


def _tile(total, cap):
    t = min(total, cap)
    assert total % t == 0
    return t


def _group(x, mod, s_hg, s_ssm, conv_buf, lb, w, gla_chunk, ssd_chunk):
    b, t, _ = x.shape
    n = b * t
    mod3 = mod.reshape(b, 1, N_MOD * D)
    if t >= 256:
        bb, tt = 1, _tile(t, 1024)
    else:
        bb, tt = b, t
    proj, proj_f = _inproj(x, mod3, w["norm1_w"], w["w_in"], bb, tt)

    o_g, s_hg_new = _gla(proj, proj_f, lb, w["hgrn_norm_w"], s_hg, _tile(t, 1024), gla_chunk)

    conv_buf8 = jnp.pad(conv_buf, ((0, 0), (5, 0), (0, 0)))
    h0t = jnp.swapaxes(s_ssm.reshape(b, SSM_G, SSM_R * SSM_P, SSM_N), 2, 3)
    y_g, h_t = _ssd(proj, proj_f, w["conv_w"], w["conv_b"], conv_buf8, w["dt_bias"], w["a_log"], w["ssm_d"],
                    w["ssm_norm_w"], h0t, ssd_chunk)
    s_ssm_new = jnp.swapaxes(h_t, 2, 3).reshape(b, SSM_HEADS, SSM_P, SSM_N)
    assert t >= 3
    conv_new = proj[:, t - 3:, COL_XS:COL_XS + SSM_XBC].astype(F32)

    if t >= 256:
        bb2, tt2 = 1, _tile(t, 256)
    else:
        bb2, tt2 = b, t
    x1, h2 = _merge(o_g, y_g, proj, x, mod3, w["norm2_w"], w["w_branch_a"], w["w_branch_b"], w["w_out"], bb2, tt2)

    h2f = h2.reshape(n, D)
    e1t, e2t, gwt = _route(h2f, w["peer_wq"], w["peer_keys1"], w["peer_keys2"], _tile(n, 1024))
    bb3, tt3 = (1, _tile(t, 512)) if t >= 256 else (b, t)
    y = _experts(h2f, e1t, e2t, gwt, w["peer_u"], w["peer_v"], x1, mod3, w["final_norm_w"], bb3, tt3)
    return y, s_hg_new, s_ssm_new, conv_new


def kernel(x_prompt, x_sample, c_prompt, c_sample, state_hgrn, state_ssm, state_conv, w_ada, b_ada, norm1_w, w_in,
           hgrn_lower_bounds, hgrn_norm_w, conv_w, conv_b, dt_bias, a_log, ssm_d, ssm_norm_w, w_branch_a,
           w_branch_b, w_out, norm2_w, peer_wq, peer_keys1, peer_keys2, peer_u, peer_v, final_norm_w):
    depth = w_ada.shape[0]
    assert depth == 1
    bp, tp, _ = x_prompt.shape
    bs, ts, _ = x_sample.shape
    lb_all = jnp.cumsum(jax.nn.softmax(hgrn_lower_bounds.astype(F32), axis=0), axis=0)

    l = 0
    wi = w_in[l]
    c_f, c_i, c_dt, c_ga = 1024, 2048, 9216, 9248
    w_pad = jnp.concatenate(
        [wi[:, :c_f], wi[:, c_i:c_dt], wi[:, c_ga:],
         wi[:, c_f:c_i], wi[:, c_dt:c_ga],
         jnp.zeros((D, PROJ_F32_COLS - (c_i - c_f) - SSM_HEADS), wi.dtype)], axis=1).astype(BF16)
    assert w_pad.shape[1] == PROJ_COLS
    w = dict(norm1_w=norm1_w[l], w_in=w_pad, hgrn_norm_w=hgrn_norm_w[l], conv_w=conv_w[l], conv_b=conv_b[l],
             dt_bias=dt_bias[l], a_log=a_log[l], ssm_d=ssm_d[l], ssm_norm_w=ssm_norm_w[l],
             w_branch_a=w_branch_a[l].astype(BF16), w_branch_b=w_branch_b[l].astype(BF16),
             w_out=w_out[l].astype(BF16), norm2_w=norm2_w[l], peer_wq=peer_wq[l].astype(BF16),
             peer_keys1=peer_keys1[l].astype(BF16), peer_keys2=peer_keys2[l].astype(BF16),
             peer_u=peer_u[l].astype(BF16), peer_v=peer_v[l].astype(BF16), final_norm_w=final_norm_w)

    mod = _adaln(jnp.concatenate([c_prompt, c_sample], axis=0), w_ada[l], b_ada[l])

    z_hg = jnp.zeros((bp, HG_HEADS, HG_DK, LANES), F32)
    z_ssm = jnp.zeros((bp, SSM_HEADS, SSM_P, SSM_N), F32)
    z_conv = jnp.zeros((bp, 3, SSM_XBC), F32)
    yp, hg_p, ssm_p, conv_p = _group(x_prompt, mod[:bp], z_hg, z_ssm, z_conv, lb_all[l], w,
                                     min(GLA_CHUNK, tp), min(SSD_CHUNK, tp))
    ys, hg_s, ssm_s, conv_s = _group(x_sample, mod[bp:], state_hgrn[l], state_ssm[l], state_conv[l], lb_all[l], w,
                                     min(GLA_CHUNK, ts), min(SSD_CHUNK, ts))
    return (yp, ys, hg_p[None], ssm_p[None], conv_p[None], hg_s[None], ssm_s[None], conv_s[None])
```

```python
import functools

import jax
import jax.numpy as jnp
from jax import lax
from jax.experimental import pallas as pl
from jax.experimental.pallas import tpu as pltpu

F32 = jnp.float32
BF16 = jnp.bfloat16
I32 = jnp.int32

EPS = 1e-6
LOG2E = 1.4426950408889634
D = 1024
N_MOD = 6
HG_HEADS = 8
HG_DK = 128
SSM_HEADS = 32
SSM_P = 64
SSM_G = 4
SSM_R = SSM_HEADS // SSM_G
SSM_N = 128
SSM_INNER = SSM_HEADS * SSM_P
SSM_GW = SSM_INNER // SSM_G
SSM_XBC = SSM_INNER + 2 * SSM_G * SSM_N
PEER_HEADS = 8
PEER_NK = 128
PEER_TOPK = 16
PEER_HK = PEER_HEADS * PEER_TOPK
PEER_DK = 256

LANES = 128
GLA_CHUNK = 64
SSD_CHUNK = 256
VMEM_LIMIT = 56 * 1024 * 1024

COL_Q = 0
COL_I = 1024
COL_G = 2048
COL_Z = 3072
COL_XS = 5120
COL_B = COL_XS + SSM_INNER
COL_C = COL_B + SSM_G * SSM_N
COL_GA = 8192
COL_GB = 9216
PROJ_BF16_COLS = 10240
COL_F = 0
COL_DT = 1024
PROJ_F32_COLS = 1280
PROJ_TN = 1280
PROJ_COLS = PROJ_BF16_COLS + PROJ_F32_COLS


def _cparams(sem):
    return pltpu.CompilerParams(dimension_semantics=sem, vmem_limit_bytes=VMEM_LIMIT)


def _silu(x):
    return x * jax.nn.sigmoid(x)


def _split3(x):
    hi = x.astype(BF16)
    r = x - hi.astype(F32)
    mid = r.astype(BF16)
    lo = (r - mid.astype(F32)).astype(BF16)
    return hi, mid, lo


def _dot(a, b):
    return jnp.dot(a, b, preferred_element_type=F32)


def _dot_nt(a, b):
    return lax.dot_general(a, b, (((1,), (1,)), ((), ())), preferred_element_type=F32)


def _dot_tn(a, b):
    return lax.dot_general(a, b, (((0,), (0,)), ((), ())), preferred_element_type=F32)


def _dot3_left(m_bf16, x):
    hi, mid, lo = _split3(x)
    return _dot(m_bf16, hi) + _dot(m_bf16, mid) + _dot(m_bf16, lo)


def _dot3_right(x, m_bf16):
    hi, mid, lo = _split3(x)
    return _dot(hi, m_bf16) + _dot(mid, m_bf16) + _dot(lo, m_bf16)


def _chunk_cumsum(x, chunk):
    pos = lax.broadcasted_iota(I32, x.shape, 0) % chunk
    s = 1
    while s < chunk:
        x = x + jnp.where(pos >= s, pltpu.roll(x, shift=s, axis=0), 0.0)
        s *= 2
    return x


def _tril(n):
    r = lax.broadcasted_iota(I32, (n, n), 0)
    c = lax.broadcasted_iota(I32, (n, n), 1)
    return r >= c


def _adaln_kernel(c_ref, w_ref, b_ref, o_ref):
    c = c_ref[...]
    o_ref[...] = _dot(_silu(c).astype(BF16), w_ref[...].astype(BF16)) + b_ref[...]


def _adaln(c, w_ada, b_ada):
    nb = c.shape[0]
    ncol = w_ada.shape[1]
    return pl.pallas_call(
        _adaln_kernel,
        grid=(ncol // D,),
        in_specs=[pl.BlockSpec((nb, D), lambda j: (0, 0)),
                  pl.BlockSpec((D, D), lambda j: (0, j)),
                  pl.BlockSpec((1, D), lambda j: (0, j))],
        out_specs=pl.BlockSpec((nb, D), lambda j: (0, j)),
        out_shape=jax.ShapeDtypeStruct((nb, ncol), F32),
        compiler_params=_cparams(("arbitrary",)),
        name="adaln",
    )(c, w_ada, b_ada.reshape(1, ncol))


def _inproj_kernel(x_ref, sh_ref, sc_ref, nw_ref, w_ref, ob_ref, of_ref, h_ref):
    bb, tt, _ = x_ref.shape
    n = pl.program_id(2)
    n_bf16 = PROJ_BF16_COLS // PROJ_TN

    @pl.when(n == 0)
    def _():
        x = x_ref[...]
        xn = x * lax.rsqrt(jnp.mean(x * x, axis=-1, keepdims=True) + EPS)
        h = xn * nw_ref[...] * (1.0 + sc_ref[...]) + sh_ref[...]
        h_ref[...] = h.reshape(bb * tt, D).astype(BF16)

    res = _dot(h_ref[...], w_ref[...]).reshape(bb, tt, -1)

    @pl.when(n < n_bf16)
    def _():
        ob_ref[...] = res.astype(BF16)

    @pl.when(n == n_bf16)
    def _():
        of_ref[...] = res


def _inproj(x, mod3, norm_w, w_pad, bb, tt):
    b, t, _ = x.shape
    n_bf16 = PROJ_BF16_COLS // PROJ_TN
    assert PROJ_F32_COLS == PROJ_TN
    return pl.pallas_call(
        _inproj_kernel,
        grid=(b // bb, t // tt, PROJ_COLS // PROJ_TN),
        in_specs=[pl.BlockSpec((bb, tt, D), lambda i, j, n: (i, j, 0)),
                  pl.BlockSpec((bb, 1, D), lambda i, j, n: (i, 0, 0)),
                  pl.BlockSpec((bb, 1, D), lambda i, j, n: (i, 0, 1)),
                  pl.BlockSpec((1, D), lambda i, j, n: (0, 0)),
                  pl.BlockSpec((D, PROJ_TN), lambda i, j, n: (0, n))],
        out_specs=[pl.BlockSpec((bb, tt, PROJ_TN), lambda i, j, n: (i, j, jnp.minimum(n, n_bf16 - 1))),
                   pl.BlockSpec((bb, tt, PROJ_TN), lambda i, j, n: (i, j, 0))],
        out_shape=[jax.ShapeDtypeStruct((b, t, PROJ_BF16_COLS), BF16),
                   jax.ShapeDtypeStruct((b, t, PROJ_F32_COLS), F32)],
        scratch_shapes=[pltpu.VMEM((bb * tt, D), BF16)],
        compiler_params=_cparams(("arbitrary", "arbitrary", "arbitrary")),
        name="inproj",
    )(x, mod3, mod3, norm_w.reshape(1, D), w_pad)


def _gla_kernel(q_ref, f_ref, i_ref, g_ref, lb_ref, nw_ref, s0_ref, o_ref, s_out_ref, s_ref, *, chunk):
    tt = q_ref.shape[1]
    nchunk = tt // chunk
    t_idx = pl.program_id(2)

    @pl.when(t_idx == 0)
    def _():
        s_ref[...] = s0_ref[0, 0].T

    lb = lb_ref[...]
    tril = _tril(chunk)
    mid = chunk // 2

    q_all = _silu(q_ref[0].astype(F32))
    ff = f_ref[0]
    logf = jnp.log(lb + (1.0 - lb) * jax.nn.sigmoid(ff))
    k_all = (1.0 - lb) * jax.nn.sigmoid(-ff)
    v_all = i_ref[0]
    g_all = _chunk_cumsum(logf, chunk)

    st = s_ref[...]
    outs = []
    for c in range(nchunk):
        sl = slice(c * chunk, (c + 1) * chunk)
        q, k, v, g = q_all[sl], k_all[sl], v_all[sl], g_all[sl]
        g_mid = g[mid:mid + 1, :]
        g_last = g[chunk - 1:chunk, :]
        qs = (q * jnp.exp(g - g_mid)).astype(BF16)
        ks = (k * jnp.exp(g_mid - g)).astype(BF16)
        att = jnp.where(tril, _dot_nt(qs, ks), 0.0).astype(BF16)
        outs.append(_dot(att, v) + _dot_nt((q * jnp.exp(g)).astype(BF16), st.astype(BF16)))
        kd = (k * jnp.exp(g_last - g)).astype(BF16)
        st = jnp.exp(g_last) * st + _dot_tn(v, kd)
    s_ref[...] = st
    o = jnp.concatenate(outs, axis=0) if nchunk > 1 else outs[0]
    on = o * lax.rsqrt(jnp.mean(o * o, axis=-1, keepdims=True) + EPS) * nw_ref[...]
    o_ref[0] = (on * _silu(g_ref[0].astype(F32))).astype(BF16)

    @pl.when(t_idx == pl.num_programs(2) - 1)
    def _():
        s_out_ref[0, 0] = s_ref[...].T


def _gla(proj, proj_f, lb, hg_norm_w, s0, tt, chunk):
    b, t, _ = proj.shape
    col = lambda base: (lambda i, h, j: (i, j, base // LANES + h))
    return pl.pallas_call(
        functools.partial(_gla_kernel, chunk=chunk),
        grid=(b, HG_HEADS, t // tt),
        in_specs=[pl.BlockSpec((1, tt, LANES), col(COL_Q)),
                  pl.BlockSpec((1, tt, LANES), col(COL_F)),
                  pl.BlockSpec((1, tt, LANES), col(COL_I)),
                  pl.BlockSpec((1, tt, LANES), col(COL_G)),
                  pl.BlockSpec((1, LANES), lambda i, h, j: (0, h)),
                  pl.BlockSpec((1, LANES), lambda i, h, j: (0, h)),
                  pl.BlockSpec((1, 1, HG_DK, LANES), lambda i, h, j: (i, h, 0, 0))],
        out_specs=[pl.BlockSpec((1, tt, LANES), lambda i, h, j: (i, j, h)),
                   pl.BlockSpec((1, 1, HG_DK, LANES), lambda i, h, j: (i, h, 0, 0))],
        out_shape=[jax.ShapeDtypeStruct((b, t, D), BF16),
                   jax.ShapeDtypeStruct((b, HG_HEADS, HG_DK, LANES), F32)],
        scratch_shapes=[pltpu.VMEM((HG_DK, LANES), F32)],
        compiler_params=_cparams(("arbitrary", "arbitrary", "arbitrary")),
        name="gla",
    )(proj, proj_f, proj, proj, lb.reshape(1, D), hg_norm_w.reshape(1, D), s0)


def _ssd_kernel(xs_ref, b_ref, c_ref, z_ref, dt_ref, cwx_ref, cwb_ref, cwc_ref, cbx_ref, cbb_ref, cbc_ref,
                bufx_ref, bufb_ref, bufc_ref, dtb_ref, alog_ref, dpar_ref, sel_ref, exp_ref, shift_ref, nw_ref, h0_ref,
                y_ref, h_out_ref, h_ref, xpx_ref, xpb_ref, xpc_ref):
    L = xs_ref.shape[1]
    t_idx = pl.program_id(2)

    @pl.when(t_idx == 0)
    def _():
        h_ref[...] = h0_ref[0, 0]
        xpx_ref[0:8, :] = bufx_ref[0]
        xpb_ref[0:8, :] = bufb_ref[0]
        xpc_ref[0:8, :] = bufc_ref[0]

    def conv(xp_ref, raw, w_ref, bias_ref):
        cur = raw.astype(F32)
        xp_ref[8:16, :] = cur[0:8, :]
        acc = bias_ref[...] + cur * w_ref[3:4, :]
        for j in range(3):
            delayed = _dot(shift_ref[j], raw)
            head = xp_ref[5 + j:13 + j, :]
            if L > 8:
                delayed = jnp.concatenate([head, delayed[8:, :]], axis=0)
            else:
                delayed = head
            acc = acc + delayed * w_ref[j:j + 1, :]
        xp_ref[0:8, :] = cur[L - 8:L, :]
        return _silu(acc)

    xs = conv(xpx_ref, xs_ref[0], cwx_ref, cbx_ref)
    bm = conv(xpb_ref, b_ref[0], cwb_ref, cbb_ref)
    cm = conv(xpc_ref, c_ref[0], cwc_ref, cbc_ref)
    bm_bf = bm.astype(BF16)
    cm_bf = cm.astype(BF16)

    sel = sel_ref[0]
    expand = exp_ref[...]
    dt_raw = _dot3_right(dt_ref[0], sel)
    dt = jax.nn.softplus(dt_raw + dtb_ref[0])
    a_neg = -jnp.exp(alog_ref[0])
    tril = _tril(L)
    ac = _dot3_left(tril.astype(BF16), dt * a_neg)
    a_last = ac[L - 1:L, :]

    def expand2(v):
        hi = v.astype(BF16)
        lo = (v - hi.astype(F32)).astype(BF16)
        return _dot(hi, expand) + _dot(lo, expand)

    dt_x = expand2(dt)
    eac_x = expand2(jnp.exp(ac))
    wd_x = expand2(jnp.exp(a_last - ac) * dt)
    row8 = lax.broadcasted_iota(I32, (8, LANES), 0)
    tail_x = expand2(jnp.where(row8 == 0, jnp.exp(a_last), jnp.where(row8 == 1, dpar_ref[0], 0.0)))
    decay_x = tail_x[0:1, :]
    d_x = tail_x[1:2, :]

    xdt = (xs * dt_x).astype(BF16)
    cbm = jnp.where(tril, _dot_nt(cm_bf, bm_bf), 0.0)
    ac2 = ac * LOG2E
    ac2_t = ac2.T
    lane = lax.broadcasted_iota(I32, (L, LANES), 1)
    zero = jnp.zeros((), BF16)
    ys = []
    for pair in range(SSM_R // 2):
        xq = xdt[:, pair * LANES:(pair + 1) * LANES]
        acc = None
        for half in range(2):
            r = 2 * pair + half
            seg = jnp.exp2(jnp.minimum(ac2[:, r:r + 1] - ac2_t[r:r + 1, :], 0.0))
            w = (cbm * seg).astype(BF16)
            keep = (lane < SSM_P) if half == 0 else (lane >= SSM_P)
            part = _dot(w, jnp.where(keep, xq, zero))
            acc = part if acc is None else acc + part
        ys.append(acc)
    h_old = h_ref[...]
    y = jnp.concatenate(ys, axis=1) + _dot(cm_bf, h_old.astype(BF16)) * eac_x
    xw = (xs * wd_x).astype(BF16)
    h_ref[...] = decay_x * h_old + _dot_tn(bm_bf, xw)

    y = (y + d_x * xs) * _silu(z_ref[0].astype(F32))
    y = y * lax.rsqrt(jnp.mean(y * y, axis=-1, keepdims=True) + EPS) * nw_ref[...]
    y_ref[0] = y.astype(BF16)

    @pl.when(t_idx == pl.num_programs(2) - 1)
    def _():
        h_out_ref[0, 0] = h_ref[...]


def _ssd(proj, proj_f, conv_w, conv_b, conv_buf8, dt_bias, a_log, ssm_d, ssm_norm_w, h0t, chunk):
    b, t, _ = proj.shape
    L = chunk
    grp = lambda v: jnp.pad(v.reshape(SSM_G, 1, SSM_R), ((0, 0), (0, 0), (0, LANES - SSM_R)))
    hh = jnp.arange(LANES)
    sel = (hh[None, :, None] == (jnp.arange(SSM_G)[:, None, None] * SSM_R + hh[None, None, :])) & (hh[None, None, :] < SSM_R)
    expand = (hh[:, None] == (jnp.arange(SSM_GW)[None, :] // SSM_P))
    tpos = jnp.arange(L)
    shift = jnp.stack([tpos[None, :] == tpos[:, None] - (3 - j) for j in range(3)])
    cb2 = conv_b.reshape(1, SSM_XBC)
    nb_x, nb_b, nb_c = 0, SSM_INNER // LANES, (SSM_INNER + SSM_G * SSM_N) // LANES
    pcol = lambda base, width: (lambda i, g, j: (i, j, base // width + g))
    return pl.pallas_call(
        _ssd_kernel,
        grid=(b, SSM_G, t // L),
        in_specs=[pl.BlockSpec((1, L, SSM_GW), pcol(COL_XS, SSM_GW)),
                  pl.BlockSpec((1, L, SSM_N), pcol(COL_B, SSM_N)),
                  pl.BlockSpec((1, L, SSM_N), pcol(COL_C, SSM_N)),
                  pl.BlockSpec((1, L, SSM_GW), pcol(COL_Z, SSM_GW)),
                  pl.BlockSpec((1, L, LANES), lambda i, g, j: (i, j, COL_DT // LANES)),
                  pl.BlockSpec((4, SSM_GW), lambda i, g, j: (0, g)),
                  pl.BlockSpec((4, SSM_N), lambda i, g, j: (0, nb_b + g)),
                  pl.BlockSpec((4, SSM_N), lambda i, g, j: (0, nb_c + g)),
                  pl.BlockSpec((1, SSM_GW), lambda i, g, j: (0, g)),
                  pl.BlockSpec((1, SSM_N), lambda i, g, j: (0, nb_b + g)),
                  pl.BlockSpec((1, SSM_N), lambda i, g, j: (0, nb_c + g)),
                  pl.BlockSpec((1, 8, SSM_GW), lambda i, g, j: (i, 0, g)),
                  pl.BlockSpec((1, 8, SSM_N), lambda i, g, j: (i, 0, nb_b + g)),
                  pl.BlockSpec((1, 8, SSM_N), lambda i, g, j: (i, 0, nb_c + g)),
                  pl.BlockSpec((1, 1, LANES), lambda i, g, j: (g, 0, 0)),
                  pl.BlockSpec((1, 1, LANES), lambda i, g, j: (g, 0, 0)),
                  pl.BlockSpec((1, 1, LANES), lambda i, g, j: (g, 0, 0)),
                  pl.BlockSpec((1, LANES, LANES), lambda i, g, j: (g, 0, 0)),
                  pl.BlockSpec((LANES, SSM_GW), lambda i, g, j: (0, 0)),
                  pl.BlockSpec((3, L, L), lambda i, g, j: (0, 0, 0)),
                  pl.BlockSpec((1, SSM_GW), lambda i, g, j: (0, g)),
                  pl.BlockSpec((1, 1, SSM_N, SSM_GW), lambda i, g, j: (i, g, 0, 0))],
        out_specs=[pl.BlockSpec((1, L, SSM_GW), lambda i, g, j: (i, j, g)),
                   pl.BlockSpec((1, 1, SSM_N, SSM_GW), lambda i, g, j: (i, g, 0, 0))],
        out_shape=[jax.ShapeDtypeStruct((b, t, SSM_INNER), BF16),
                   jax.ShapeDtypeStruct((b, SSM_G, SSM_N, SSM_GW), F32)],
        scratch_shapes=[pltpu.VMEM((SSM_N, SSM_GW), F32),
                        pltpu.VMEM((L + 8, SSM_GW), F32),
                        pltpu.VMEM((L + 8, SSM_N), F32),
                        pltpu.VMEM((L + 8, SSM_N), F32)],
        compiler_params=_cparams(("arbitrary", "arbitrary", "arbitrary")),
        name="ssd",
    )(proj, proj, proj, proj, proj_f, conv_w, conv_w, conv_w, cb2, cb2, cb2, conv_buf8, conv_buf8, conv_buf8,
      grp(dt_bias), grp(a_log), grp(ssm_d), sel.astype(BF16), expand.astype(BF16), shift.astype(BF16),
      ssm_norm_w.reshape(1, SSM_INNER), h0t)


def _merge_kernel(o_ref, y_ref, ga_ref, gb_ref, x_ref, g1_ref, sh2_ref, sc2_ref, nw_ref, wa_ref, wb_ref, wo_ref,
                  x1_ref, h2_ref):
    bb, tt, _ = x_ref.shape
    n = bb * tt
    pa = _dot(o_ref[...].reshape(n, D), wa_ref[...])
    pb = _dot(y_ref[...].reshape(n, SSM_INNER), wb_ref[...])
    ga = ga_ref[...].reshape(n, D).astype(F32)
    gb = gb_ref[...].reshape(n, D).astype(F32)
    merged = jax.nn.sigmoid(ga) * pa + jax.nn.sigmoid(gb) * pb
    mix = _dot(merged.astype(BF16), wo_ref[...]).reshape(bb, tt, D)
    x1 = x_ref[...] + g1_ref[...] * mix
    x1_ref[...] = x1
    xn = x1 * lax.rsqrt(jnp.mean(x1 * x1, axis=-1, keepdims=True) + EPS)
    h2_ref[...] = (xn * nw_ref[...] * (1.0 + sc2_ref[...]) + sh2_ref[...]).astype(BF16)


def _merge(o_g, y_g, proj, x, mod3, norm2_w, wa, wb, wo, bb, tt):
    b, t, _ = x.shape
    row = lambda i, j: (i, j, 0)
    modc = lambda c: (lambda i, j: (i, 0, c))
    const = lambda i, j: (0, 0)
    return pl.pallas_call(
        _merge_kernel,
        grid=(b // bb, t // tt),
        in_specs=[pl.BlockSpec((bb, tt, D), row),
                  pl.BlockSpec((bb, tt, SSM_INNER), row),
                  pl.BlockSpec((bb, tt, D), lambda i, j: (i, j, COL_GA // D)),
                  pl.BlockSpec((bb, tt, D), lambda i, j: (i, j, COL_GB // D)),
                  pl.BlockSpec((bb, tt, D), row),
                  pl.BlockSpec((bb, 1, D), modc(2)),
                  pl.BlockSpec((bb, 1, D), modc(3)),
                  pl.BlockSpec((bb, 1, D), modc(4)),
                  pl.BlockSpec((1, D), const),
                  pl.BlockSpec((D, D), const),
                  pl.BlockSpec((SSM_INNER, D), const),
                  pl.BlockSpec((D, D), const)],
        out_specs=[pl.BlockSpec((bb, tt, D), row), pl.BlockSpec((bb, tt, D), row)],
        out_shape=[jax.ShapeDtypeStruct((b, t, D), F32), jax.ShapeDtypeStruct((b, t, D), BF16)],
        compiler_params=_cparams(("arbitrary", "arbitrary")),
        name="merge",
    )(o_g, y_g, proj, proj, x, mod3, mod3, mod3, norm2_w.reshape(1, D), wa, wb, wo)


_NO_ROW = float(1 << 20)


def _tree(op, xs):
    xs = list(xs)
    while len(xs) > 1:
        xs = [op(xs[i], xs[i + 1]) for i in range(0, len(xs) - 1, 2)] + ([xs[-1]] if len(xs) % 2 else [])
    return xs[0]


def _topk_list(vals, ids, k):
    out_v, out_i = [], []
    for _ in range(k):
        m = _tree(jnp.maximum, vals)
        i = _tree(jnp.minimum, [jnp.where(v == m, c, _NO_ROW) for v, c in zip(vals, ids)])
        out_v.append(m)
        out_i.append(i)
        vals = [jnp.where(i == c, -jnp.inf, v) for v, c in zip(vals, ids)]
    return out_v, out_i


def _sort_pairs(n):
    pairs = []
    p = 1
    while p < n:
        k = p
        while k >= 1:
            for j in range(k % p, n - k, 2 * k):
                for i in range(min(k, n - j - k)):
                    if (i + j) // (2 * p) == (i + j + k) // (2 * p):
                        pairs.append((i + j, i + j + k))
            k //= 2
        p *= 2
    return pairs


def _first(a, b):
    (va, ia), (vb, ib) = a, b
    eq = va == vb
    return jnp.where(eq, ib, va) > jnp.where(eq, ia, vb)


def _cex(a, b):
    take_a = _first(a, b)
    return ((jnp.maximum(a[0], b[0]), jnp.where(take_a, a[1], b[1])),
            (jnp.minimum(a[0], b[0]), jnp.where(take_a, b[1], a[1])))


def _merge_top(a, b):
    k = len(a)
    c = []
    for i in range(k):
        x, y = a[i], b[k - 1 - i]
        c.append((jnp.maximum(x[0], y[0]), jnp.where(_first(x, y), x[1], y[1])))
    s = k // 2
    while s >= 1:
        for i in range(k):
            if (i & s) == 0:
                c[i], c[i + s] = _cex(c[i], c[i + s])
        s //= 2
    return c


def _topk_sorted(vals, ids, k, presorted=()):
    pairs = _sort_pairs(k)
    items = list(zip(vals, ids))
    blocks = list(presorted)
    for i in range(0, len(items), k):
        blk = items[i:i + k]
        for lo, hi in pairs:
            blk[lo], blk[hi] = _cex(blk[lo], blk[hi])
        blocks.append(blk)
    while len(blocks) > 1:
        nxt = [_merge_top(blocks[i], blocks[i + 1]) for i in range(0, len(blocks) - 1, 2)]
        blocks = nxt + ([blocks[-1]] if len(blocks) % 2 else [])
    return [v for v, _ in blocks[0]], [i for _, i in blocks[0]]


def _pick_list(table, sel):
    out = table[0]
    for i in range(1, len(table)):
        out = jnp.where(sel == float(i), table[i], out)
    return out


def _pair_candidates(v1, v2):
    kk = PEER_TOPK
    first_row = [(v1[0] + v2[j], float(j)) for j in range(kk)]
    vals, ids = [], []
    for i in range(1, kk):
        for j in range(kk):
            if (i + 1) * (j + 1) <= kk:
                vals.append(v1[i] + v2[j])
                ids.append(float(i * kk + j))
    pad = (-len(vals)) % kk
    vals += [jnp.full_like(v1[0], -jnp.inf)] * pad
    ids += [_NO_ROW + float(p) for p in range(pad)]
    return first_row, vals, ids


ROUTE_PITCH = LANES + 8


def _route_kernel(h2_ref, wq_ref, k1_ref, k2_ref, e1_ref, e2_ref, gw_ref, q_ref, s1_ref, s2_ref):
    tm = h2_ref.shape[0]
    groups = tm // LANES
    half = PEER_DK // 2
    q = _dot(h2_ref[...], wq_ref[...])
    for c in range(2 * PEER_HEADS):
        q_ref[c] = q[:, c * half:(c + 1) * half].astype(BF16)
    key_ids = [float(k) for k in range(PEER_NK)]

    def per_key_scores(keys, qh, s_ref):
        for g in range(groups):
            s_ref[g * ROUTE_PITCH:g * ROUTE_PITCH + PEER_NK, :] = _dot_nt(keys, qh[g * LANES:(g + 1) * LANES, :])
        return [s_ref[pl.ds(k, groups, stride=ROUTE_PITCH), :] for k in range(PEER_NK)]

    def body(h, carry):
        v1, i1 = _topk_sorted(per_key_scores(k1_ref[h], q_ref[2 * h], s1_ref), key_ids, PEER_TOPK)
        v2, i2 = _topk_sorted(per_key_scores(k2_ref[h], q_ref[2 * h + 1], s2_ref), key_ids, PEER_TOPK)
        first_row, cand, flat = _pair_candidates(v1, v2)
        sv, sf = _topk_sorted(cand, flat, PEER_TOPK, presorted=[first_row])
        p = [jnp.exp(v - sv[0]) for v in sv]
        inv = 1.0 / _tree(jnp.add, p)
        for k in range(PEER_TOPK):
            sel_i = jnp.floor(sf[k] * (1.0 / PEER_TOPK))
            sel_j = sf[k] - sel_i * float(PEER_TOPK)
            slot = pl.ds(h * PEER_TOPK + k, 1)
            e1_ref[:, slot, :] = _pick_list(i1, sel_i).astype(I32).reshape(groups, 1, LANES)
            e2_ref[:, slot, :] = _pick_list(i2, sel_j).astype(I32).reshape(groups, 1, LANES)
            gw_ref[:, slot, :] = (p[k] * inv).reshape(groups, 1, LANES)
        return carry

    lax.fori_loop(0, PEER_HEADS, body, 0)


def _route(h2, wq, k1, k2, tm):
    n = h2.shape[0]
    groups = tm // LANES
    out = jax.ShapeDtypeStruct((n // LANES, PEER_HK, LANES), I32)
    ospec = pl.BlockSpec((groups, PEER_HK, LANES), lambda i: (i, 0, 0))
    const3 = lambda i: (0, 0, 0)
    return pl.pallas_call(
        _route_kernel,
        grid=(n // tm,),
        in_specs=[pl.BlockSpec((tm, D), lambda i: (i, 0)),
                  pl.BlockSpec((D, PEER_HEADS * PEER_DK), lambda i: (0, 0)),
                  pl.BlockSpec((PEER_HEADS, PEER_NK, PEER_DK // 2), const3),
                  pl.BlockSpec((PEER_HEADS, PEER_NK, PEER_DK // 2), const3)],
        out_specs=[ospec, ospec, ospec],
        out_shape=[out, out, jax.ShapeDtypeStruct((n // LANES, PEER_HK, LANES), F32)],
        scratch_shapes=[pltpu.VMEM((2 * PEER_HEADS, tm, PEER_DK // 2), BF16),
                        pltpu.VMEM((groups * ROUTE_PITCH, LANES), F32),
                        pltpu.VMEM((groups * ROUTE_PITCH, LANES), F32)],
        compiler_params=_cparams(("arbitrary",)),
        name="route",
    )(h2, wq, k1, k2)


EXPERT_AB = 16
W_PAD = 8
SCATTER_UNROLL = 32
HI16 = -65536


def _experts_kernel(h2_ref, e1t_ref, e2t_ref, gwt_ref, u_ref, v_ref, x1_ref, g2_ref, fw_ref, y_ref,
                    e1_ref, e2_ref, gw_ref, gact_ref, coef_ref, w_ref, acc_ref, act_ref):
    tm = h2_ref.shape[0]
    phase = pl.program_id(2)
    j = pl.program_id(3)
    nj = pl.num_programs(3)

    def select_from_act(step):
        e1 = e1_ref[...]
        e2 = e2_ref[...]
        g = jnp.zeros((tm, PEER_HK), F32)
        for p in range(EXPERT_AB // 2):
            both = jnp.take_along_axis(act_ref[:, p * LANES:(p + 1) * LANES], e2, axis=1,
                                       mode="promise_in_bounds")
            first = lax.bitcast_convert_type(both & HI16, F32)
            second = lax.bitcast_convert_type(lax.shift_left(both, 16), F32)
            a = step * EXPERT_AB + 2 * p
            g = jnp.where(e1 == a, first, g)
            g = jnp.where(e1 == a + 1, second, g)
        return g

    def pack_pairs(act):
        out = []
        for p in range(EXPERT_AB // 2):
            first = act[:, (2 * p) * LANES:(2 * p + 1) * LANES].astype(BF16).astype(F32)
            second = act[:, (2 * p + 1) * LANES:(2 * p + 2) * LANES].astype(BF16).astype(F32)
            out.append(lax.bitcast_convert_type(first, I32)
                       | lax.shift_right_logical(lax.bitcast_convert_type(second, I32), 16))
        return jnp.concatenate(out, axis=1)

    @pl.when((phase == 0) & (j == 0))
    def _():
        for r in range(tm // LANES):
            rows = slice(r * LANES, (r + 1) * LANES)
            e1_ref[rows, :] = e1t_ref[r].astype(F32).T.astype(I32)
            e2_ref[rows, :] = e2t_ref[r].astype(F32).T.astype(I32)
            gw_ref[rows, :] = gwt_ref[r].T
        gact_ref[...] = jnp.zeros_like(gact_ref)
        act_ref[...] = jnp.zeros_like(act_ref)

    @pl.when(phase == 0)
    def _():
        gact_ref[...] += select_from_act(j - 1)
        act_ref[...] = pack_pairs(_dot_nt(h2_ref[...], u_ref[...]))

    @pl.when((phase == 1) & (j == 0))
    def _():
        ga = gact_ref[...] + select_from_act(nj - 1)
        gelu = 0.5 * ga * (1.0 + lax.erf(ga * (0.5 ** 0.5)))
        coef_ref[...] = gw_ref[...] * gelu
        acc_ref[...] = jnp.zeros_like(acc_ref)

    half_steps = nj // 2
    a_half = PEER_NK // 2
    pitch = tm + W_PAD

    @pl.when((phase == 1) & (j % half_steps == 0))
    def _():
        a0 = (j // half_steps) * a_half
        rows_a = lax.broadcasted_iota(I32, (a_half, PEER_HK), 0) + a0
        rows_b = lax.broadcasted_iota(I32, (PEER_NK, PEER_HK), 0)

        def scatter(t, carry):
            ct = jnp.where(rows_a == e1_ref[pl.ds(t, 1), :], coef_ref[pl.ds(t, 1), :], 0.0).astype(BF16)
            bt = (rows_b == e2_ref[pl.ds(t, 1), :]).astype(BF16)
            w_ref[pl.ds(t, a_half, stride=pitch), :] = _dot_nt(ct, bt)
            return carry

        lax.fori_loop(0, tm, scatter, 0, unroll=SCATTER_UNROLL)

    @pl.when(phase == 1)
    def _():
        a_loc = (j % half_steps) * EXPERT_AB
        parts = [w_ref[pl.ds(pl.multiple_of((a_loc + a) * pitch, 8), tm), :].astype(BF16) for a in range(EXPERT_AB)]
        acc_ref[...] += _dot(jnp.concatenate(parts, axis=1), v_ref[...])

    @pl.when((phase == 1) & (j == nj - 1))
    def _():
        bb, tt, _ = x1_ref.shape
        x2 = x1_ref[...] + g2_ref[...] * acc_ref[...].reshape(bb, tt, D)
        y_ref[...] = x2 * lax.rsqrt(jnp.mean(x2 * x2, axis=-1, keepdims=True) + EPS) * fw_ref[...]


def _experts(h2, e1t, e2t, gwt, u, v, x1, mod3, final_w, bb, tt):
    b, t, _ = x1.shape
    tm = bb * tt
    nj = PEER_NK // EXPERT_AB
    eb = EXPERT_AB * PEER_NK
    nt = t // tt
    tspec = pl.BlockSpec((tm // LANES, PEER_HK, LANES), lambda i, k, p, j: (i * nt + k, 0, 0))
    rspec = pl.BlockSpec((bb, tt, D), lambda i, k, p, j: (i, k, 0))
    return pl.pallas_call(
        _experts_kernel,
        grid=(b // bb, nt, 2, nj),
        in_specs=[pl.BlockSpec((tm, D), lambda i, k, p, j: (i * nt + k, 0)), tspec, tspec, tspec,
                  pl.BlockSpec((eb, D), lambda i, k, p, j: (jnp.where(p == 0, j, nj - 1), 0)),
                  pl.BlockSpec((eb, D), lambda i, k, p, j: (jnp.where(p == 1, j, 0), 0)),
                  rspec,
                  pl.BlockSpec((bb, 1, D), lambda i, k, p, j: (i, 0, 5)),
                  pl.BlockSpec((1, D), lambda i, k, p, j: (0, 0))],
        out_specs=rspec,
        out_shape=jax.ShapeDtypeStruct((b, t, D), F32),
        scratch_shapes=[pltpu.VMEM((tm, PEER_HK), I32),
                        pltpu.VMEM((tm, PEER_HK), I32),
                        pltpu.VMEM((tm, PEER_HK), F32),
                        pltpu.VMEM((tm, PEER_HK), F32),
                        pltpu.VMEM((tm, PEER_HK), F32),
                        pltpu.VMEM(((PEER_NK // 2) * (tm + W_PAD), PEER_NK), F32),
                        pltpu.VMEM((tm, D), F32),
                        pltpu.VMEM((tm, EXPERT_AB * LANES // 2), I32)],
        compiler_params=_cparams(("arbitrary", "arbitrary", "arbitrary", "arbitrary")),
        name="experts",
    )(h2, e1t, e2t, gwt, u, v, x1, mod3, final_w.reshape(1, D))


def _tile(total, cap):
    t = min(total, cap)
    assert total % t == 0
    return t


def _group(x, mod, s_hg, s_ssm, conv_buf, lb, w, gla_chunk, ssd_chunk):
    b, t, _ = x.shape
    n = b * t
    mod3 = mod.reshape(b, 1, N_MOD * D)
    if t >= 256:
        bb, tt = 1, _tile(t, 1024)
    else:
        bb, tt = b, t
    proj, proj_f = _inproj(x, mod3, w["norm1_w"], w["w_in"], bb, tt)

    o_g, s_hg_new = _gla(proj, proj_f, lb, w["hgrn_norm_w"], s_hg, _tile(t, 1024), gla_chunk)

    conv_buf8 = jnp.pad(conv_buf, ((0, 0), (5, 0), (0, 0)))
    h0t = jnp.swapaxes(s_ssm.reshape(b, SSM_G, SSM_R * SSM_P, SSM_N), 2, 3)
    y_g, h_t = _ssd(proj, proj_f, w["conv_w"], w["conv_b"], conv_buf8, w["dt_bias"], w["a_log"], w["ssm_d"],
                    w["ssm_norm_w"], h0t, ssd_chunk)
    s_ssm_new = jnp.swapaxes(h_t, 2, 3).reshape(b, SSM_HEADS, SSM_P, SSM_N)
    assert t >= 3
    conv_new = proj[:, t - 3:, COL_XS:COL_XS + SSM_XBC].astype(F32)

    if t >= 256:
        bb2, tt2 = 1, _tile(t, 256)
    else:
        bb2, tt2 = b, t
    x1, h2 = _merge(o_g, y_g, proj, x, mod3, w["norm2_w"], w["w_branch_a"], w["w_branch_b"], w["w_out"], bb2, tt2)

    h2f = h2.reshape(n, D)
    e1t, e2t, gwt = _route(h2f, w["peer_wq"], w["peer_keys1"], w["peer_keys2"], _tile(n, 1024))
    bb3, tt3 = (1, _tile(t, 512)) if t >= 256 else (b, t)
    y = _experts(h2f, e1t, e2t, gwt, w["peer_u"], w["peer_v"], x1, mod3, w["final_norm_w"], bb3, tt3)
    return y, s_hg_new, s_ssm_new, conv_new


def kernel(x_prompt, x_sample, c_prompt, c_sample, state_hgrn, state_ssm, state_conv, w_ada, b_ada, norm1_w, w_in,
           hgrn_lower_bounds, hgrn_norm_w, conv_w, conv_b, dt_bias, a_log, ssm_d, ssm_norm_w, w_branch_a,
           w_branch_b, w_out, norm2_w, peer_wq, peer_keys1, peer_keys2, peer_u, peer_v, final_norm_w):
    depth = w_ada.shape[0]
    assert depth == 1
    bp, tp, _ = x_prompt.shape
    bs, ts, _ = x_sample.shape
    lb_all = jnp.cumsum(jax.nn.softmax(hgrn_lower_bounds.astype(F32), axis=0), axis=0)

    l = 0
    wi = w_in[l]
    c_f, c_i, c_dt, c_ga = 1024, 2048, 9216, 9248
    w_pad = jnp.concatenate(
        [wi[:, :c_f], wi[:, c_i:c_dt], wi[:, c_ga:],
         wi[:, c_f:c_i], wi[:, c_dt:c_ga],
         jnp.zeros((D, PROJ_F32_COLS - (c_i - c_f) - SSM_HEADS), wi.dtype)], axis=1).astype(BF16)
    assert w_pad.shape[1] == PROJ_COLS
    w = dict(norm1_w=norm1_w[l], w_in=w_pad, hgrn_norm_w=hgrn_norm_w[l], conv_w=conv_w[l], conv_b=conv_b[l],
             dt_bias=dt_bias[l], a_log=a_log[l], ssm_d=ssm_d[l], ssm_norm_w=ssm_norm_w[l],
             w_branch_a=w_branch_a[l].astype(BF16), w_branch_b=w_branch_b[l].astype(BF16),
             w_out=w_out[l].astype(BF16), norm2_w=norm2_w[l], peer_wq=peer_wq[l].astype(BF16),
             peer_keys1=peer_keys1[l].astype(BF16), peer_keys2=peer_keys2[l].astype(BF16),
             peer_u=peer_u[l].astype(BF16), peer_v=peer_v[l].astype(BF16), final_norm_w=final_norm_w)

    mod = _adaln(jnp.concatenate([c_prompt, c_sample], axis=0), w_ada[l], b_ada[l])

    z_hg = jnp.zeros((bp, HG_HEADS, HG_DK, LANES), F32)
    z_ssm = jnp.zeros((bp, SSM_HEADS, SSM_P, SSM_N), F32)
    z_conv = jnp.zeros((bp, 3, SSM_XBC), F32)
    yp, hg_p, ssm_p, conv_p = _group(x_prompt, mod[:bp], z_hg, z_ssm, z_conv, lb_all[l], w,
                                     min(GLA_CHUNK, tp), min(SSD_CHUNK, tp))
    ys, hg_s, ssm_s, conv_s = _group(x_sample, mod[bp:], state_hgrn[l], state_ssm[l], state_conv[l], lb_all[l], w,
                                     min(GLA_CHUNK, ts), min(SSD_CHUNK, ts))
    return (yp, ys, hg_p[None], ssm_p[None], conv_p[None], hg_s[None], ssm_s[None], conv_s[None])
```

```python
import functools

import jax
import jax.numpy as jnp
from jax import lax
from jax.experimental import pallas as pl
from jax.experimental.pallas import tpu as pltpu

F32 = jnp.float32
BF16 = jnp.bfloat16
I32 = jnp.int32

EPS = 1e-6
LOG2E = 1.4426950408889634
D = 1024
N_MOD = 6
HG_HEADS = 8
HG_DK = 128
SSM_HEADS = 32
SSM_P = 64
SSM_G = 4
SSM_R = SSM_HEADS // SSM_G
SSM_N = 128
SSM_INNER = SSM_HEADS * SSM_P
SSM_GW = SSM_INNER // SSM_G
SSM_XBC = SSM_INNER + 2 * SSM_G * SSM_N
PEER_HEADS = 8
PEER_NK = 128
PEER_TOPK = 16
PEER_HK = PEER_HEADS * PEER_TOPK
PEER_DK = 256

LANES = 128
GLA_CHUNK = 64
SSD_CHUNK = 256
VMEM_LIMIT = 56 * 1024 * 1024

COL_Q = 0
COL_I = 1024
COL_G = 2048
COL_Z = 3072
COL_XS = 5120
COL_B = COL_XS + SSM_INNER
COL_C = COL_B + SSM_G * SSM_N
COL_GA = 8192
COL_GB = 9216
PROJ_BF16_COLS = 10240
COL_F = 0
COL_DT = 1024
PROJ_F32_COLS = 1280
PROJ_TN = 1280
PROJ_COLS = PROJ_BF16_COLS + PROJ_F32_COLS


def _cparams(sem):
    return pltpu.CompilerParams(dimension_semantics=sem, vmem_limit_bytes=VMEM_LIMIT)


def _silu(x):
    return x * jax.nn.sigmoid(x)


def _split3(x):
    hi = x.astype(BF16)
    r = x - hi.astype(F32)
    mid = r.astype(BF16)
    lo = (r - mid.astype(F32)).astype(BF16)
    return hi, mid, lo


def _dot(a, b):
    return jnp.dot(a, b, preferred_element_type=F32)


def _dot_nt(a, b):
    return lax.dot_general(a, b, (((1,), (1,)), ((), ())), preferred_element_type=F32)


def _dot_tn(a, b):
    return lax.dot_general(a, b, (((0,), (0,)), ((), ())), preferred_element_type=F32)


def _dot3_left(m_bf16, x):
    hi, mid, lo = _split3(x)
    return _dot(m_bf16, hi) + _dot(m_bf16, mid) + _dot(m_bf16, lo)


def _dot3_right(x, m_bf16):
    hi, mid, lo = _split3(x)
    return _dot(hi, m_bf16) + _dot(mid, m_bf16) + _dot(lo, m_bf16)


def _chunk_cumsum(x, chunk):
    pos = lax.broadcasted_iota(I32, x.shape, 0) % chunk
    s = 1
    while s < chunk:
        x = x + jnp.where(pos >= s, pltpu.roll(x, shift=s, axis=0), 0.0)
        s *= 2
    return x


def _tril(n):
    r = lax.broadcasted_iota(I32, (n, n), 0)
    c = lax.broadcasted_iota(I32, (n, n), 1)
    return r >= c


def _adaln_kernel(c_ref, w_ref, b_ref, o_ref):
    c = c_ref[...]
    o_ref[...] = _dot(_silu(c).astype(BF16), w_ref[...].astype(BF16)) + b_ref[...]


def _adaln(c, w_ada, b_ada):
    nb = c.shape[0]
    ncol = w_ada.shape[1]
    return pl.pallas_call(
        _adaln_kernel,
        grid=(ncol // D,),
        in_specs=[pl.BlockSpec((nb, D), lambda j: (0, 0)),
                  pl.BlockSpec((D, D), lambda j: (0, j)),
                  pl.BlockSpec((1, D), lambda j: (0, j))],
        out_specs=pl.BlockSpec((nb, D), lambda j: (0, j)),
        out_shape=jax.ShapeDtypeStruct((nb, ncol), F32),
        compiler_params=_cparams(("arbitrary",)),
        name="adaln",
    )(c, w_ada, b_ada.reshape(1, ncol))


def _inproj_kernel(x_ref, sh_ref, sc_ref, nw_ref, w_ref, ob_ref, of_ref, h_ref):
    bb, tt, _ = x_ref.shape
    n = pl.program_id(2)
    n_bf16 = PROJ_BF16_COLS // PROJ_TN

    @pl.when(n == 0)
    def _():
        x = x_ref[...]
        xn = x * lax.rsqrt(jnp.mean(x * x, axis=-1, keepdims=True) + EPS)
        h = xn * nw_ref[...] * (1.0 + sc_ref[...]) + sh_ref[...]
        h_ref[...] = h.reshape(bb * tt, D).astype(BF16)

    res = _dot(h_ref[...], w_ref[...]).reshape(bb, tt, -1)

    @pl.when(n < n_bf16)
    def _():
        ob_ref[...] = res.astype(BF16)

    @pl.when(n == n_bf16)
    def _():
        of_ref[...] = res


def _inproj(x, mod3, norm_w, w_pad, bb, tt):
    b, t, _ = x.shape
    n_bf16 = PROJ_BF16_COLS // PROJ_TN
    assert PROJ_F32_COLS == PROJ_TN
    return pl.pallas_call(
        _inproj_kernel,
        grid=(b // bb, t // tt, PROJ_COLS // PROJ_TN),
        in_specs=[pl.BlockSpec((bb, tt, D), lambda i, j, n: (i, j, 0)),
                  pl.BlockSpec((bb, 1, D), lambda i, j, n: (i, 0, 0)),
                  pl.BlockSpec((bb, 1, D), lambda i, j, n: (i, 0, 1)),
                  pl.BlockSpec((1, D), lambda i, j, n: (0, 0)),
                  pl.BlockSpec((D, PROJ_TN), lambda i, j, n: (0, n))],
        out_specs=[pl.BlockSpec((bb, tt, PROJ_TN), lambda i, j, n: (i, j, jnp.minimum(n, n_bf16 - 1))),
                   pl.BlockSpec((bb, tt, PROJ_TN), lambda i, j, n: (i, j, 0))],
        out_shape=[jax.ShapeDtypeStruct((b, t, PROJ_BF16_COLS), BF16),
                   jax.ShapeDtypeStruct((b, t, PROJ_F32_COLS), F32)],
        scratch_shapes=[pltpu.VMEM((bb * tt, D), BF16)],
        compiler_params=_cparams(("arbitrary", "arbitrary", "arbitrary")),
        name="inproj",
    )(x, mod3, mod3, norm_w.reshape(1, D), w_pad)


def _gla_kernel(q_ref, f_ref, i_ref, g_ref, lb_ref, nw_ref, s0_ref, o_ref, s_out_ref, s_ref, *, chunk):
    tt = q_ref.shape[1]
    nchunk = tt // chunk
    t_idx = pl.program_id(2)

    @pl.when(t_idx == 0)
    def _():
        s_ref[...] = s0_ref[0, 0].T

    lb = lb_ref[...]
    tril = _tril(chunk)
    mid = chunk // 2

    q_all = _silu(q_ref[0].astype(F32))
    ff = f_ref[0]
    logf = jnp.log(lb + (1.0 - lb) * jax.nn.sigmoid(ff))
    k_all = (1.0 - lb) * jax.nn.sigmoid(-ff)
    v_all = i_ref[0]
    g_all = _chunk_cumsum(logf, chunk)

    st = s_ref[...]
    outs = []
    for c in range(nchunk):
        sl = slice(c * chunk, (c + 1) * chunk)
        q, k, v, g = q_all[sl], k_all[sl], v_all[sl], g_all[sl]
        g_mid = g[mid:mid + 1, :]
        g_last = g[chunk - 1:chunk, :]
        qs = (q * jnp.exp(g - g_mid)).astype(BF16)
        ks = (k * jnp.exp(g_mid - g)).astype(BF16)
        att = jnp.where(tril, _dot_nt(qs, ks), 0.0).astype(BF16)
        outs.append(_dot(att, v) + _dot_nt((q * jnp.exp(g)).astype(BF16), st.astype(BF16)))
        kd = (k * jnp.exp(g_last - g)).astype(BF16)
        st = jnp.exp(g_last) * st + _dot_tn(v, kd)
    s_ref[...] = st
    o = jnp.concatenate(outs, axis=0) if nchunk > 1 else outs[0]
    on = o * lax.rsqrt(jnp.mean(o * o, axis=-1, keepdims=True) + EPS) * nw_ref[...]
    o_ref[0] = (on * _silu(g_ref[0].astype(F32))).astype(BF16)

    @pl.when(t_idx == pl.num_programs(2) - 1)
    def _():
        s_out_ref[0, 0] = s_ref[...].T


def _gla(proj, proj_f, lb, hg_norm_w, s0, tt, chunk):
    b, t, _ = proj.shape
    col = lambda base: (lambda i, h, j: (i, j, base // LANES + h))
    return pl.pallas_call(
        functools.partial(_gla_kernel, chunk=chunk),
        grid=(b, HG_HEADS, t // tt),
        in_specs=[pl.BlockSpec((1, tt, LANES), col(COL_Q)),
                  pl.BlockSpec((1, tt, LANES), col(COL_F)),
                  pl.BlockSpec((1, tt, LANES), col(COL_I)),
                  pl.BlockSpec((1, tt, LANES), col(COL_G)),
                  pl.BlockSpec((1, LANES), lambda i, h, j: (0, h)),
                  pl.BlockSpec((1, LANES), lambda i, h, j: (0, h)),
                  pl.BlockSpec((1, 1, HG_DK, LANES), lambda i, h, j: (i, h, 0, 0))],
        out_specs=[pl.BlockSpec((1, tt, LANES), lambda i, h, j: (i, j, h)),
                   pl.BlockSpec((1, 1, HG_DK, LANES), lambda i, h, j: (i, h, 0, 0))],
        out_shape=[jax.ShapeDtypeStruct((b, t, D), BF16),
                   jax.ShapeDtypeStruct((b, HG_HEADS, HG_DK, LANES), F32)],
        scratch_shapes=[pltpu.VMEM((HG_DK, LANES), F32)],
        compiler_params=_cparams(("arbitrary", "arbitrary", "arbitrary")),
        name="gla",
    )(proj, proj_f, proj, proj, lb.reshape(1, D), hg_norm_w.reshape(1, D), s0)


def _ssd_kernel(xs_ref, b_ref, c_ref, z_ref, dt_ref, cwx_ref, cwb_ref, cwc_ref, cbx_ref, cbb_ref, cbc_ref,
                bufx_ref, bufb_ref, bufc_ref, dtb_ref, alog_ref, dpar_ref, sel_ref, exp_ref, shift_ref, nw_ref, h0_ref,
                y_ref, h_out_ref, h_ref, xpx_ref, xpb_ref, xpc_ref):
    L = xs_ref.shape[1]
    t_idx = pl.program_id(2)

    @pl.when(t_idx == 0)
    def _():
        h_ref[...] = h0_ref[0, 0]
        xpx_ref[0:8, :] = bufx_ref[0]
        xpb_ref[0:8, :] = bufb_ref[0]
        xpc_ref[0:8, :] = bufc_ref[0]

    def conv(xp_ref, raw, w_ref, bias_ref):
        cur = raw.astype(F32)
        xp_ref[8:16, :] = cur[0:8, :]
        acc = bias_ref[...] + cur * w_ref[3:4, :]
        for j in range(3):
            delayed = _dot(shift_ref[j], raw)
            head = xp_ref[5 + j:13 + j, :]
            if L > 8:
                delayed = jnp.concatenate([head, delayed[8:, :]], axis=0)
            else:
                delayed = head
            acc = acc + delayed * w_ref[j:j + 1, :]
        xp_ref[0:8, :] = cur[L - 8:L, :]
        return _silu(acc)

    xs = conv(xpx_ref, xs_ref[0], cwx_ref, cbx_ref)
    bm = conv(xpb_ref, b_ref[0], cwb_ref, cbb_ref)
    cm = conv(xpc_ref, c_ref[0], cwc_ref, cbc_ref)
    bm_bf = bm.astype(BF16)
    cm_bf = cm.astype(BF16)

    sel = sel_ref[0]
    expand = exp_ref[...]
    dt_raw = _dot3_right(dt_ref[0], sel)
    dt = jax.nn.softplus(dt_raw + dtb_ref[0])
    a_neg = -jnp.exp(alog_ref[0])
    tril = _tril(L)
    ac = _dot3_left(tril.astype(BF16), dt * a_neg)
    a_last = ac[L - 1:L, :]

    def expand2(v):
        hi = v.astype(BF16)
        lo = (v - hi.astype(F32)).astype(BF16)
        return _dot(hi, expand) + _dot(lo, expand)

    dt_x = expand2(dt)
    eac_x = expand2(jnp.exp(ac))
    wd_x = expand2(jnp.exp(a_last - ac) * dt)
    row8 = lax.broadcasted_iota(I32, (8, LANES), 0)
    tail_x = expand2(jnp.where(row8 == 0, jnp.exp(a_last), jnp.where(row8 == 1, dpar_ref[0], 0.0)))
    decay_x = tail_x[0:1, :]
    d_x = tail_x[1:2, :]

    xdt = (xs * dt_x).astype(BF16)
    cbm = jnp.where(tril, _dot_nt(cm_bf, bm_bf), 0.0)
    ac2 = ac * LOG2E
    ac2_t = ac2.T
    lane = lax.broadcasted_iota(I32, (L, LANES), 1)
    zero = jnp.zeros((), BF16)
    ys = []
    for pair in range(SSM_R // 2):
        xq = xdt[:, pair * LANES:(pair + 1) * LANES]
        acc = None
        for half in range(2):
            r = 2 * pair + half
            seg = jnp.exp2(jnp.minimum(ac2[:, r:r + 1] - ac2_t[r:r + 1, :], 0.0))
            w = (cbm * seg).astype(BF16)
            keep = (lane < SSM_P) if half == 0 else (lane >= SSM_P)
            part = _dot(w, jnp.where(keep, xq, zero))
            acc = part if acc is None else acc + part
        ys.append(acc)
    h_old = h_ref[...]
    y = jnp.concatenate(ys, axis=1) + _dot(cm_bf, h_old.astype(BF16)) * eac_x
    xw = (xs * wd_x).astype(BF16)
    h_ref[...] = decay_x * h_old + _dot_tn(bm_bf, xw)

    y = (y + d_x * xs) * _silu(z_ref[0].astype(F32))
    y = y * lax.rsqrt(jnp.mean(y * y, axis=-1, keepdims=True) + EPS) * nw_ref[...]
    y_ref[0] = y.astype(BF16)

    @pl.when(t_idx == pl.num_programs(2) - 1)
    def _():
        h_out_ref[0, 0] = h_ref[...]


def _ssd(proj, proj_f, conv_w, conv_b, conv_buf8, dt_bias, a_log, ssm_d, ssm_norm_w, h0t, chunk):
    b, t, _ = proj.shape
    L = chunk
    grp = lambda v: jnp.pad(v.reshape(SSM_G, 1, SSM_R), ((0, 0), (0, 0), (0, LANES - SSM_R)))
    hh = jnp.arange(LANES)
    sel = (hh[None, :, None] == (jnp.arange(SSM_G)[:, None, None] * SSM_R + hh[None, None, :])) & (hh[None, None, :] < SSM_R)
    expand = (hh[:, None] == (jnp.arange(SSM_GW)[None, :] // SSM_P))
    tpos = jnp.arange(L)
    shift = jnp.stack([tpos[None, :] == tpos[:, None] - (3 - j) for j in range(3)])
    cb2 = conv_b.reshape(1, SSM_XBC)
    nb_b, nb_c = SSM_INNER // LANES, (SSM_INNER + SSM_G * SSM_N) // LANES
    pcol = lambda base, width: (lambda i, g, j: (i, j, base // width + g))
    return pl.pallas_call(
        _ssd_kernel,
        grid=(b, SSM_G, t // L),
        in_specs=[pl.BlockSpec((1, L, SSM_GW), pcol(COL_XS, SSM_GW)),
                  pl.BlockSpec((1, L, SSM_N), pcol(COL_B, SSM_N)),
                  pl.BlockSpec((1, L, SSM_N), pcol(COL_C, SSM_N)),
                  pl.BlockSpec((1, L, SSM_GW), pcol(COL_Z, SSM_GW)),
                  pl.BlockSpec((1, L, LANES), lambda i, g, j: (i, j, COL_DT // LANES)),
                  pl.BlockSpec((4, SSM_GW), lambda i, g, j: (0, g)),
                  pl.BlockSpec((4, SSM_N), lambda i, g, j: (0, nb_b + g)),
                  pl.BlockSpec((4, SSM_N), lambda i, g, j: (0, nb_c + g)),
                  pl.BlockSpec((1, SSM_GW), lambda i, g, j: (0, g)),
                  pl.BlockSpec((1, SSM_N), lambda i, g, j: (0, nb_b + g)),
                  pl.BlockSpec((1, SSM_N), lambda i, g, j: (0, nb_c + g)),
                  pl.BlockSpec((1, 8, SSM_GW), lambda i, g, j: (i, 0, g)),
                  pl.BlockSpec((1, 8, SSM_N), lambda i, g, j: (i, 0, nb_b + g)),
                  pl.BlockSpec((1, 8, SSM_N), lambda i, g, j: (i, 0, nb_c + g)),
                  pl.BlockSpec((1, 1, LANES), lambda i, g, j: (g, 0, 0)),
                  pl.BlockSpec((1, 1, LANES), lambda i, g, j: (g, 0, 0)),
                  pl.BlockSpec((1, 1, LANES), lambda i, g, j: (g, 0, 0)),
                  pl.BlockSpec((1, LANES, LANES), lambda i, g, j: (g, 0, 0)),
                  pl.BlockSpec((LANES, SSM_GW), lambda i, g, j: (0, 0)),
                  pl.BlockSpec((3, L, L), lambda i, g, j: (0, 0, 0)),
                  pl.BlockSpec((1, SSM_GW), lambda i, g, j: (0, g)),
                  pl.BlockSpec((1, 1, SSM_N, SSM_GW), lambda i, g, j: (i, g, 0, 0))],
        out_specs=[pl.BlockSpec((1, L, SSM_GW), lambda i, g, j: (i, j, g)),
                   pl.BlockSpec((1, 1, SSM_N, SSM_GW), lambda i, g, j: (i, g, 0, 0))],
        out_shape=[jax.ShapeDtypeStruct((b, t, SSM_INNER), BF16),
                   jax.ShapeDtypeStruct((b, SSM_G, SSM_N, SSM_GW), F32)],
        scratch_shapes=[pltpu.VMEM((SSM_N, SSM_GW), F32),
                        pltpu.VMEM((L + 8, SSM_GW), F32),
                        pltpu.VMEM((L + 8, SSM_N), F32),
                        pltpu.VMEM((L + 8, SSM_N), F32)],
        compiler_params=_cparams(("arbitrary", "arbitrary", "arbitrary")),
        name="ssd",
    )(proj, proj, proj, proj, proj_f, conv_w, conv_w, conv_w, cb2, cb2, cb2, conv_buf8, conv_buf8, conv_buf8,
      grp(dt_bias), grp(a_log), grp(ssm_d), sel.astype(BF16), expand.astype(BF16), shift.astype(BF16),
      ssm_norm_w.reshape(1, SSM_INNER), h0t)


def _merge_kernel(o_ref, y_ref, ga_ref, gb_ref, x_ref, g1_ref, sh2_ref, sc2_ref, nw_ref, wa_ref, wb_ref, wo_ref,
                  x1_ref, h2_ref):
    bb, tt, _ = x_ref.shape
    n = bb * tt
    pa = _dot(o_ref[...].reshape(n, D), wa_ref[...])
    pb = _dot(y_ref[...].reshape(n, SSM_INNER), wb_ref[...])
    ga = ga_ref[...].reshape(n, D).astype(F32)
    gb = gb_ref[...].reshape(n, D).astype(F32)
    merged = jax.nn.sigmoid(ga) * pa + jax.nn.sigmoid(gb) * pb
    mix = _dot(merged.astype(BF16), wo_ref[...]).reshape(bb, tt, D)
    x1 = x_ref[...] + g1_ref[...] * mix
    x1_ref[...] = x1
    xn = x1 * lax.rsqrt(jnp.mean(x1 * x1, axis=-1, keepdims=True) + EPS)
    h2_ref[...] = (xn * nw_ref[...] * (1.0 + sc2_ref[...]) + sh2_ref[...]).astype(BF16)


def _merge(o_g, y_g, proj, x, mod3, norm2_w, wa, wb, wo, bb, tt):
    b, t, _ = x.shape
    row = lambda i, j: (i, j, 0)
    modc = lambda c: (lambda i, j: (i, 0, c))
    const = lambda i, j: (0, 0)
    return pl.pallas_call(
        _merge_kernel,
        grid=(b // bb, t // tt),
        in_specs=[pl.BlockSpec((bb, tt, D), row),
                  pl.BlockSpec((bb, tt, SSM_INNER), row),
                  pl.BlockSpec((bb, tt, D), lambda i, j: (i, j, COL_GA // D)),
                  pl.BlockSpec((bb, tt, D), lambda i, j: (i, j, COL_GB // D)),
                  pl.BlockSpec((bb, tt, D), row),
                  pl.BlockSpec((bb, 1, D), modc(2)),
                  pl.BlockSpec((bb, 1, D), modc(3)),
                  pl.BlockSpec((bb, 1, D), modc(4)),
                  pl.BlockSpec((1, D), const),
                  pl.BlockSpec((D, D), const),
                  pl.BlockSpec((SSM_INNER, D), const),
                  pl.BlockSpec((D, D), const)],
        out_specs=[pl.BlockSpec((bb, tt, D), row), pl.BlockSpec((bb, tt, D), row)],
        out_shape=[jax.ShapeDtypeStruct((b, t, D), F32), jax.ShapeDtypeStruct((b, t, D), BF16)],
        compiler_params=_cparams(("arbitrary", "arbitrary")),
        name="merge",
    )(o_g, y_g, proj, proj, x, mod3, mod3, mod3, norm2_w.reshape(1, D), wa, wb, wo)


_NO_ROW = float(1 << 20)


def _tree(op, xs):
    xs = list(xs)
    while len(xs) > 1:
        xs = [op(xs[i], xs[i + 1]) for i in range(0, len(xs) - 1, 2)] + ([xs[-1]] if len(xs) % 2 else [])
    return xs[0]


def _sort_pairs(n):
    pairs = []
    p = 1
    while p < n:
        k = p
        while k >= 1:
            for j in range(k % p, n - k, 2 * k):
                for i in range(min(k, n - j - k)):
                    if (i + j) // (2 * p) == (i + j + k) // (2 * p):
                        pairs.append((i + j, i + j + k))
            k //= 2
        p *= 2
    return pairs


def _first(a, b):
    (va, ia), (vb, ib) = a, b
    eq = va == vb
    return jnp.where(eq, ib, va) > jnp.where(eq, ia, vb)


def _cex(a, b):
    take_a = _first(a, b)
    return ((jnp.maximum(a[0], b[0]), jnp.where(take_a, a[1], b[1])),
            (jnp.minimum(a[0], b[0]), jnp.where(take_a, b[1], a[1])))


def _merge_top(a, b):
    k = len(a)
    c = []
    for i in range(k):
        x, y = a[i], b[k - 1 - i]
        c.append((jnp.maximum(x[0], y[0]), jnp.where(_first(x, y), x[1], y[1])))
    s = k // 2
    while s >= 1:
        for i in range(k):
            if (i & s) == 0:
                c[i], c[i + s] = _cex(c[i], c[i + s])
        s //= 2
    return c


def _topk_sorted(vals, ids, k, presorted=()):
    pairs = _sort_pairs(k)
    items = list(zip(vals, ids))
    blocks = list(presorted)
    for i in range(0, len(items), k):
        blk = items[i:i + k]
        for lo, hi in pairs:
            blk[lo], blk[hi] = _cex(blk[lo], blk[hi])
        blocks.append(blk)
    while len(blocks) > 1:
        nxt = [_merge_top(blocks[i], blocks[i + 1]) for i in range(0, len(blocks) - 1, 2)]
        blocks = nxt + ([blocks[-1]] if len(blocks) % 2 else [])
    return [v for v, _ in blocks[0]], [i for _, i in blocks[0]]


def _pick_list(table, sel):
    out = table[0]
    for i in range(1, len(table)):
        out = jnp.where(sel == float(i), table[i], out)
    return out


def _pair_candidates(v1, v2):
    kk = PEER_TOPK
    first_row = [(v1[0] + v2[j], float(j)) for j in range(kk)]
    vals, ids = [], []
    for i in range(1, kk):
        for j in range(kk):
            if (i + 1) * (j + 1) <= kk:
                vals.append(v1[i] + v2[j])
                ids.append(float(i * kk + j))
    pad = (-len(vals)) % kk
    vals += [jnp.full_like(v1[0], -jnp.inf)] * pad
    ids += [_NO_ROW + float(p) for p in range(pad)]
    return first_row, vals, ids


ROUTE_PITCH = LANES + 8


def _route_kernel(h2_ref, wq_ref, k1_ref, k2_ref, e1_ref, e2_ref, gw_ref, q_ref, s1_ref, s2_ref):
    tm = h2_ref.shape[0]
    groups = tm // LANES
    half = PEER_DK // 2
    q = _dot(h2_ref[...], wq_ref[...])
    for c in range(2 * PEER_HEADS):
        q_ref[c] = q[:, c * half:(c + 1) * half].astype(BF16)
    key_ids = [float(k) for k in range(PEER_NK)]

    def per_key_scores(keys, qh, s_ref):
        for g in range(groups):
            s_ref[g * ROUTE_PITCH:g * ROUTE_PITCH + PEER_NK, :] = _dot_nt(keys, qh[g * LANES:(g + 1) * LANES, :])
        return [s_ref[pl.ds(k, groups, stride=ROUTE_PITCH), :] for k in range(PEER_NK)]

    def body(h, carry):
        v1, i1 = _topk_sorted(per_key_scores(k1_ref[h], q_ref[2 * h], s1_ref), key_ids, PEER_TOPK)
        v2, i2 = _topk_sorted(per_key_scores(k2_ref[h], q_ref[2 * h + 1], s2_ref), key_ids, PEER_TOPK)
        first_row, cand, flat = _pair_candidates(v1, v2)
        sv, sf = _topk_sorted(cand, flat, PEER_TOPK, presorted=[first_row])
        p = [jnp.exp(v - sv[0]) for v in sv]
        inv = 1.0 / _tree(jnp.add, p)
        for k in range(PEER_TOPK):
            sel_i = jnp.floor(sf[k] * (1.0 / PEER_TOPK))
            sel_j = sf[k] - sel_i * float(PEER_TOPK)
            slot = pl.ds(h * PEER_TOPK + k, 1)
            e1_ref[:, slot, :] = _pick_list(i1, sel_i).astype(I32).reshape(groups, 1, LANES)
            e2_ref[:, slot, :] = _pick_list(i2, sel_j).astype(I32).reshape(groups, 1, LANES)
            gw_ref[:, slot, :] = (p[k] * inv).reshape(groups, 1, LANES)
        return carry

    lax.fori_loop(0, PEER_HEADS, body, 0)


def _route(h2, wq, k1, k2, tm):
    n = h2.shape[0]
    groups = tm // LANES
    out = jax.ShapeDtypeStruct((n // LANES, PEER_HK, LANES), I32)
    ospec = pl.BlockSpec((groups, PEER_HK, LANES), lambda i: (i, 0, 0))
    const3 = lambda i: (0, 0, 0)
    return pl.pallas_call(
        _route_kernel,
        grid=(n // tm,),
        in_specs=[pl.BlockSpec((tm, D), lambda i: (i, 0)),
                  pl.BlockSpec((D, PEER_HEADS * PEER_DK), lambda i: (0, 0)),
                  pl.BlockSpec((PEER_HEADS, PEER_NK, PEER_DK // 2), const3),
                  pl.BlockSpec((PEER_HEADS, PEER_NK, PEER_DK // 2), const3)],
        out_specs=[ospec, ospec, ospec],
        out_shape=[out, out, jax.ShapeDtypeStruct((n // LANES, PEER_HK, LANES), F32)],
        scratch_shapes=[pltpu.VMEM((2 * PEER_HEADS, tm, PEER_DK // 2), BF16),
                        pltpu.VMEM((groups * ROUTE_PITCH, LANES), F32),
                        pltpu.VMEM((groups * ROUTE_PITCH, LANES), F32)],
        compiler_params=_cparams(("arbitrary",)),
        name="route",
    )(h2, wq, k1, k2)


EXPERT_AB = 16
W_PAD = 8
SCATTER_UNROLL = 32
HI16 = -65536


def _experts_kernel(h2_ref, e1t_ref, e2t_ref, gwt_ref, u_ref, v_ref, x1_ref, g2_ref, fw_ref, y_ref,
                    e1_ref, e2_ref, gw_ref, gact_ref, coef_ref, w_ref, acc_ref, act_ref):
    tm = h2_ref.shape[0]
    phase = pl.program_id(2)
    j = pl.program_id(3)
    nj = pl.num_programs(3)

    def select_from_act(step):
        e1 = e1_ref[...]
        e2 = e2_ref[...]
        g = jnp.zeros((tm, PEER_HK), F32)
        for p in range(EXPERT_AB // 2):
            both = jnp.take_along_axis(act_ref[:, p * LANES:(p + 1) * LANES], e2, axis=1,
                                       mode="promise_in_bounds")
            first = lax.bitcast_convert_type(both & HI16, F32)
            second = lax.bitcast_convert_type(lax.shift_left(both, 16), F32)
            a = step * EXPERT_AB + 2 * p
            g = jnp.where(e1 == a, first, g)
            g = jnp.where(e1 == a + 1, second, g)
        return g

    def pack_pairs(act):
        out = []
        for p in range(EXPERT_AB // 2):
            first = act[:, (2 * p) * LANES:(2 * p + 1) * LANES].astype(BF16).astype(F32)
            second = act[:, (2 * p + 1) * LANES:(2 * p + 2) * LANES].astype(BF16).astype(F32)
            out.append(lax.bitcast_convert_type(first, I32)
                       | lax.shift_right_logical(lax.bitcast_convert_type(second, I32), 16))
        return jnp.concatenate(out, axis=1)

    @pl.when((phase == 0) & (j == 0))
    def _():
        for r in range(tm // LANES):
            rows = slice(r * LANES, (r + 1) * LANES)
            e1_ref[rows, :] = e1t_ref[r].astype(F32).T.astype(I32)
            e2_ref[rows, :] = e2t_ref[r].astype(F32).T.astype(I32)
            gw_ref[rows, :] = gwt_ref[r].T
        gact_ref[...] = jnp.zeros_like(gact_ref)
        act_ref[...] = jnp.zeros_like(act_ref)

    @pl.when(phase == 0)
    def _():
        gact_ref[...] += select_from_act(j - 1)
        act_ref[...] = pack_pairs(_dot_nt(h2_ref[...], u_ref[...]))

    @pl.when((phase == 1) & (j == 0))
    def _():
        ga = gact_ref[...] + select_from_act(nj - 1)
        gelu = 0.5 * ga * (1.0 + lax.erf(ga * (0.5 ** 0.5)))
        coef_ref[...] = gw_ref[...] * gelu
        acc_ref[...] = jnp.zeros_like(acc_ref)

    half_steps = nj // 2
    a_half = PEER_NK // 2
    pitch = tm + W_PAD

    @pl.when((phase == 1) & (j % half_steps == 0))
    def _():
        a0 = (j // half_steps) * a_half
        rows_a = lax.broadcasted_iota(I32, (a_half, PEER_HK), 0) + a0
        rows_b = lax.broadcasted_iota(I32, (PEER_NK, PEER_HK), 0)

        def scatter(t, carry):
            ct = jnp.where(rows_a == e1_ref[pl.ds(t, 1), :], coef_ref[pl.ds(t, 1), :], 0.0).astype(BF16)
            bt = (rows_b == e2_ref[pl.ds(t, 1), :]).astype(BF16)
            w_ref[pl.ds(t, a_half, stride=pitch), :] = _dot_nt(ct, bt)
            return carry

        lax.fori_loop(0, tm, scatter, 0, unroll=SCATTER_UNROLL)

    @pl.when(phase == 1)
    def _():
        a_loc = (j % half_steps) * EXPERT_AB
        parts = [w_ref[pl.ds(pl.multiple_of((a_loc + a) * pitch, 8), tm), :].astype(BF16) for a in range(EXPERT_AB)]
        acc_ref[...] += _dot(jnp.concatenate(parts, axis=1), v_ref[...])

    @pl.when((phase == 1) & (j == nj - 1))
    def _():
        bb, tt, _ = x1_ref.shape
        x2 = x1_ref[...] + g2_ref[...] * acc_ref[...].reshape(bb, tt, D)
        y_ref[...] = x2 * lax.rsqrt(jnp.mean(x2 * x2, axis=-1, keepdims=True) + EPS) * fw_ref[...]


def _experts(h2, e1t, e2t, gwt, u, v, x1, mod3, final_w, bb, tt):
    b, t, _ = x1.shape
    tm = bb * tt
    nj = PEER_NK // EXPERT_AB
    eb = EXPERT_AB * PEER_NK
    nt = t // tt
    tspec = pl.BlockSpec((tm // LANES, PEER_HK, LANES), lambda i, k, p, j: (i * nt + k, 0, 0))
    rspec = pl.BlockSpec((bb, tt, D), lambda i, k, p, j: (i, k, 0))
    return pl.pallas_call(
        _experts_kernel,
        grid=(b // bb, nt, 2, nj),
        in_specs=[pl.BlockSpec((tm, D), lambda i, k, p, j: (i * nt + k, 0)), tspec, tspec, tspec,
                  pl.BlockSpec((eb, D), lambda i, k, p, j: (jnp.where(p == 0, j, nj - 1), 0)),
                  pl.BlockSpec((eb, D), lambda i, k, p, j: (jnp.where(p == 1, j, 0), 0)),
                  rspec,
                  pl.BlockSpec((bb, 1, D), lambda i, k, p, j: (i, 0, 5)),
                  pl.BlockSpec((1, D), lambda i, k, p, j: (0, 0))],
        out_specs=rspec,
        out_shape=jax.ShapeDtypeStruct((b, t, D), F32),
        scratch_shapes=[pltpu.VMEM((tm, PEER_HK), I32),
                        pltpu.VMEM((tm, PEER_HK), I32),
                        pltpu.VMEM((tm, PEER_HK), F32),
                        pltpu.VMEM((tm, PEER_HK), F32),
                        pltpu.VMEM((tm, PEER_HK), F32),
                        pltpu.VMEM(((PEER_NK // 2) * (tm + W_PAD), PEER_NK), F32),
                        pltpu.VMEM((tm, D), F32),
                        pltpu.VMEM((tm, EXPERT_AB * LANES // 2), I32)],
        compiler_params=_cparams(("arbitrary", "arbitrary", "arbitrary", "arbitrary")),
        name="experts",
    )(h2, e1t, e2t, gwt, u, v, x1, mod3, final_w.reshape(1, D))


TILE_INPROJ = 1024
TILE_GLA = 1024
TILE_MERGE = 512
TILE_ROUTE = 1024
TILE_EXPERTS = 512
LONG_SEQ = 256


def _tile(total, cap):
    t = min(total, cap)
    assert total % t == 0
    return t


def _row_tile(b, t, cap):
    return (1, _tile(t, cap)) if t >= LONG_SEQ else (b, t)


def _group(x, mod, s_hg, s_ssm, conv_buf, lb, w, gla_chunk, ssd_chunk):
    b, t, _ = x.shape
    n = b * t
    mod3 = mod.reshape(b, 1, N_MOD * D)
    bb, tt = _row_tile(b, t, TILE_INPROJ)
    proj, proj_f = _inproj(x, mod3, w["norm1_w"], w["w_in"], bb, tt)

    o_g, s_hg_new = _gla(proj, proj_f, lb, w["hgrn_norm_w"], s_hg, _tile(t, TILE_GLA), gla_chunk)

    conv_buf8 = jnp.pad(conv_buf, ((0, 0), (5, 0), (0, 0)))
    h0t = jnp.swapaxes(s_ssm.reshape(b, SSM_G, SSM_R * SSM_P, SSM_N), 2, 3)
    y_g, h_t = _ssd(proj, proj_f, w["conv_w"], w["conv_b"], conv_buf8, w["dt_bias"], w["a_log"], w["ssm_d"],
                    w["ssm_norm_w"], h0t, ssd_chunk)
    s_ssm_new = jnp.swapaxes(h_t, 2, 3).reshape(b, SSM_HEADS, SSM_P, SSM_N)
    assert t >= 3
    conv_new = proj[:, t - 3:, COL_XS:COL_XS + SSM_XBC].astype(F32)

    bb2, tt2 = _row_tile(b, t, TILE_MERGE)
    x1, h2 = _merge(o_g, y_g, proj, x, mod3, w["norm2_w"], w["w_branch_a"], w["w_branch_b"], w["w_out"], bb2, tt2)

    h2f = h2.reshape(n, D)
    e1t, e2t, gwt = _route(h2f, w["peer_wq"], w["peer_keys1"], w["peer_keys2"], _tile(n, TILE_ROUTE))
    bb3, tt3 = _row_tile(b, t, TILE_EXPERTS)
    y = _experts(h2f, e1t, e2t, gwt, w["peer_u"], w["peer_v"], x1, mod3, w["final_norm_w"], bb3, tt3)
    return y, s_hg_new, s_ssm_new, conv_new


def kernel(x_prompt, x_sample, c_prompt, c_sample, state_hgrn, state_ssm, state_conv, w_ada, b_ada, norm1_w, w_in,
           hgrn_lower_bounds, hgrn_norm_w, conv_w, conv_b, dt_bias, a_log, ssm_d, ssm_norm_w, w_branch_a,
           w_branch_b, w_out, norm2_w, peer_wq, peer_keys1, peer_keys2, peer_u, peer_v, final_norm_w):
    depth = w_ada.shape[0]
    assert depth == 1
    bp, tp, _ = x_prompt.shape
    bs, ts, _ = x_sample.shape
    lb_all = jnp.cumsum(jax.nn.softmax(hgrn_lower_bounds.astype(F32), axis=0), axis=0)

    l = 0
    wi = w_in[l]
    c_f, c_i, c_dt, c_ga = 1024, 2048, 9216, 9248
    w_pad = jnp.concatenate(
        [wi[:, :c_f], wi[:, c_i:c_dt], wi[:, c_ga:],
         wi[:, c_f:c_i], wi[:, c_dt:c_ga],
         jnp.zeros((D, PROJ_F32_COLS - (c_i - c_f) - SSM_HEADS), wi.dtype)], axis=1).astype(BF16)
    assert w_pad.shape[1] == PROJ_COLS
    w = dict(norm1_w=norm1_w[l], w_in=w_pad, hgrn_norm_w=hgrn_norm_w[l], conv_w=conv_w[l], conv_b=conv_b[l],
             dt_bias=dt_bias[l], a_log=a_log[l], ssm_d=ssm_d[l], ssm_norm_w=ssm_norm_w[l],
             w_branch_a=w_branch_a[l].astype(BF16), w_branch_b=w_branch_b[l].astype(BF16),
             w_out=w_out[l].astype(BF16), norm2_w=norm2_w[l], peer_wq=peer_wq[l].astype(BF16),
             peer_keys1=peer_keys1[l].astype(BF16), peer_keys2=peer_keys2[l].astype(BF16),
             peer_u=peer_u[l].astype(BF16), peer_v=peer_v[l].astype(BF16), final_norm_w=final_norm_w)

    mod = _adaln(jnp.concatenate([c_prompt, c_sample], axis=0), w_ada[l], b_ada[l])

    z_hg = jnp.zeros((bp, HG_HEADS, HG_DK, LANES), F32)
    z_ssm = jnp.zeros((bp, SSM_HEADS, SSM_P, SSM_N), F32)
    z_conv = jnp.zeros((bp, 3, SSM_XBC), F32)
    yp, hg_p, ssm_p, conv_p = _group(x_prompt, mod[:bp], z_hg, z_ssm, z_conv, lb_all[l], w,
                                     min(GLA_CHUNK, tp), min(SSD_CHUNK, tp))
    ys, hg_s, ssm_s, conv_s = _group(x_sample, mod[bp:], state_hgrn[l], state_ssm[l], state_conv[l], lb_all[l], w,
                                     min(GLA_CHUNK, ts), min(SSD_CHUNK, ts))
    return (yp, ys, hg_p[None], ssm_p[None], conv_p[None], hg_s[None], ssm_s[None], conv_s[None])
```

```python
import functools

import jax
import jax.numpy as jnp
from jax import lax
from jax.experimental import pallas as pl
from jax.experimental.pallas import tpu as pltpu

F32 = jnp.float32
BF16 = jnp.bfloat16
I32 = jnp.int32

EPS = 1e-6
LOG2E = 1.4426950408889634
D = 1024
N_MOD = 6
HG_HEADS = 8
HG_DK = 128
SSM_HEADS = 32
SSM_P = 64
SSM_G = 4
SSM_R = SSM_HEADS // SSM_G
SSM_N = 128
SSM_INNER = SSM_HEADS * SSM_P
SSM_GW = SSM_INNER // SSM_G
SSM_XBC = SSM_INNER + 2 * SSM_G * SSM_N
PEER_HEADS = 8
PEER_NK = 128
PEER_TOPK = 16
PEER_HK = PEER_HEADS * PEER_TOPK
PEER_DK = 256

LANES = 128
GLA_CHUNK = 64
SSD_CHUNK = 256
VMEM_LIMIT = 56 * 1024 * 1024

COL_Q = 0
COL_I = 1024
COL_G = 2048
COL_Z = 3072
COL_XS = 5120
COL_B = COL_XS + SSM_INNER
COL_C = COL_B + SSM_G * SSM_N
COL_GA = 8192
COL_GB = 9216
PROJ_BF16_COLS = 10240
COL_F = 0
COL_DT = 1024
PROJ_F32_COLS = 1280
PROJ_TN = 1280
PROJ_COLS = PROJ_BF16_COLS + PROJ_F32_COLS


def _cparams(sem):
    return pltpu.CompilerParams(dimension_semantics=sem, vmem_limit_bytes=VMEM_LIMIT)


def _silu(x):
    return x * jax.nn.sigmoid(x)


def _split3(x):
    hi = x.astype(BF16)
    r = x - hi.astype(F32)
    mid = r.astype(BF16)
    lo = (r - mid.astype(F32)).astype(BF16)
    return hi, mid, lo


def _dot(a, b):
    return jnp.dot(a, b, preferred_element_type=F32)


def _dot_nt(a, b):
    return lax.dot_general(a, b, (((1,), (1,)), ((), ())), preferred_element_type=F32)


def _dot_tn(a, b):
    return lax.dot_general(a, b, (((0,), (0,)), ((), ())), preferred_element_type=F32)


def _dot3_left(m_bf16, x):
    hi, mid, lo = _split3(x)
    return _dot(m_bf16, hi) + _dot(m_bf16, mid) + _dot(m_bf16, lo)


def _dot3_right(x, m_bf16):
    hi, mid, lo = _split3(x)
    return _dot(hi, m_bf16) + _dot(mid, m_bf16) + _dot(lo, m_bf16)


def _chunk_cumsum(x, chunk):
    pos = lax.broadcasted_iota(I32, x.shape, 0) % chunk
    s = 1
    while s < chunk:
        x = x + jnp.where(pos >= s, pltpu.roll(x, shift=s, axis=0), 0.0)
        s *= 2
    return x


def _tril(n):
    r = lax.broadcasted_iota(I32, (n, n), 0)
    c = lax.broadcasted_iota(I32, (n, n), 1)
    return r >= c


def _adaln_kernel(c_ref, w_ref, b_ref, o_ref):
    c = c_ref[...]
    o_ref[...] = _dot(_silu(c).astype(BF16), w_ref[...].astype(BF16)) + b_ref[...]


def _adaln(c, w_ada, b_ada):
    nb = c.shape[0]
    ncol = w_ada.shape[1]
    return pl.pallas_call(
        _adaln_kernel,
        grid=(ncol // D,),
        in_specs=[pl.BlockSpec((nb, D), lambda j: (0, 0)),
                  pl.BlockSpec((D, D), lambda j: (0, j)),
                  pl.BlockSpec((1, D), lambda j: (0, j))],
        out_specs=pl.BlockSpec((nb, D), lambda j: (0, j)),
        out_shape=jax.ShapeDtypeStruct((nb, ncol), F32),
        compiler_params=_cparams(("arbitrary",)),
        name="adaln",
    )(c, w_ada, b_ada.reshape(1, ncol))


def _inproj_kernel(x_ref, sh_ref, sc_ref, nw_ref, w_ref, ob_ref, of_ref, h_ref):
    bb, tt, _ = x_ref.shape
    n = pl.program_id(2)
    n_bf16 = PROJ_BF16_COLS // PROJ_TN

    @pl.when(n == 0)
    def _():
        x = x_ref[...]
        xn = x * lax.rsqrt(jnp.mean(x * x, axis=-1, keepdims=True) + EPS)
        h = xn * nw_ref[...] * (1.0 + sc_ref[...]) + sh_ref[...]
        h_ref[...] = h.reshape(bb * tt, D).astype(BF16)

    res = _dot(h_ref[...], w_ref[...]).reshape(bb, tt, -1)

    @pl.when(n < n_bf16)
    def _():
        ob_ref[...] = res.astype(BF16)

    @pl.when(n == n_bf16)
    def _():
        of_ref[...] = res


def _inproj(x, mod3, norm_w, w_pad, bb, tt):
    b, t, _ = x.shape
    n_bf16 = PROJ_BF16_COLS // PROJ_TN
    assert PROJ_F32_COLS == PROJ_TN
    return pl.pallas_call(
        _inproj_kernel,
        grid=(b // bb, t // tt, PROJ_COLS // PROJ_TN),
        in_specs=[pl.BlockSpec((bb, tt, D), lambda i, j, n: (i, j, 0)),
                  pl.BlockSpec((bb, 1, D), lambda i, j, n: (i, 0, 0)),
                  pl.BlockSpec((bb, 1, D), lambda i, j, n: (i, 0, 1)),
                  pl.BlockSpec((1, D), lambda i, j, n: (0, 0)),
                  pl.BlockSpec((D, PROJ_TN), lambda i, j, n: (0, n))],
        out_specs=[pl.BlockSpec((bb, tt, PROJ_TN), lambda i, j, n: (i, j, jnp.minimum(n, n_bf16 - 1))),
                   pl.BlockSpec((bb, tt, PROJ_TN), lambda i, j, n: (i, j, 0))],
        out_shape=[jax.ShapeDtypeStruct((b, t, PROJ_BF16_COLS), BF16),
                   jax.ShapeDtypeStruct((b, t, PROJ_F32_COLS), F32)],
        scratch_shapes=[pltpu.VMEM((bb * tt, D), BF16)],
        compiler_params=_cparams(("arbitrary", "arbitrary", "arbitrary")),
        name="inproj",
    )(x, mod3, mod3, norm_w.reshape(1, D), w_pad)


def _gla_kernel(q_ref, f_ref, i_ref, g_ref, lb_ref, nw_ref, s0_ref, o_ref, s_out_ref, s_ref, *, chunk):
    tt = q_ref.shape[1]
    nchunk = tt // chunk
    t_idx = pl.program_id(2)

    @pl.when(t_idx == 0)
    def _():
        s_ref[...] = s0_ref[0, 0].T

    lb = lb_ref[...]
    tril = _tril(chunk)
    mid = chunk // 2

    q_all = _silu(q_ref[0].astype(F32))
    ff = f_ref[0]
    logf = jnp.log(lb + (1.0 - lb) * jax.nn.sigmoid(ff))
    k_all = (1.0 - lb) * jax.nn.sigmoid(-ff)
    v_all = i_ref[0]
    g_all = _chunk_cumsum(logf, chunk)

    st = s_ref[...]
    outs = []
    for c in range(nchunk):
        sl = slice(c * chunk, (c + 1) * chunk)
        q, k, v, g = q_all[sl], k_all[sl], v_all[sl], g_all[sl]
        g_mid = g[mid:mid + 1, :]
        g_last = g[chunk - 1:chunk, :]
        qs = (q * jnp.exp(g - g_mid)).astype(BF16)
        ks = (k * jnp.exp(g_mid - g)).astype(BF16)
        att = jnp.where(tril, _dot_nt(qs, ks), 0.0).astype(BF16)
        outs.append(_dot(att, v) + _dot_nt((q * jnp.exp(g)).astype(BF16), st.astype(BF16)))
        kd = (k * jnp.exp(g_last - g)).astype(BF16)
        st = jnp.exp(g_last) * st + _dot_tn(v, kd)
    s_ref[...] = st
    o = jnp.concatenate(outs, axis=0) if nchunk > 1 else outs[0]
    on = o * lax.rsqrt(jnp.mean(o * o, axis=-1, keepdims=True) + EPS) * nw_ref[...]
    o_ref[0] = (on * _silu(g_ref[0].astype(F32))).astype(BF16)

    @pl.when(t_idx == pl.num_programs(2) - 1)
    def _():
        s_out_ref[0, 0] = s_ref[...].T


def _gla(proj, proj_f, lb, hg_norm_w, s0, tt, chunk):
    b, t, _ = proj.shape
    col = lambda base: (lambda i, h, j: (i, j, base // LANES + h))
    return pl.pallas_call(
        functools.partial(_gla_kernel, chunk=chunk),
        grid=(b, HG_HEADS, t // tt),
        in_specs=[pl.BlockSpec((1, tt, LANES), col(COL_Q)),
                  pl.BlockSpec((1, tt, LANES), col(COL_F)),
                  pl.BlockSpec((1, tt, LANES), col(COL_I)),
                  pl.BlockSpec((1, tt, LANES), col(COL_G)),
                  pl.BlockSpec((1, LANES), lambda i, h, j: (0, h)),
                  pl.BlockSpec((1, LANES), lambda i, h, j: (0, h)),
                  pl.BlockSpec((1, 1, HG_DK, LANES), lambda i, h, j: (i, h, 0, 0))],
        out_specs=[pl.BlockSpec((1, tt, LANES), lambda i, h, j: (i, j, h)),
                   pl.BlockSpec((1, 1, HG_DK, LANES), lambda i, h, j: (i, h, 0, 0))],
        out_shape=[jax.ShapeDtypeStruct((b, t, D), BF16),
                   jax.ShapeDtypeStruct((b, HG_HEADS, HG_DK, LANES), F32)],
        scratch_shapes=[pltpu.VMEM((HG_DK, LANES), F32)],
        compiler_params=_cparams(("arbitrary", "arbitrary", "arbitrary")),
        name="gla",
    )(proj, proj_f, proj, proj, lb.reshape(1, D), hg_norm_w.reshape(1, D), s0)


def _ssd_kernel(xs_ref, b_ref, c_ref, z_ref, dt_ref, cwx_ref, cwb_ref, cwc_ref, cbx_ref, cbb_ref, cbc_ref,
                bufx_ref, bufb_ref, bufc_ref, dtb_ref, alog_ref, dpar_ref, sel_ref, exp_ref, shift_ref, nw_ref, h0_ref,
                y_ref, h_out_ref, h_ref, xpx_ref, xpb_ref, xpc_ref):
    L = xs_ref.shape[1]
    t_idx = pl.program_id(2)

    @pl.when(t_idx == 0)
    def _():
        h_ref[...] = h0_ref[0, 0]
        xpx_ref[0:8, :] = bufx_ref[0]
        xpb_ref[0:8, :] = bufb_ref[0]
        xpc_ref[0:8, :] = bufc_ref[0]

    def conv(xp_ref, raw, w_ref, bias_ref):
        cur = raw.astype(F32)
        xp_ref[8:16, :] = cur[0:8, :]
        acc = bias_ref[...] + cur * w_ref[3:4, :]
        for j in range(3):
            delayed = _dot(shift_ref[j], raw)
            head = xp_ref[5 + j:13 + j, :]
            if L > 8:
                delayed = jnp.concatenate([head, delayed[8:, :]], axis=0)
            else:
                delayed = head
            acc = acc + delayed * w_ref[j:j + 1, :]
        xp_ref[0:8, :] = cur[L - 8:L, :]
        return _silu(acc)

    xs = conv(xpx_ref, xs_ref[0], cwx_ref, cbx_ref)
    bm = conv(xpb_ref, b_ref[0], cwb_ref, cbb_ref)
    cm = conv(xpc_ref, c_ref[0], cwc_ref, cbc_ref)
    bm_bf = bm.astype(BF16)
    cm_bf = cm.astype(BF16)

    sel = sel_ref[0]
    expand = exp_ref[...]
    dt_raw = _dot3_right(dt_ref[0], sel)
    dt = jax.nn.softplus(dt_raw + dtb_ref[0])
    a_neg = -jnp.exp(alog_ref[0])
    tril = _tril(L)
    ac = _dot3_left(tril.astype(BF16), dt * a_neg)
    a_last = ac[L - 1:L, :]

    def expand2(v):
        hi = v.astype(BF16)
        lo = (v - hi.astype(F32)).astype(BF16)
        return _dot(hi, expand) + _dot(lo, expand)

    dt_x = expand2(dt)
    eac_x = expand2(jnp.exp(ac))
    wd_x = expand2(jnp.exp(a_last - ac) * dt)
    row8 = lax.broadcasted_iota(I32, (8, LANES), 0)
    tail_x = expand2(jnp.where(row8 == 0, jnp.exp(a_last), jnp.where(row8 == 1, dpar_ref[0], 0.0)))
    decay_x = tail_x[0:1, :]
    d_x = tail_x[1:2, :]

    xdt = (xs * dt_x).astype(BF16)
    cbm = jnp.where(tril, _dot_nt(cm_bf, bm_bf), 0.0)
    ac2 = ac * LOG2E
    ac2_t = ac2.T
    lane = lax.broadcasted_iota(I32, (L, LANES), 1)
    zero = jnp.zeros((), BF16)
    ys = []
    for pair in range(SSM_R // 2):
        xq = xdt[:, pair * LANES:(pair + 1) * LANES]
        acc = None
        for half in range(2):
            r = 2 * pair + half
            seg = jnp.exp2(jnp.minimum(ac2[:, r:r + 1] - ac2_t[r:r + 1, :], 0.0))
            w = (cbm * seg).astype(BF16)
            keep = (lane < SSM_P) if half == 0 else (lane >= SSM_P)
            part = _dot(w, jnp.where(keep, xq, zero))
            acc = part if acc is None else acc + part
        ys.append(acc)
    h_old = h_ref[...]
    y = jnp.concatenate(ys, axis=1) + _dot(cm_bf, h_old.astype(BF16)) * eac_x
    xw = (xs * wd_x).astype(BF16)
    h_ref[...] = decay_x * h_old + _dot_tn(bm_bf, xw)

    y = (y + d_x * xs) * _silu(z_ref[0].astype(F32))
    y = y * lax.rsqrt(jnp.mean(y * y, axis=-1, keepdims=True) + EPS) * nw_ref[...]
    y_ref[0] = y.astype(BF16)

    @pl.when(t_idx == pl.num_programs(2) - 1)
    def _():
        h_out_ref[0, 0] = h_ref[...]


def _ssd(proj, proj_f, conv_w, conv_b, conv_buf8, dt_bias, a_log, ssm_d, ssm_norm_w, h0t, chunk):
    b, t, _ = proj.shape
    L = chunk
    grp = lambda v: jnp.pad(v.reshape(SSM_G, 1, SSM_R), ((0, 0), (0, 0), (0, LANES - SSM_R)))
    hh = jnp.arange(LANES)
    sel = (hh[None, :, None] == (jnp.arange(SSM_G)[:, None, None] * SSM_R + hh[None, None, :])) & (hh[None, None, :] < SSM_R)
    expand = (hh[:, None] == (jnp.arange(SSM_GW)[None, :] // SSM_P))
    tpos = jnp.arange(L)
    shift = jnp.stack([tpos[None, :] == tpos[:, None] - (3 - j) for j in range(3)])
    cb2 = conv_b.reshape(1, SSM_XBC)
    nb_b, nb_c = SSM_INNER // LANES, (SSM_INNER + SSM_G * SSM_N) // LANES
    pcol = lambda base, width: (lambda i, g, j: (i, j, base // width + g))
    return pl.pallas_call(
        _ssd_kernel,
        grid=(b, SSM_G, t // L),
        in_specs=[pl.BlockSpec((1, L, SSM_GW), pcol(COL_XS, SSM_GW)),
                  pl.BlockSpec((1, L, SSM_N), pcol(COL_B, SSM_N)),
                  pl.BlockSpec((1, L, SSM_N), pcol(COL_C, SSM_N)),
                  pl.BlockSpec((1, L, SSM_GW), pcol(COL_Z, SSM_GW)),
                  pl.BlockSpec((1, L, LANES), lambda i, g, j: (i, j, COL_DT // LANES)),
                  pl.BlockSpec((4, SSM_GW), lambda i, g, j: (0, g)),
                  pl.BlockSpec((4, SSM_N), lambda i, g, j: (0, nb_b + g)),
                  pl.BlockSpec((4, SSM_N), lambda i, g, j: (0, nb_c + g)),
                  pl.BlockSpec((1, SSM_GW), lambda i, g, j: (0, g)),
                  pl.BlockSpec((1, SSM_N), lambda i, g, j: (0, nb_b + g)),
                  pl.BlockSpec((1, SSM_N), lambda i, g, j: (0, nb_c + g)),
                  pl.BlockSpec((1, 8, SSM_GW), lambda i, g, j: (i, 0, g)),
                  pl.BlockSpec((1, 8, SSM_N), lambda i, g, j: (i, 0, nb_b + g)),
                  pl.BlockSpec((1, 8, SSM_N), lambda i, g, j: (i, 0, nb_c + g)),
                  pl.BlockSpec((1, 1, LANES), lambda i, g, j: (g, 0, 0)),
                  pl.BlockSpec((1, 1, LANES), lambda i, g, j: (g, 0, 0)),
                  pl.BlockSpec((1, 1, LANES), lambda i, g, j: (g, 0, 0)),
                  pl.BlockSpec((1, LANES, LANES), lambda i, g, j: (g, 0, 0)),
                  pl.BlockSpec((LANES, SSM_GW), lambda i, g, j: (0, 0)),
                  pl.BlockSpec((3, L, L), lambda i, g, j: (0, 0, 0)),
                  pl.BlockSpec((1, SSM_GW), lambda i, g, j: (0, g)),
                  pl.BlockSpec((1, 1, SSM_N, SSM_GW), lambda i, g, j: (i, g, 0, 0))],
        out_specs=[pl.BlockSpec((1, L, SSM_GW), lambda i, g, j: (i, j, g)),
                   pl.BlockSpec((1, 1, SSM_N, SSM_GW), lambda i, g, j: (i, g, 0, 0))],
        out_shape=[jax.ShapeDtypeStruct((b, t, SSM_INNER), BF16),
                   jax.ShapeDtypeStruct((b, SSM_G, SSM_N, SSM_GW), F32)],
        scratch_shapes=[pltpu.VMEM((SSM_N, SSM_GW), F32),
                        pltpu.VMEM((L + 8, SSM_GW), F32),
                        pltpu.VMEM((L + 8, SSM_N), F32),
                        pltpu.VMEM((L + 8, SSM_N), F32)],
        compiler_params=_cparams(("arbitrary", "arbitrary", "arbitrary")),
        name="ssd",
    )(proj, proj, proj, proj, proj_f, conv_w, conv_w, conv_w, cb2, cb2, cb2, conv_buf8, conv_buf8, conv_buf8,
      grp(dt_bias), grp(a_log), grp(ssm_d), sel.astype(BF16), expand.astype(BF16), shift.astype(BF16),
      ssm_norm_w.reshape(1, SSM_INNER), h0t)


def _merge_kernel(o_ref, y_ref, ga_ref, gb_ref, x_ref, g1_ref, sh2_ref, sc2_ref, nw_ref, wa_ref, wb_ref, wo_ref,
                  x1_ref, h2_ref):
    bb, tt, _ = x_ref.shape
    n = bb * tt
    pa = _dot(o_ref[...].reshape(n, D), wa_ref[...])
    pb = _dot(y_ref[...].reshape(n, SSM_INNER), wb_ref[...])
    ga = ga_ref[...].reshape(n, D).astype(F32)
    gb = gb_ref[...].reshape(n, D).astype(F32)
    merged = jax.nn.sigmoid(ga) * pa + jax.nn.sigmoid(gb) * pb
    mix = _dot(merged.astype(BF16), wo_ref[...]).reshape(bb, tt, D)
    x1 = x_ref[...] + g1_ref[...] * mix
    x1_ref[...] = x1
    xn = x1 * lax.rsqrt(jnp.mean(x1 * x1, axis=-1, keepdims=True) + EPS)
    h2_ref[...] = (xn * nw_ref[...] * (1.0 + sc2_ref[...]) + sh2_ref[...]).astype(BF16)


def _merge(o_g, y_g, proj, x, mod3, norm2_w, wa, wb, wo, bb, tt):
    b, t, _ = x.shape
    row = lambda i, j: (i, j, 0)
    modc = lambda c: (lambda i, j: (i, 0, c))
    const = lambda i, j: (0, 0)
    return pl.pallas_call(
        _merge_kernel,
        grid=(b // bb, t // tt),
        in_specs=[pl.BlockSpec((bb, tt, D), row),
                  pl.BlockSpec((bb, tt, SSM_INNER), row),
                  pl.BlockSpec((bb, tt, D), lambda i, j: (i, j, COL_GA // D)),
                  pl.BlockSpec((bb, tt, D), lambda i, j: (i, j, COL_GB // D)),
                  pl.BlockSpec((bb, tt, D), row),
                  pl.BlockSpec((bb, 1, D), modc(2)),
                  pl.BlockSpec((bb, 1, D), modc(3)),
                  pl.BlockSpec((bb, 1, D), modc(4)),
                  pl.BlockSpec((1, D), const),
                  pl.BlockSpec((D, D), const),
                  pl.BlockSpec((SSM_INNER, D), const),
                  pl.BlockSpec((D, D), const)],
        out_specs=[pl.BlockSpec((bb, tt, D), row), pl.BlockSpec((bb, tt, D), row)],
        out_shape=[jax.ShapeDtypeStruct((b, t, D), F32), jax.ShapeDtypeStruct((b, t, D), BF16)],
        compiler_params=_cparams(("arbitrary", "arbitrary")),
        name="merge",
    )(o_g, y_g, proj, proj, x, mod3, mod3, mod3, norm2_w.reshape(1, D), wa, wb, wo)


_NO_ROW = float(1 << 20)


def _tree(op, xs):
    xs = list(xs)
    while len(xs) > 1:
        xs = [op(xs[i], xs[i + 1]) for i in range(0, len(xs) - 1, 2)] + ([xs[-1]] if len(xs) % 2 else [])
    return xs[0]


def _sort_pairs(n):
    pairs = []
    p = 1
    while p < n:
        k = p
        while k >= 1:
            for j in range(k % p, n - k, 2 * k):
                for i in range(min(k, n - j - k)):
                    if (i + j) // (2 * p) == (i + j + k) // (2 * p):
                        pairs.append((i + j, i + j + k))
            k //= 2
        p *= 2
    return pairs


def _first(a, b):
    (va, ia), (vb, ib) = a, b
    eq = va == vb
    return jnp.where(eq, ib, va) > jnp.where(eq, ia, vb)


def _cex(a, b):
    take_a = _first(a, b)
    return ((jnp.maximum(a[0], b[0]), jnp.where(take_a, a[1], b[1])),
            (jnp.minimum(a[0], b[0]), jnp.where(take_a, b[1], a[1])))


def _merge_top(a, b):
    k = len(a)
    c = []
    for i in range(k):
        x, y = a[i], b[k - 1 - i]
        c.append((jnp.maximum(x[0], y[0]), jnp.where(_first(x, y), x[1], y[1])))
    s = k // 2
    while s >= 1:
        for i in range(k):
            if (i & s) == 0:
                c[i], c[i + s] = _cex(c[i], c[i + s])
        s //= 2
    return c


def _topk_sorted(vals, ids, k, presorted=()):
    pairs = _sort_pairs(k)
    items = list(zip(vals, ids))
    blocks = list(presorted)
    for i in range(0, len(items), k):
        blk = items[i:i + k]
        for lo, hi in pairs:
            blk[lo], blk[hi] = _cex(blk[lo], blk[hi])
        blocks.append(blk)
    while len(blocks) > 1:
        nxt = [_merge_top(blocks[i], blocks[i + 1]) for i in range(0, len(blocks) - 1, 2)]
        blocks = nxt + ([blocks[-1]] if len(blocks) % 2 else [])
    return [v for v, _ in blocks[0]], [i for _, i in blocks[0]]


def _pick_list(table, sel):
    out = table[0]
    for i in range(1, len(table)):
        out = jnp.where(sel == float(i), table[i], out)
    return out


def _pair_candidates(v1, v2):
    kk = PEER_TOPK
    first_row = [(v1[0] + v2[j], float(j)) for j in range(kk)]
    vals, ids = [], []
    for i in range(1, kk):
        for j in range(kk):
            if (i + 1) * (j + 1) <= kk:
                vals.append(v1[i] + v2[j])
                ids.append(float(i * kk + j))
    pad = (-len(vals)) % kk
    vals += [jnp.full_like(v1[0], -jnp.inf)] * pad
    ids += [_NO_ROW + float(p) for p in range(pad)]
    return first_row, vals, ids


ROUTE_PITCH = LANES + 8


def _route_kernel(h2_ref, wq_ref, k1_ref, k2_ref, e1_ref, e2_ref, gw_ref, q_ref, s1_ref, s2_ref):
    tm = h2_ref.shape[0]
    groups = tm // LANES
    half = PEER_DK // 2
    q = _dot(h2_ref[...], wq_ref[...])
    for c in range(2 * PEER_HEADS):
        q_ref[c] = q[:, c * half:(c + 1) * half].astype(BF16)
    key_ids = [float(k) for k in range(PEER_NK)]

    def per_key_scores(keys, qh, s_ref):
        for g in range(groups):
            s_ref[g * ROUTE_PITCH:g * ROUTE_PITCH + PEER_NK, :] = _dot_nt(keys, qh[g * LANES:(g + 1) * LANES, :])
        return [s_ref[pl.ds(k, groups, stride=ROUTE_PITCH), :] for k in range(PEER_NK)]

    def body(h, carry):
        v1, i1 = _topk_sorted(per_key_scores(k1_ref[h], q_ref[2 * h], s1_ref), key_ids, PEER_TOPK)
        v2, i2 = _topk_sorted(per_key_scores(k2_ref[h], q_ref[2 * h + 1], s2_ref), key_ids, PEER_TOPK)
        first_row, cand, flat = _pair_candidates(v1, v2)
        sv, sf = _topk_sorted(cand, flat, PEER_TOPK, presorted=[first_row])
        p = [jnp.exp(v - sv[0]) for v in sv]
        inv = 1.0 / _tree(jnp.add, p)
        for k in range(PEER_TOPK):
            sel_i = jnp.floor(sf[k] * (1.0 / PEER_TOPK))
            sel_j = sf[k] - sel_i * float(PEER_TOPK)
            slot = pl.ds(h * PEER_TOPK + k, 1)
            e1_ref[:, slot, :] = _pick_list(i1, sel_i).astype(I32).reshape(groups, 1, LANES)
            e2_ref[:, slot, :] = _pick_list(i2, sel_j).astype(I32).reshape(groups, 1, LANES)
            gw_ref[:, slot, :] = (p[k] * inv).reshape(groups, 1, LANES)
        return carry

    lax.fori_loop(0, PEER_HEADS, body, 0)


def _route(h2, wq, k1, k2, tm):
    n = h2.shape[0]
    groups = tm // LANES
    out = jax.ShapeDtypeStruct((n // LANES, PEER_HK, LANES), I32)
    ospec = pl.BlockSpec((groups, PEER_HK, LANES), lambda i: (i, 0, 0))
    const3 = lambda i: (0, 0, 0)
    return pl.pallas_call(
        _route_kernel,
        grid=(n // tm,),
        in_specs=[pl.BlockSpec((tm, D), lambda i: (i, 0)),
                  pl.BlockSpec((D, PEER_HEADS * PEER_DK), lambda i: (0, 0)),
                  pl.BlockSpec((PEER_HEADS, PEER_NK, PEER_DK // 2), const3),
                  pl.BlockSpec((PEER_HEADS, PEER_NK, PEER_DK // 2), const3)],
        out_specs=[ospec, ospec, ospec],
        out_shape=[out, out, jax.ShapeDtypeStruct((n // LANES, PEER_HK, LANES), F32)],
        scratch_shapes=[pltpu.VMEM((2 * PEER_HEADS, tm, PEER_DK // 2), BF16),
                        pltpu.VMEM((groups * ROUTE_PITCH, LANES), F32),
                        pltpu.VMEM((groups * ROUTE_PITCH, LANES), F32)],
        compiler_params=_cparams(("arbitrary",)),
        name="route",
    )(h2, wq, k1, k2)


EXPERT_AB = 16
W_PAD = 8
SCATTER_UNROLL = 32
HI16 = -65536


def _experts_kernel(h2_ref, e1t_ref, e2t_ref, gwt_ref, u_ref, v_ref, x1_ref, g2_ref, fw_ref, y_ref,
                    e1_ref, e2_ref, gw_ref, gact_ref, coef_ref, w_ref, acc_ref, act_ref):
    tm = h2_ref.shape[0]
    phase = pl.program_id(2)
    j = pl.program_id(3)
    nj = pl.num_programs(3)

    def select_from_act(step):
        e1 = e1_ref[...]
        e2 = e2_ref[...]
        g = jnp.zeros((tm, PEER_HK), F32)
        for p in range(EXPERT_AB // 2):
            both = jnp.take_along_axis(act_ref[:, p * LANES:(p + 1) * LANES], e2, axis=1,
                                       mode="promise_in_bounds")
            first = lax.bitcast_convert_type(both & HI16, F32)
            second = lax.bitcast_convert_type(lax.shift_left(both, 16), F32)
            a = step * EXPERT_AB + 2 * p
            g = jnp.where(e1 == a, first, g)
            g = jnp.where(e1 == a + 1, second, g)
        return g

    def pack_pairs(act):
        out = []
        for p in range(EXPERT_AB // 2):
            first = act[:, (2 * p) * LANES:(2 * p + 1) * LANES].astype(BF16).astype(F32)
            second = act[:, (2 * p + 1) * LANES:(2 * p + 2) * LANES].astype(BF16).astype(F32)
            out.append(lax.bitcast_convert_type(first, I32)
                       | lax.shift_right_logical(lax.bitcast_convert_type(second, I32), 16))
        return jnp.concatenate(out, axis=1)

    @pl.when((phase == 0) & (j == 0))
    def _():
        for r in range(tm // LANES):
            rows = slice(r * LANES, (r + 1) * LANES)
            e1_ref[rows, :] = e1t_ref[r].astype(F32).T.astype(I32)
            e2_ref[rows, :] = e2t_ref[r].astype(F32).T.astype(I32)
            gw_ref[rows, :] = gwt_ref[r].T
        gact_ref[...] = jnp.zeros_like(gact_ref)
        act_ref[...] = jnp.zeros_like(act_ref)

    @pl.when(phase == 0)
    def _():
        gact_ref[...] += select_from_act(j - 1)
        act_ref[...] = pack_pairs(_dot_nt(h2_ref[...], u_ref[...]))

    @pl.when((phase == 1) & (j == 0))
    def _():
        ga = gact_ref[...] + select_from_act(nj - 1)
        gelu = 0.5 * ga * (1.0 + lax.erf(ga * (0.5 ** 0.5)))
        coef_ref[...] = gw_ref[...] * gelu
        acc_ref[...] = jnp.zeros_like(acc_ref)

    hp = tm // 2
    pitch = hp + W_PAD

    @pl.when((phase == 1) & (j == 0))
    def _():
        rows = lax.broadcasted_iota(I32, (PEER_NK, PEER_HK), 0)

        def weights_bits(t):
            ct = jnp.where(rows == e1_ref[pl.ds(t, 1), :], coef_ref[pl.ds(t, 1), :], 0.0).astype(BF16)
            bt = (rows == e2_ref[pl.ds(t, 1), :]).astype(BF16)
            return lax.bitcast_convert_type(_dot_nt(ct, bt).astype(BF16).astype(F32), I32)

        def scatter(t, carry):
            words = weights_bits(t) | lax.shift_right_logical(weights_bits(t + hp), 16)
            w_ref[pl.ds(t, PEER_NK, stride=pitch), :] = words
            return carry

        lax.fori_loop(0, hp, scatter, 0, unroll=SCATTER_UNROLL // 2)

    @pl.when(phase == 1)
    def _():
        parts = []
        for a in range(EXPERT_AB):
            words = w_ref[pl.ds(pl.multiple_of((j * EXPERT_AB + a) * pitch, 8), hp), :]
            first = lax.bitcast_convert_type(words & HI16, F32).astype(BF16)
            second = lax.bitcast_convert_type(lax.shift_left(words, 16), F32).astype(BF16)
            parts.append(jnp.concatenate([first, second], axis=0))
        acc_ref[...] += _dot(jnp.concatenate(parts, axis=1), v_ref[...])

    @pl.when((phase == 1) & (j == nj - 1))
    def _():
        bb, tt, _ = x1_ref.shape
        x2 = x1_ref[...] + g2_ref[...] * acc_ref[...].reshape(bb, tt, D)
        y_ref[...] = x2 * lax.rsqrt(jnp.mean(x2 * x2, axis=-1, keepdims=True) + EPS) * fw_ref[...]


def _experts(h2, e1t, e2t, gwt, u, v, x1, mod3, final_w, bb, tt):
    b, t, _ = x1.shape
    tm = bb * tt
    nj = PEER_NK // EXPERT_AB
    eb = EXPERT_AB * PEER_NK
    nt = t // tt
    tspec = pl.BlockSpec((tm // LANES, PEER_HK, LANES), lambda i, k, p, j: (i * nt + k, 0, 0))
    rspec = pl.BlockSpec((bb, tt, D), lambda i, k, p, j: (i, k, 0))
    return pl.pallas_call(
        _experts_kernel,
        grid=(b // bb, nt, 2, nj),
        in_specs=[pl.BlockSpec((tm, D), lambda i, k, p, j: (i * nt + k, 0)), tspec, tspec, tspec,
                  pl.BlockSpec((eb, D), lambda i, k, p, j: (jnp.where(p == 0, j, nj - 1), 0)),
                  pl.BlockSpec((eb, D), lambda i, k, p, j: (jnp.where(p == 1, j, 0), 0)),
                  rspec,
                  pl.BlockSpec((bb, 1, D), lambda i, k, p, j: (i, 0, 5)),
                  pl.BlockSpec((1, D), lambda i, k, p, j: (0, 0))],
        out_specs=rspec,
        out_shape=jax.ShapeDtypeStruct((b, t, D), F32),
        scratch_shapes=[pltpu.VMEM((tm, PEER_HK), I32),
                        pltpu.VMEM((tm, PEER_HK), I32),
                        pltpu.VMEM((tm, PEER_HK), F32),
                        pltpu.VMEM((tm, PEER_HK), F32),
                        pltpu.VMEM((tm, PEER_HK), F32),
                        pltpu.VMEM((PEER_NK * (tm // 2 + W_PAD), PEER_NK), I32),
                        pltpu.VMEM((tm, D), F32),
                        pltpu.VMEM((tm, EXPERT_AB * LANES // 2), I32)],
        compiler_params=_cparams(("arbitrary", "arbitrary", "arbitrary", "arbitrary")),
        name="experts",
    )(h2, e1t, e2t, gwt, u, v, x1, mod3, final_w.reshape(1, D))


TILE_INPROJ = 1024
TILE_GLA = 1024
TILE_MERGE = 512
TILE_ROUTE = 1024
TILE_EXPERTS = 512
LONG_SEQ = 256


def _tile(total, cap):
    t = min(total, cap)
    assert total % t == 0
    return t


def _row_tile(b, t, cap):
    return (1, _tile(t, cap)) if t >= LONG_SEQ else (b, t)


def _group(x, mod, s_hg, s_ssm, conv_buf, lb, w, gla_chunk, ssd_chunk):
    b, t, _ = x.shape
    n = b * t
    mod3 = mod.reshape(b, 1, N_MOD * D)
    bb, tt = _row_tile(b, t, TILE_INPROJ)
    proj, proj_f = _inproj(x, mod3, w["norm1_w"], w["w_in"], bb, tt)

    o_g, s_hg_new = _gla(proj, proj_f, lb, w["hgrn_norm_w"], s_hg, _tile(t, TILE_GLA), gla_chunk)

    conv_buf8 = jnp.pad(conv_buf, ((0, 0), (5, 0), (0, 0)))
    h0t = jnp.swapaxes(s_ssm.reshape(b, SSM_G, SSM_R * SSM_P, SSM_N), 2, 3)
    y_g, h_t = _ssd(proj, proj_f, w["conv_w"], w["conv_b"], conv_buf8, w["dt_bias"], w["a_log"], w["ssm_d"],
                    w["ssm_norm_w"], h0t, ssd_chunk)
    s_ssm_new = jnp.swapaxes(h_t, 2, 3).reshape(b, SSM_HEADS, SSM_P, SSM_N)
    assert t >= 3
    conv_new = proj[:, t - 3:, COL_XS:COL_XS + SSM_XBC].astype(F32)

    bb2, tt2 = _row_tile(b, t, TILE_MERGE)
    x1, h2 = _merge(o_g, y_g, proj, x, mod3, w["norm2_w"], w["w_branch_a"], w["w_branch_b"], w["w_out"], bb2, tt2)

    h2f = h2.reshape(n, D)
    e1t, e2t, gwt = _route(h2f, w["peer_wq"], w["peer_keys1"], w["peer_keys2"], _tile(n, TILE_ROUTE))
    bb3, tt3 = _row_tile(b, t, TILE_EXPERTS)
    y = _experts(h2f, e1t, e2t, gwt, w["peer_u"], w["peer_v"], x1, mod3, w["final_norm_w"], bb3, tt3)
    return y, s_hg_new, s_ssm_new, conv_new


def kernel(x_prompt, x_sample, c_prompt, c_sample, state_hgrn, state_ssm, state_conv, w_ada, b_ada, norm1_w, w_in,
           hgrn_lower_bounds, hgrn_norm_w, conv_w, conv_b, dt_bias, a_log, ssm_d, ssm_norm_w, w_branch_a,
           w_branch_b, w_out, norm2_w, peer_wq, peer_keys1, peer_keys2, peer_u, peer_v, final_norm_w):
    depth = w_ada.shape[0]
    assert depth == 1
    bp, tp, _ = x_prompt.shape
    bs, ts, _ = x_sample.shape
    lb_all = jnp.cumsum(jax.nn.softmax(hgrn_lower_bounds.astype(F32), axis=0), axis=0)

    l = 0
    wi = w_in[l]
    c_f, c_i, c_dt, c_ga = 1024, 2048, 9216, 9248
    w_pad = jnp.concatenate(
        [wi[:, :c_f], wi[:, c_i:c_dt], wi[:, c_ga:],
         wi[:, c_f:c_i], wi[:, c_dt:c_ga],
         jnp.zeros((D, PROJ_F32_COLS - (c_i - c_f) - SSM_HEADS), wi.dtype)], axis=1).astype(BF16)
    assert w_pad.shape[1] == PROJ_COLS
    w = dict(norm1_w=norm1_w[l], w_in=w_pad, hgrn_norm_w=hgrn_norm_w[l], conv_w=conv_w[l], conv_b=conv_b[l],
             dt_bias=dt_bias[l], a_log=a_log[l], ssm_d=ssm_d[l], ssm_norm_w=ssm_norm_w[l],
             w_branch_a=w_branch_a[l].astype(BF16), w_branch_b=w_branch_b[l].astype(BF16),
             w_out=w_out[l].astype(BF16), norm2_w=norm2_w[l], peer_wq=peer_wq[l].astype(BF16),
             peer_keys1=peer_keys1[l].astype(BF16), peer_keys2=peer_keys2[l].astype(BF16),
             peer_u=peer_u[l].astype(BF16), peer_v=peer_v[l].astype(BF16), final_norm_w=final_norm_w)

    mod = _adaln(jnp.concatenate([c_prompt, c_sample], axis=0), w_ada[l], b_ada[l])

    z_hg = jnp.zeros((bp, HG_HEADS, HG_DK, LANES), F32)
    z_ssm = jnp.zeros((bp, SSM_HEADS, SSM_P, SSM_N), F32)
    z_conv = jnp.zeros((bp, 3, SSM_XBC), F32)
    yp, hg_p, ssm_p, conv_p = _group(x_prompt, mod[:bp], z_hg, z_ssm, z_conv, lb_all[l], w,
                                     min(GLA_CHUNK, tp), min(SSD_CHUNK, tp))
    ys, hg_s, ssm_s, conv_s = _group(x_sample, mod[bp:], state_hgrn[l], state_ssm[l], state_conv[l], lb_all[l], w,
                                     min(GLA_CHUNK, ts), min(SSD_CHUNK, ts))
    return (yp, ys, hg_p[None], ssm_p[None], conv_p[None], hg_s[None], ssm_s[None], conv_s[None])
```

```python
import functools

import jax
import jax.numpy as jnp
from jax import lax
from jax.experimental import pallas as pl
from jax.experimental.pallas import tpu as pltpu

F32 = jnp.float32
BF16 = jnp.bfloat16
I32 = jnp.int32

EPS = 1e-6
LOG2E = 1.4426950408889634
D = 1024
N_MOD = 6
HG_HEADS = 8
HG_DK = 128
SSM_HEADS = 32
SSM_P = 64
SSM_G = 4
SSM_R = SSM_HEADS // SSM_G
SSM_N = 128
SSM_INNER = SSM_HEADS * SSM_P
SSM_GW = SSM_INNER // SSM_G
SSM_XBC = SSM_INNER + 2 * SSM_G * SSM_N
PEER_HEADS = 8
PEER_NK = 128
PEER_TOPK = 16
PEER_HK = PEER_HEADS * PEER_TOPK
PEER_DK = 256

LANES = 128
GLA_CHUNK = 64
SSD_CHUNK = 256
VMEM_LIMIT = 56 * 1024 * 1024

COL_Q = 0
COL_I = 1024
COL_G = 2048
COL_Z = 3072
COL_XS = 5120
COL_B = COL_XS + SSM_INNER
COL_C = COL_B + SSM_G * SSM_N
COL_GA = 8192
COL_GB = 9216
PROJ_BF16_COLS = 10240
COL_F = 0
COL_DT = 1024
PROJ_F32_COLS = 1280
PROJ_TN = 1280
PROJ_COLS = PROJ_BF16_COLS + PROJ_F32_COLS


def _cparams(sem):
    return pltpu.CompilerParams(dimension_semantics=sem, vmem_limit_bytes=VMEM_LIMIT)


def _silu(x):
    return x * jax.nn.sigmoid(x)


def _split3(x):
    hi = x.astype(BF16)
    r = x - hi.astype(F32)
    mid = r.astype(BF16)
    lo = (r - mid.astype(F32)).astype(BF16)
    return hi, mid, lo


def _dot(a, b):
    return jnp.dot(a, b, preferred_element_type=F32)


def _dot_nt(a, b):
    return lax.dot_general(a, b, (((1,), (1,)), ((), ())), preferred_element_type=F32)


def _dot_tn(a, b):
    return lax.dot_general(a, b, (((0,), (0,)), ((), ())), preferred_element_type=F32)


def _dot3_left(m_bf16, x):
    hi, mid, lo = _split3(x)
    return _dot(m_bf16, hi) + _dot(m_bf16, mid) + _dot(m_bf16, lo)


def _dot3_right(x, m_bf16):
    hi, mid, lo = _split3(x)
    return _dot(hi, m_bf16) + _dot(mid, m_bf16) + _dot(lo, m_bf16)


def _chunk_cumsum(x, chunk):
    pos = lax.broadcasted_iota(I32, x.shape, 0) % chunk
    s = 1
    while s < chunk:
        x = x + jnp.where(pos >= s, pltpu.roll(x, shift=s, axis=0), 0.0)
        s *= 2
    return x


def _tril(n):
    r = lax.broadcasted_iota(I32, (n, n), 0)
    c = lax.broadcasted_iota(I32, (n, n), 1)
    return r >= c


def _adaln_kernel(c_ref, w_ref, b_ref, o_ref):
    c = c_ref[...]
    o_ref[...] = _dot(_silu(c).astype(BF16), w_ref[...].astype(BF16)) + b_ref[...]


def _adaln(c, w_ada, b_ada):
    nb = c.shape[0]
    ncol = w_ada.shape[1]
    return pl.pallas_call(
        _adaln_kernel,
        grid=(ncol // D,),
        in_specs=[pl.BlockSpec((nb, D), lambda j: (0, 0)),
                  pl.BlockSpec((D, D), lambda j: (0, j)),
                  pl.BlockSpec((1, D), lambda j: (0, j))],
        out_specs=pl.BlockSpec((nb, D), lambda j: (0, j)),
        out_shape=jax.ShapeDtypeStruct((nb, ncol), F32),
        compiler_params=_cparams(("arbitrary",)),
        name="adaln",
    )(c, w_ada, b_ada.reshape(1, ncol))


def _inproj_kernel(x_ref, sh_ref, sc_ref, nw_ref, w_ref, ob_ref, of_ref, h_ref):
    bb, tt, _ = x_ref.shape
    n = pl.program_id(2)
    n_bf16 = PROJ_BF16_COLS // PROJ_TN

    @pl.when(n == 0)
    def _():
        x = x_ref[...]
        xn = x * lax.rsqrt(jnp.mean(x * x, axis=-1, keepdims=True) + EPS)
        h = xn * nw_ref[...] * (1.0 + sc_ref[...]) + sh_ref[...]
        h_ref[...] = h.reshape(bb * tt, D).astype(BF16)

    res = _dot(h_ref[...], w_ref[...]).reshape(bb, tt, -1)

    @pl.when(n < n_bf16)
    def _():
        ob_ref[...] = res.astype(BF16)

    @pl.when(n == n_bf16)
    def _():
        of_ref[...] = res


def _inproj(x, mod3, norm_w, w_pad, bb, tt):
    b, t, _ = x.shape
    n_bf16 = PROJ_BF16_COLS // PROJ_TN
    assert PROJ_F32_COLS == PROJ_TN
    return pl.pallas_call(
        _inproj_kernel,
        grid=(b // bb, t // tt, PROJ_COLS // PROJ_TN),
        in_specs=[pl.BlockSpec((bb, tt, D), lambda i, j, n: (i, j, 0)),
                  pl.BlockSpec((bb, 1, D), lambda i, j, n: (i, 0, 0)),
                  pl.BlockSpec((bb, 1, D), lambda i, j, n: (i, 0, 1)),
                  pl.BlockSpec((1, D), lambda i, j, n: (0, 0)),
                  pl.BlockSpec((D, PROJ_TN), lambda i, j, n: (0, n))],
        out_specs=[pl.BlockSpec((bb, tt, PROJ_TN), lambda i, j, n: (i, j, jnp.minimum(n, n_bf16 - 1))),
                   pl.BlockSpec((bb, tt, PROJ_TN), lambda i, j, n: (i, j, 0))],
        out_shape=[jax.ShapeDtypeStruct((b, t, PROJ_BF16_COLS), BF16),
                   jax.ShapeDtypeStruct((b, t, PROJ_F32_COLS), F32)],
        scratch_shapes=[pltpu.VMEM((bb * tt, D), BF16)],
        compiler_params=_cparams(("arbitrary", "arbitrary", "arbitrary")),
        name="inproj",
    )(x, mod3, mod3, norm_w.reshape(1, D), w_pad)


def _gla_kernel(q_ref, f_ref, i_ref, g_ref, lb_ref, nw_ref, s0_ref, o_ref, s_out_ref, s_ref, *, chunk):
    tt = q_ref.shape[1]
    nchunk = tt // chunk
    t_idx = pl.program_id(2)

    @pl.when(t_idx == 0)
    def _():
        s_ref[...] = s0_ref[0, 0].T

    lb = lb_ref[...]
    tril = _tril(chunk)
    mid = chunk // 2

    q_all = _silu(q_ref[0].astype(F32))
    ff = f_ref[0]
    logf = jnp.log(lb + (1.0 - lb) * jax.nn.sigmoid(ff))
    k_all = (1.0 - lb) * jax.nn.sigmoid(-ff)
    v_all = i_ref[0]
    g_all = _chunk_cumsum(logf, chunk)

    st = s_ref[...]
    outs = []
    for c in range(nchunk):
        sl = slice(c * chunk, (c + 1) * chunk)
        q, k, v, g = q_all[sl], k_all[sl], v_all[sl], g_all[sl]
        g_mid = g[mid:mid + 1, :]
        g_last = g[chunk - 1:chunk, :]
        qs = (q * jnp.exp(g - g_mid)).astype(BF16)
        ks = (k * jnp.exp(g_mid - g)).astype(BF16)
        att = jnp.where(tril, _dot_nt(qs, ks), 0.0).astype(BF16)
        outs.append(_dot(att, v) + _dot_nt((q * jnp.exp(g)).astype(BF16), st.astype(BF16)))
        kd = (k * jnp.exp(g_last - g)).astype(BF16)
        st = jnp.exp(g_last) * st + _dot_tn(v, kd)
    s_ref[...] = st
    o = jnp.concatenate(outs, axis=0) if nchunk > 1 else outs[0]
    on = o * lax.rsqrt(jnp.mean(o * o, axis=-1, keepdims=True) + EPS) * nw_ref[...]
    o_ref[0] = (on * _silu(g_ref[0].astype(F32))).astype(BF16)

    @pl.when(t_idx == pl.num_programs(2) - 1)
    def _():
        s_out_ref[0, 0] = s_ref[...].T


def _gla(proj, proj_f, lb, hg_norm_w, s0, tt, chunk):
    b, t, _ = proj.shape
    col = lambda base: (lambda i, h, j: (i, j, base // LANES + h))
    return pl.pallas_call(
        functools.partial(_gla_kernel, chunk=chunk),
        grid=(b, HG_HEADS, t // tt),
        in_specs=[pl.BlockSpec((1, tt, LANES), col(COL_Q)),
                  pl.BlockSpec((1, tt, LANES), col(COL_F)),
                  pl.BlockSpec((1, tt, LANES), col(COL_I)),
                  pl.BlockSpec((1, tt, LANES), col(COL_G)),
                  pl.BlockSpec((1, LANES), lambda i, h, j: (0, h)),
                  pl.BlockSpec((1, LANES), lambda i, h, j: (0, h)),
                  pl.BlockSpec((1, 1, HG_DK, LANES), lambda i, h, j: (i, h, 0, 0))],
        out_specs=[pl.BlockSpec((1, tt, LANES), lambda i, h, j: (i, j, h)),
                   pl.BlockSpec((1, 1, HG_DK, LANES), lambda i, h, j: (i, h, 0, 0))],
        out_shape=[jax.ShapeDtypeStruct((b, t, D), BF16),
                   jax.ShapeDtypeStruct((b, HG_HEADS, HG_DK, LANES), F32)],
        scratch_shapes=[pltpu.VMEM((HG_DK, LANES), F32)],
        compiler_params=_cparams(("arbitrary", "arbitrary", "arbitrary")),
        name="gla",
    )(proj, proj_f, proj, proj, lb.reshape(1, D), hg_norm_w.reshape(1, D), s0)


def _ssd_kernel(xs_ref, b_ref, c_ref, z_ref, dt_ref, cwx_ref, cwb_ref, cwc_ref, cbx_ref, cbb_ref, cbc_ref,
                bufx_ref, bufb_ref, bufc_ref, dtb_ref, alog_ref, dpar_ref, sel_ref, exp_ref, shift_ref, nw_ref, h0_ref,
                y_ref, h_out_ref, h_ref, xpx_ref, xpb_ref, xpc_ref):
    L = xs_ref.shape[1]
    t_idx = pl.program_id(2)

    @pl.when(t_idx == 0)
    def _():
        h_ref[...] = h0_ref[0, 0]
        xpx_ref[0:8, :] = bufx_ref[0]
        xpb_ref[0:8, :] = bufb_ref[0]
        xpc_ref[0:8, :] = bufc_ref[0]

    def conv(xp_ref, raw, w_ref, bias_ref):
        cur = raw.astype(F32)
        xp_ref[8:16, :] = cur[0:8, :]
        acc = bias_ref[...] + cur * w_ref[3:4, :]
        for j in range(3):
            delayed = _dot(shift_ref[j], raw)
            head = xp_ref[5 + j:13 + j, :]
            if L > 8:
                delayed = jnp.concatenate([head, delayed[8:, :]], axis=0)
            else:
                delayed = head
            acc = acc + delayed * w_ref[j:j + 1, :]
        xp_ref[0:8, :] = cur[L - 8:L, :]
        return _silu(acc)

    xs = conv(xpx_ref, xs_ref[0], cwx_ref, cbx_ref)
    bm = conv(xpb_ref, b_ref[0], cwb_ref, cbb_ref)
    cm = conv(xpc_ref, c_ref[0], cwc_ref, cbc_ref)
    bm_bf = bm.astype(BF16)
    cm_bf = cm.astype(BF16)

    sel = sel_ref[0]
    expand = exp_ref[...]
    dt_raw = _dot3_right(dt_ref[0], sel)
    dt = jax.nn.softplus(dt_raw + dtb_ref[0])
    a_neg = -jnp.exp(alog_ref[0])
    tril = _tril(L)
    ac = _dot3_left(tril.astype(BF16), dt * a_neg)
    a_last = ac[L - 1:L, :]

    def expand2(v):
        hi = v.astype(BF16)
        lo = (v - hi.astype(F32)).astype(BF16)
        return _dot(hi, expand) + _dot(lo, expand)

    dt_x = expand2(dt)
    eac_x = expand2(jnp.exp(ac))
    wd_x = expand2(jnp.exp(a_last - ac) * dt)
    row8 = lax.broadcasted_iota(I32, (8, LANES), 0)
    tail_x = expand2(jnp.where(row8 == 0, jnp.exp(a_last), jnp.where(row8 == 1, dpar_ref[0], 0.0)))
    decay_x = tail_x[0:1, :]
    d_x = tail_x[1:2, :]

    xdt = (xs * dt_x).astype(BF16)
    cbm = jnp.where(tril, _dot_nt(cm_bf, bm_bf), 0.0)
    ac2 = ac * LOG2E
    ac2_t = ac2.T
    lane = lax.broadcasted_iota(I32, (L, LANES), 1)
    zero = jnp.zeros((), BF16)
    ys = []
    for pair in range(SSM_R // 2):
        xq = xdt[:, pair * LANES:(pair + 1) * LANES]
        acc = None
        for half in range(2):
            r = 2 * pair + half
            seg = jnp.exp2(jnp.minimum(ac2[:, r:r + 1] - ac2_t[r:r + 1, :], 0.0))
            w = (cbm * seg).astype(BF16)
            keep = (lane < SSM_P) if half == 0 else (lane >= SSM_P)
            part = _dot(w, jnp.where(keep, xq, zero))
            acc = part if acc is None else acc + part
        ys.append(acc)
    h_old = h_ref[...]
    y = jnp.concatenate(ys, axis=1) + _dot(cm_bf, h_old.astype(BF16)) * eac_x
    xw = (xs * wd_x).astype(BF16)
    h_ref[...] = decay_x * h_old + _dot_tn(bm_bf, xw)

    y = (y + d_x * xs) * _silu(z_ref[0].astype(F32))
    y = y * lax.rsqrt(jnp.mean(y * y, axis=-1, keepdims=True) + EPS) * nw_ref[...]
    y_ref[0] = y.astype(BF16)

    @pl.when(t_idx == pl.num_programs(2) - 1)
    def _():
        h_out_ref[0, 0] = h_ref[...]


def _ssd(proj, proj_f, conv_w, conv_b, conv_buf8, dt_bias, a_log, ssm_d, ssm_norm_w, h0t, chunk):
    b, t, _ = proj.shape
    L = chunk
    grp = lambda v: jnp.pad(v.reshape(SSM_G, 1, SSM_R), ((0, 0), (0, 0), (0, LANES - SSM_R)))
    hh = jnp.arange(LANES)
    sel = (hh[None, :, None] == (jnp.arange(SSM_G)[:, None, None] * SSM_R + hh[None, None, :])) & (hh[None, None, :] < SSM_R)
    expand = (hh[:, None] == (jnp.arange(SSM_GW)[None, :] // SSM_P))
    tpos = jnp.arange(L)
    shift = jnp.stack([tpos[None, :] == tpos[:, None] - (3 - j) for j in range(3)])
    cb2 = conv_b.reshape(1, SSM_XBC)
    nb_b, nb_c = SSM_INNER // LANES, (SSM_INNER + SSM_G * SSM_N) // LANES
    pcol = lambda base, width: (lambda i, g, j: (i, j, base // width + g))
    return pl.pallas_call(
        _ssd_kernel,
        grid=(b, SSM_G, t // L),
        in_specs=[pl.BlockSpec((1, L, SSM_GW), pcol(COL_XS, SSM_GW)),
                  pl.BlockSpec((1, L, SSM_N), pcol(COL_B, SSM_N)),
                  pl.BlockSpec((1, L, SSM_N), pcol(COL_C, SSM_N)),
                  pl.BlockSpec((1, L, SSM_GW), pcol(COL_Z, SSM_GW)),
                  pl.BlockSpec((1, L, LANES), lambda i, g, j: (i, j, COL_DT // LANES)),
                  pl.BlockSpec((4, SSM_GW), lambda i, g, j: (0, g)),
                  pl.BlockSpec((4, SSM_N), lambda i, g, j: (0, nb_b + g)),
                  pl.BlockSpec((4, SSM_N), lambda i, g, j: (0, nb_c + g)),
                  pl.BlockSpec((1, SSM_GW), lambda i, g, j: (0, g)),
                  pl.BlockSpec((1, SSM_N), lambda i, g, j: (0, nb_b + g)),
                  pl.BlockSpec((1, SSM_N), lambda i, g, j: (0, nb_c + g)),
                  pl.BlockSpec((1, 8, SSM_GW), lambda i, g, j: (i, 0, g)),
                  pl.BlockSpec((1, 8, SSM_N), lambda i, g, j: (i, 0, nb_b + g)),
                  pl.BlockSpec((1, 8, SSM_N), lambda i, g, j: (i, 0, nb_c + g)),
                  pl.BlockSpec((1, 1, LANES), lambda i, g, j: (g, 0, 0)),
                  pl.BlockSpec((1, 1, LANES), lambda i, g, j: (g, 0, 0)),
                  pl.BlockSpec((1, 1, LANES), lambda i, g, j: (g, 0, 0)),
                  pl.BlockSpec((1, LANES, LANES), lambda i, g, j: (g, 0, 0)),
                  pl.BlockSpec((LANES, SSM_GW), lambda i, g, j: (0, 0)),
                  pl.BlockSpec((3, L, L), lambda i, g, j: (0, 0, 0)),
                  pl.BlockSpec((1, SSM_GW), lambda i, g, j: (0, g)),
                  pl.BlockSpec((1, 1, SSM_N, SSM_GW), lambda i, g, j: (i, g, 0, 0))],
        out_specs=[pl.BlockSpec((1, L, SSM_GW), lambda i, g, j: (i, j, g)),
                   pl.BlockSpec((1, 1, SSM_N, SSM_GW), lambda i, g, j: (i, g, 0, 0))],
        out_shape=[jax.ShapeDtypeStruct((b, t, SSM_INNER), BF16),
                   jax.ShapeDtypeStruct((b, SSM_G, SSM_N, SSM_GW), F32)],
        scratch_shapes=[pltpu.VMEM((SSM_N, SSM_GW), F32),
                        pltpu.VMEM((L + 8, SSM_GW), F32),
                        pltpu.VMEM((L + 8, SSM_N), F32),
                        pltpu.VMEM((L + 8, SSM_N), F32)],
        compiler_params=_cparams(("arbitrary", "arbitrary", "arbitrary")),
        name="ssd",
    )(proj, proj, proj, proj, proj_f, conv_w, conv_w, conv_w, cb2, cb2, cb2, conv_buf8, conv_buf8, conv_buf8,
      grp(dt_bias), grp(a_log), grp(ssm_d), sel.astype(BF16), expand.astype(BF16), shift.astype(BF16),
      ssm_norm_w.reshape(1, SSM_INNER), h0t)


def _merge_kernel(o_ref, y_ref, ga_ref, gb_ref, x_ref, g1_ref, sh2_ref, sc2_ref, nw_ref, wa_ref, wb_ref, wo_ref,
                  x1_ref, h2_ref):
    bb, tt, _ = x_ref.shape
    n = bb * tt
    pa = _dot(o_ref[...].reshape(n, D), wa_ref[...])
    pb = _dot(y_ref[...].reshape(n, SSM_INNER), wb_ref[...])
    ga = ga_ref[...].reshape(n, D).astype(F32)
    gb = gb_ref[...].reshape(n, D).astype(F32)
    merged = jax.nn.sigmoid(ga) * pa + jax.nn.sigmoid(gb) * pb
    mix = _dot(merged.astype(BF16), wo_ref[...]).reshape(bb, tt, D)
    x1 = x_ref[...] + g1_ref[...] * mix
    x1_ref[...] = x1
    xn = x1 * lax.rsqrt(jnp.mean(x1 * x1, axis=-1, keepdims=True) + EPS)
    h2_ref[...] = (xn * nw_ref[...] * (1.0 + sc2_ref[...]) + sh2_ref[...]).astype(BF16)


def _merge(o_g, y_g, proj, x, mod3, norm2_w, wa, wb, wo, bb, tt):
    b, t, _ = x.shape
    row = lambda i, j: (i, j, 0)
    modc = lambda c: (lambda i, j: (i, 0, c))
    const = lambda i, j: (0, 0)
    return pl.pallas_call(
        _merge_kernel,
        grid=(b // bb, t // tt),
        in_specs=[pl.BlockSpec((bb, tt, D), row),
                  pl.BlockSpec((bb, tt, SSM_INNER), row),
                  pl.BlockSpec((bb, tt, D), lambda i, j: (i, j, COL_GA // D)),
                  pl.BlockSpec((bb, tt, D), lambda i, j: (i, j, COL_GB // D)),
                  pl.BlockSpec((bb, tt, D), row),
                  pl.BlockSpec((bb, 1, D), modc(2)),
                  pl.BlockSpec((bb, 1, D), modc(3)),
                  pl.BlockSpec((bb, 1, D), modc(4)),
                  pl.BlockSpec((1, D), const),
                  pl.BlockSpec((D, D), const),
                  pl.BlockSpec((SSM_INNER, D), const),
                  pl.BlockSpec((D, D), const)],
        out_specs=[pl.BlockSpec((bb, tt, D), row), pl.BlockSpec((bb, tt, D), row)],
        out_shape=[jax.ShapeDtypeStruct((b, t, D), F32), jax.ShapeDtypeStruct((b, t, D), BF16)],
        compiler_params=_cparams(("arbitrary", "arbitrary")),
        name="merge",
    )(o_g, y_g, proj, proj, x, mod3, mod3, mod3, norm2_w.reshape(1, D), wa, wb, wo)


_NO_ROW = float(1 << 20)


def _tree(op, xs):
    xs = list(xs)
    while len(xs) > 1:
        xs = [op(xs[i], xs[i + 1]) for i in range(0, len(xs) - 1, 2)] + ([xs[-1]] if len(xs) % 2 else [])
    return xs[0]


def _sort_pairs(n):
    pairs = []
    p = 1
    while p < n:
        k = p
        while k >= 1:
            for j in range(k % p, n - k, 2 * k):
                for i in range(min(k, n - j - k)):
                    if (i + j) // (2 * p) == (i + j + k) // (2 * p):
                        pairs.append((i + j, i + j + k))
            k //= 2
        p *= 2
    return pairs


def _first(a, b):
    (va, ia), (vb, ib) = a, b
    eq = va == vb
    return jnp.where(eq, ib, va) > jnp.where(eq, ia, vb)


def _cex(a, b):
    take_a = _first(a, b)
    return ((jnp.maximum(a[0], b[0]), jnp.where(take_a, a[1], b[1])),
            (jnp.minimum(a[0], b[0]), jnp.where(take_a, b[1], a[1])))


def _merge_top(a, b):
    k = len(a)
    c = []
    for i in range(k):
        x, y = a[i], b[k - 1 - i]
        c.append((jnp.maximum(x[0], y[0]), jnp.where(_first(x, y), x[1], y[1])))
    s = k // 2
    while s >= 1:
        for i in range(k):
            if (i & s) == 0:
                c[i], c[i + s] = _cex(c[i], c[i + s])
        s //= 2
    return c


def _topk_sorted(vals, ids, k, presorted=()):
    pairs = _sort_pairs(k)
    items = list(zip(vals, ids))
    blocks = list(presorted)
    for i in range(0, len(items), k):
        blk = items[i:i + k]
        for lo, hi in pairs:
            blk[lo], blk[hi] = _cex(blk[lo], blk[hi])
        blocks.append(blk)
    while len(blocks) > 1:
        nxt = [_merge_top(blocks[i], blocks[i + 1]) for i in range(0, len(blocks) - 1, 2)]
        blocks = nxt + ([blocks[-1]] if len(blocks) % 2 else [])
    return [v for v, _ in blocks[0]], [i for _, i in blocks[0]]


def _pick_list(table, sel):
    out = table[0]
    for i in range(1, len(table)):
        out = jnp.where(sel == float(i), table[i], out)
    return out


def _pair_candidates(v1, v2):
    kk = PEER_TOPK
    first_row = [(v1[0] + v2[j], float(j)) for j in range(kk)]
    vals, ids = [], []
    for i in range(1, kk):
        for j in range(kk):
            if (i + 1) * (j + 1) <= kk:
                vals.append(v1[i] + v2[j])
                ids.append(float(i * kk + j))
    pad = (-len(vals)) % kk
    vals += [jnp.full_like(v1[0], -jnp.inf)] * pad
    ids += [_NO_ROW + float(p) for p in range(pad)]
    return first_row, vals, ids


ROUTE_PITCH = LANES + 8


def _route_kernel(h2_ref, wq_ref, k1_ref, k2_ref, e1_ref, e2_ref, gw_ref, q_ref, s1_ref, s2_ref):
    tm = h2_ref.shape[0]
    groups = tm // LANES
    half = PEER_DK // 2
    q = _dot(h2_ref[...], wq_ref[...])
    for c in range(2 * PEER_HEADS):
        q_ref[c] = q[:, c * half:(c + 1) * half].astype(BF16)
    key_ids = [float(k) for k in range(PEER_NK)]

    def per_key_scores(keys, qh, s_ref):
        for g in range(groups):
            s_ref[g * ROUTE_PITCH:g * ROUTE_PITCH + PEER_NK, :] = _dot_nt(keys, qh[g * LANES:(g + 1) * LANES, :])
        return [s_ref[pl.ds(k, groups, stride=ROUTE_PITCH), :] for k in range(PEER_NK)]

    def body(h, carry):
        v1, i1 = _topk_sorted(per_key_scores(k1_ref[h], q_ref[2 * h], s1_ref), key_ids, PEER_TOPK)
        v2, i2 = _topk_sorted(per_key_scores(k2_ref[h], q_ref[2 * h + 1], s2_ref), key_ids, PEER_TOPK)
        first_row, cand, flat = _pair_candidates(v1, v2)
        sv, sf = _topk_sorted(cand, flat, PEER_TOPK, presorted=[first_row])
        p = [jnp.exp(v - sv[0]) for v in sv]
        inv = 1.0 / _tree(jnp.add, p)
        for k in range(PEER_TOPK):
            sel_i = jnp.floor(sf[k] * (1.0 / PEER_TOPK))
            sel_j = sf[k] - sel_i * float(PEER_TOPK)
            slot = pl.ds(h * PEER_TOPK + k, 1)
            e1_ref[:, slot, :] = _pick_list(i1, sel_i).astype(I32).reshape(groups, 1, LANES)
            e2_ref[:, slot, :] = _pick_list(i2, sel_j).astype(I32).reshape(groups, 1, LANES)
            gw_ref[:, slot, :] = (p[k] * inv).reshape(groups, 1, LANES)
        return carry

    lax.fori_loop(0, PEER_HEADS, body, 0)


def _route(h2, wq, k1, k2, tm):
    n = h2.shape[0]
    groups = tm // LANES
    out = jax.ShapeDtypeStruct((n // LANES, PEER_HK, LANES), I32)
    ospec = pl.BlockSpec((groups, PEER_HK, LANES), lambda i: (i, 0, 0))
    const3 = lambda i: (0, 0, 0)
    return pl.pallas_call(
        _route_kernel,
        grid=(n // tm,),
        in_specs=[pl.BlockSpec((tm, D), lambda i: (i, 0)),
                  pl.BlockSpec((D, PEER_HEADS * PEER_DK), lambda i: (0, 0)),
                  pl.BlockSpec((PEER_HEADS, PEER_NK, PEER_DK // 2), const3),
                  pl.BlockSpec((PEER_HEADS, PEER_NK, PEER_DK // 2), const3)],
        out_specs=[ospec, ospec, ospec],
        out_shape=[out, out, jax.ShapeDtypeStruct((n // LANES, PEER_HK, LANES), F32)],
        scratch_shapes=[pltpu.VMEM((2 * PEER_HEADS, tm, PEER_DK // 2), BF16),
                        pltpu.VMEM((groups * ROUTE_PITCH, LANES), F32),
                        pltpu.VMEM((groups * ROUTE_PITCH, LANES), F32)],
        compiler_params=_cparams(("arbitrary",)),
        name="route",
    )(h2, wq, k1, k2)


EXPERT_AB = 16
W_PAD = 8
SCATTER_UNROLL = 128
HI16 = -65536


def _experts_kernel(h2_ref, e1t_ref, e2t_ref, gwt_ref, u_ref, v_ref, x1_ref, g2_ref, fw_ref, y_ref,
                    e1_ref, e2_ref, gw_ref, gact_ref, coef_ref, w_ref, acc_ref, act_ref):
    tm = h2_ref.shape[0]
    phase = pl.program_id(2)
    j = pl.program_id(3)
    nj = pl.num_programs(3)

    def select_from_act(step):
        e1 = e1_ref[...]
        e2 = e2_ref[...]
        g = jnp.zeros((tm, PEER_HK), F32)
        for p in range(EXPERT_AB // 2):
            both = jnp.take_along_axis(act_ref[:, p * LANES:(p + 1) * LANES], e2, axis=1,
                                       mode="promise_in_bounds")
            first = lax.bitcast_convert_type(both & HI16, F32)
            second = lax.bitcast_convert_type(lax.shift_left(both, 16), F32)
            a = step * EXPERT_AB + 2 * p
            g = jnp.where(e1 == a, first, g)
            g = jnp.where(e1 == a + 1, second, g)
        return g

    def pack_pairs(act):
        out = []
        for p in range(EXPERT_AB // 2):
            first = act[:, (2 * p) * LANES:(2 * p + 1) * LANES].astype(BF16).astype(F32)
            second = act[:, (2 * p + 1) * LANES:(2 * p + 2) * LANES].astype(BF16).astype(F32)
            out.append(lax.bitcast_convert_type(first, I32)
                       | lax.shift_right_logical(lax.bitcast_convert_type(second, I32), 16))
        return jnp.concatenate(out, axis=1)

    @pl.when((phase == 0) & (j == 0))
    def _():
        for r in range(tm // LANES):
            rows = slice(r * LANES, (r + 1) * LANES)
            e1_ref[rows, :] = e1t_ref[r].astype(F32).T.astype(I32)
            e2_ref[rows, :] = e2t_ref[r].astype(F32).T.astype(I32)
            gw_ref[rows, :] = gwt_ref[r].T
        gact_ref[...] = jnp.zeros_like(gact_ref)
        act_ref[...] = jnp.zeros_like(act_ref)

    @pl.when(phase == 0)
    def _():
        gact_ref[...] += select_from_act(j - 1)
        act_ref[...] = pack_pairs(_dot_nt(h2_ref[...], u_ref[...]))

    @pl.when((phase == 1) & (j == 0))
    def _():
        ga = gact_ref[...] + select_from_act(nj - 1)
        gelu = 0.5 * ga * (1.0 + lax.erf(ga * (0.5 ** 0.5)))
        coef_ref[...] = gw_ref[...] * gelu
        acc_ref[...] = jnp.zeros_like(acc_ref)

    hp = tm // 2
    pitch = hp + W_PAD

    @pl.when((phase == 1) & (j == 0))
    def _():
        rows = lax.broadcasted_iota(I32, (PEER_NK, PEER_HK), 0)

        def weights_bits(t):
            ct = jnp.where(rows == e1_ref[pl.ds(t, 1), :], coef_ref[pl.ds(t, 1), :], 0.0).astype(BF16)
            bt = (rows == e2_ref[pl.ds(t, 1), :]).astype(BF16)
            return lax.bitcast_convert_type(_dot_nt(ct, bt).astype(BF16).astype(F32), I32)

        def scatter(t, carry):
            words = weights_bits(t) | lax.shift_right_logical(weights_bits(t + hp), 16)
            w_ref[pl.ds(t, PEER_NK, stride=pitch), :] = words
            return carry

        lax.fori_loop(0, hp, scatter, 0, unroll=SCATTER_UNROLL // 2)

    @pl.when(phase == 1)
    def _():
        parts = []
        for a in range(EXPERT_AB):
            words = w_ref[pl.ds(pl.multiple_of((j * EXPERT_AB + a) * pitch, 8), hp), :]
            first = lax.bitcast_convert_type(words & HI16, F32).astype(BF16)
            second = lax.bitcast_convert_type(lax.shift_left(words, 16), F32).astype(BF16)
            parts.append(jnp.concatenate([first, second], axis=0))
        acc_ref[...] += _dot(jnp.concatenate(parts, axis=1), v_ref[...])

    @pl.when((phase == 1) & (j == nj - 1))
    def _():
        bb, tt, _ = x1_ref.shape
        x2 = x1_ref[...] + g2_ref[...] * acc_ref[...].reshape(bb, tt, D)
        y_ref[...] = x2 * lax.rsqrt(jnp.mean(x2 * x2, axis=-1, keepdims=True) + EPS) * fw_ref[...]


def _experts(h2, e1t, e2t, gwt, u, v, x1, mod3, final_w, bb, tt):
    b, t, _ = x1.shape
    tm = bb * tt
    nj = PEER_NK // EXPERT_AB
    eb = EXPERT_AB * PEER_NK
    nt = t // tt
    tspec = pl.BlockSpec((tm // LANES, PEER_HK, LANES), lambda i, k, p, j: (i * nt + k, 0, 0))
    rspec = pl.BlockSpec((bb, tt, D), lambda i, k, p, j: (i, k, 0))
    return pl.pallas_call(
        _experts_kernel,
        grid=(b // bb, nt, 2, nj),
        in_specs=[pl.BlockSpec((tm, D), lambda i, k, p, j: (i * nt + k, 0)), tspec, tspec, tspec,
                  pl.BlockSpec((eb, D), lambda i, k, p, j: (jnp.where(p == 0, j, nj - 1), 0)),
                  pl.BlockSpec((eb, D), lambda i, k, p, j: (jnp.where(p == 1, j, 0), 0)),
                  rspec,
                  pl.BlockSpec((bb, 1, D), lambda i, k, p, j: (i, 0, 5)),
                  pl.BlockSpec((1, D), lambda i, k, p, j: (0, 0))],
        out_specs=rspec,
        out_shape=jax.ShapeDtypeStruct((b, t, D), F32),
        scratch_shapes=[pltpu.VMEM((tm, PEER_HK), I32),
                        pltpu.VMEM((tm, PEER_HK), I32),
                        pltpu.VMEM((tm, PEER_HK), F32),
                        pltpu.VMEM((tm, PEER_HK), F32),
                        pltpu.VMEM((tm, PEER_HK), F32),
                        pltpu.VMEM((PEER_NK * (tm // 2 + W_PAD), PEER_NK), I32),
                        pltpu.VMEM((tm, D), F32),
                        pltpu.VMEM((tm, EXPERT_AB * LANES // 2), I32)],
        compiler_params=_cparams(("arbitrary", "arbitrary", "arbitrary", "arbitrary")),
        name="experts",
    )(h2, e1t, e2t, gwt, u, v, x1, mod3, final_w.reshape(1, D))


TILE_INPROJ = 1024
TILE_GLA = 1024
TILE_MERGE = 512
TILE_ROUTE = 1024
TILE_EXPERTS = 512
LONG_SEQ = 256


def _tile(total, cap):
    t = min(total, cap)
    assert total % t == 0
    return t


def _row_tile(b, t, cap):
    return (1, _tile(t, cap)) if t >= LONG_SEQ else (b, t)


def _group(x, mod, s_hg, s_ssm, conv_buf, lb, w, gla_chunk, ssd_chunk):
    b, t, _ = x.shape
    n = b * t
    mod3 = mod.reshape(b, 1, N_MOD * D)
    bb, tt = _row_tile(b, t, TILE_INPROJ)
    proj, proj_f = _inproj(x, mod3, w["norm1_w"], w["w_in"], bb, tt)

    o_g, s_hg_new = _gla(proj, proj_f, lb, w["hgrn_norm_w"], s_hg, _tile(t, TILE_GLA), gla_chunk)

    conv_buf8 = jnp.pad(conv_buf, ((0, 0), (5, 0), (0, 0)))
    h0t = jnp.swapaxes(s_ssm.reshape(b, SSM_G, SSM_R * SSM_P, SSM_N), 2, 3)
    y_g, h_t = _ssd(proj, proj_f, w["conv_w"], w["conv_b"], conv_buf8, w["dt_bias"], w["a_log"], w["ssm_d"],
                    w["ssm_norm_w"], h0t, ssd_chunk)
    s_ssm_new = jnp.swapaxes(h_t, 2, 3).reshape(b, SSM_HEADS, SSM_P, SSM_N)
    assert t >= 3
    conv_new = proj[:, t - 3:, COL_XS:COL_XS + SSM_XBC].astype(F32)

    bb2, tt2 = _row_tile(b, t, TILE_MERGE)
    x1, h2 = _merge(o_g, y_g, proj, x, mod3, w["norm2_w"], w["w_branch_a"], w["w_branch_b"], w["w_out"], bb2, tt2)

    h2f = h2.reshape(n, D)
    e1t, e2t, gwt = _route(h2f, w["peer_wq"], w["peer_keys1"], w["peer_keys2"], _tile(n, TILE_ROUTE))
    bb3, tt3 = _row_tile(b, t, TILE_EXPERTS)
    y = _experts(h2f, e1t, e2t, gwt, w["peer_u"], w["peer_v"], x1, mod3, w["final_norm_w"], bb3, tt3)
    return y, s_hg_new, s_ssm_new, conv_new


def kernel(x_prompt, x_sample, c_prompt, c_sample, state_hgrn, state_ssm, state_conv, w_ada, b_ada, norm1_w, w_in,
           hgrn_lower_bounds, hgrn_norm_w, conv_w, conv_b, dt_bias, a_log, ssm_d, ssm_norm_w, w_branch_a,
           w_branch_b, w_out, norm2_w, peer_wq, peer_keys1, peer_keys2, peer_u, peer_v, final_norm_w):
    depth = w_ada.shape[0]
    assert depth == 1
    bp, tp, _ = x_prompt.shape
    bs, ts, _ = x_sample.shape
    lb_all = jnp.cumsum(jax.nn.softmax(hgrn_lower_bounds.astype(F32), axis=0), axis=0)

    l = 0
    wi = w_in[l]
    c_f, c_i, c_dt, c_ga = 1024, 2048, 9216, 9248
    w_pad = jnp.concatenate(
        [wi[:, :c_f], wi[:, c_i:c_dt], wi[:, c_ga:],
         wi[:, c_f:c_i], wi[:, c_dt:c_ga],
         jnp.zeros((D, PROJ_F32_COLS - (c_i - c_f) - SSM_HEADS), wi.dtype)], axis=1).astype(BF16)
    assert w_pad.shape[1] == PROJ_COLS
    w = dict(norm1_w=norm1_w[l], w_in=w_pad, hgrn_norm_w=hgrn_norm_w[l], conv_w=conv_w[l], conv_b=conv_b[l],
             dt_bias=dt_bias[l], a_log=a_log[l], ssm_d=ssm_d[l], ssm_norm_w=ssm_norm_w[l],
             w_branch_a=w_branch_a[l].astype(BF16), w_branch_b=w_branch_b[l].astype(BF16),
             w_out=w_out[l].astype(BF16), norm2_w=norm2_w[l], peer_wq=peer_wq[l].astype(BF16),
             peer_keys1=peer_keys1[l].astype(BF16), peer_keys2=peer_keys2[l].astype(BF16),
             peer_u=peer_u[l].astype(BF16), peer_v=peer_v[l].astype(BF16), final_norm_w=final_norm_w)

    mod = _adaln(jnp.concatenate([c_prompt, c_sample], axis=0), w_ada[l], b_ada[l])

    z_hg = jnp.zeros((bp, HG_HEADS, HG_DK, LANES), F32)
    z_ssm = jnp.zeros((bp, SSM_HEADS, SSM_P, SSM_N), F32)
    z_conv = jnp.zeros((bp, 3, SSM_XBC), F32)
    yp, hg_p, ssm_p, conv_p = _group(x_prompt, mod[:bp], z_hg, z_ssm, z_conv, lb_all[l], w,
                                     min(GLA_CHUNK, tp), min(SSD_CHUNK, tp))
    ys, hg_s, ssm_s, conv_s = _group(x_sample, mod[bp:], state_hgrn[l], state_ssm[l], state_conv[l], lb_all[l], w,
                                     min(GLA_CHUNK, ts), min(SSD_CHUNK, ts))
    return (yp, ys, hg_p[None], ssm_p[None], conv_p[None], hg_s[None], ssm_s[None], conv_s[None])
```

```python
import functools

import jax
import jax.numpy as jnp
from jax import lax
from jax.experimental import pallas as pl
from jax.experimental.pallas import tpu as pltpu

F32 = jnp.float32
BF16 = jnp.bfloat16
I32 = jnp.int32

EPS = 1e-6
LOG2E = 1.4426950408889634
D = 1024
N_MOD = 6
HG_HEADS = 8
HG_DK = 128
SSM_HEADS = 32
SSM_P = 64
SSM_G = 4
SSM_R = SSM_HEADS // SSM_G
SSM_N = 128
SSM_INNER = SSM_HEADS * SSM_P
SSM_GW = SSM_INNER // SSM_G
SSM_XBC = SSM_INNER + 2 * SSM_G * SSM_N
PEER_HEADS = 8
PEER_NK = 128
PEER_TOPK = 16
PEER_HK = PEER_HEADS * PEER_TOPK
PEER_DK = 256

LANES = 128
GLA_CHUNK = 64
SSD_CHUNK = 256
VMEM_LIMIT = 56 * 1024 * 1024

COL_Q = 0
COL_I = 1024
COL_G = 2048
COL_Z = 3072
COL_XS = 5120
COL_B = COL_XS + SSM_INNER
COL_C = COL_B + SSM_G * SSM_N
COL_GA = 8192
COL_GB = 9216
PROJ_BF16_COLS = 10240
COL_F = 0
COL_DT = 1024
PROJ_F32_COLS = 1280
PROJ_TN = 1280
PROJ_COLS = PROJ_BF16_COLS + PROJ_F32_COLS


def _cparams(sem):
    return pltpu.CompilerParams(dimension_semantics=sem, vmem_limit_bytes=VMEM_LIMIT)


def _silu(x):
    return x * jax.nn.sigmoid(x)


def _split3(x):
    hi = x.astype(BF16)
    r = x - hi.astype(F32)
    mid = r.astype(BF16)
    lo = (r - mid.astype(F32)).astype(BF16)
    return hi, mid, lo


def _dot(a, b):
    return jnp.dot(a, b, preferred_element_type=F32)


def _dot_nt(a, b):
    return lax.dot_general(a, b, (((1,), (1,)), ((), ())), preferred_element_type=F32)


def _dot_tn(a, b):
    return lax.dot_general(a, b, (((0,), (0,)), ((), ())), preferred_element_type=F32)


def _dot3_left(m_bf16, x):
    hi, mid, lo = _split3(x)
    return _dot(m_bf16, hi) + _dot(m_bf16, mid) + _dot(m_bf16, lo)


def _dot3_right(x, m_bf16):
    hi, mid, lo = _split3(x)
    return _dot(hi, m_bf16) + _dot(mid, m_bf16) + _dot(lo, m_bf16)


def _chunk_cumsum(x, chunk):
    pos = lax.broadcasted_iota(I32, x.shape, 0) % chunk
    s = 1
    while s < chunk:
        x = x + jnp.where(pos >= s, pltpu.roll(x, shift=s, axis=0), 0.0)
        s *= 2
    return x


def _tril(n):
    r = lax.broadcasted_iota(I32, (n, n), 0)
    c = lax.broadcasted_iota(I32, (n, n), 1)
    return r >= c


def _adaln_kernel(c_ref, w_ref, b_ref, o_ref):
    c = c_ref[...]
    o_ref[...] = _dot(_silu(c).astype(BF16), w_ref[...].astype(BF16)) + b_ref[...]


def _adaln(c, w_ada, b_ada):
    nb = c.shape[0]
    ncol = w_ada.shape[1]
    return pl.pallas_call(
        _adaln_kernel,
        grid=(ncol // D,),
        in_specs=[pl.BlockSpec((nb, D), lambda j: (0, 0)),
                  pl.BlockSpec((D, D), lambda j: (0, j)),
                  pl.BlockSpec((1, D), lambda j: (0, j))],
        out_specs=pl.BlockSpec((nb, D), lambda j: (0, j)),
        out_shape=jax.ShapeDtypeStruct((nb, ncol), F32),
        compiler_params=_cparams(("arbitrary",)),
        name="adaln",
    )(c, w_ada, b_ada.reshape(1, ncol))


def _inproj_kernel(x_ref, sh_ref, sc_ref, nw_ref, w_ref, ob_ref, of_ref, h_ref):
    bb, tt, _ = x_ref.shape
    n = pl.program_id(2)
    n_bf16 = PROJ_BF16_COLS // PROJ_TN

    @pl.when(n == 0)
    def _():
        x = x_ref[...]
        xn = x * lax.rsqrt(jnp.mean(x * x, axis=-1, keepdims=True) + EPS)
        h = xn * nw_ref[...] * (1.0 + sc_ref[...]) + sh_ref[...]
        h_ref[...] = h.reshape(bb * tt, D).astype(BF16)

    res = _dot(h_ref[...], w_ref[...]).reshape(bb, tt, -1)

    @pl.when(n < n_bf16)
    def _():
        ob_ref[...] = res.astype(BF16)

    @pl.when(n == n_bf16)
    def _():
        of_ref[...] = res


def _inproj(x, mod3, norm_w, w_pad, bb, tt):
    b, t, _ = x.shape
    n_bf16 = PROJ_BF16_COLS // PROJ_TN
    assert PROJ_F32_COLS == PROJ_TN
    return pl.pallas_call(
        _inproj_kernel,
        grid=(b // bb, t // tt, PROJ_COLS // PROJ_TN),
        in_specs=[pl.BlockSpec((bb, tt, D), lambda i, j, n: (i, j, 0)),
                  pl.BlockSpec((bb, 1, D), lambda i, j, n: (i, 0, 0)),
                  pl.BlockSpec((bb, 1, D), lambda i, j, n: (i, 0, 1)),
                  pl.BlockSpec((1, D), lambda i, j, n: (0, 0)),
                  pl.BlockSpec((D, PROJ_TN), lambda i, j, n: (0, n))],
        out_specs=[pl.BlockSpec((bb, tt, PROJ_TN), lambda i, j, n: (i, j, jnp.minimum(n, n_bf16 - 1))),
                   pl.BlockSpec((bb, tt, PROJ_TN), lambda i, j, n: (i, j, 0))],
        out_shape=[jax.ShapeDtypeStruct((b, t, PROJ_BF16_COLS), BF16),
                   jax.ShapeDtypeStruct((b, t, PROJ_F32_COLS), F32)],
        scratch_shapes=[pltpu.VMEM((bb * tt, D), BF16)],
        compiler_params=_cparams(("arbitrary", "arbitrary", "arbitrary")),
        name="inproj",
    )(x, mod3, mod3, norm_w.reshape(1, D), w_pad)


def _gla_kernel(q_ref, f_ref, i_ref, g_ref, lb_ref, nw_ref, s0_ref, o_ref, s_out_ref, s_ref, *, chunk):
    tt = q_ref.shape[1]
    nchunk = tt // chunk
    t_idx = pl.program_id(2)

    @pl.when(t_idx == 0)
    def _():
        s_ref[...] = s0_ref[0, 0].T

    lb = lb_ref[...]
    tril = _tril(chunk)
    mid = chunk // 2

    q_all = _silu(q_ref[0].astype(F32))
    ff = f_ref[0]
    logf = jnp.log(lb + (1.0 - lb) * jax.nn.sigmoid(ff))
    k_all = (1.0 - lb) * jax.nn.sigmoid(-ff)
    v_all = i_ref[0]
    g_all = _chunk_cumsum(logf, chunk)

    st = s_ref[...]
    outs = []
    for c in range(nchunk):
        sl = slice(c * chunk, (c + 1) * chunk)
        q, k, v, g = q_all[sl], k_all[sl], v_all[sl], g_all[sl]
        g_mid = g[mid:mid + 1, :]
        g_last = g[chunk - 1:chunk, :]
        qs = (q * jnp.exp(g - g_mid)).astype(BF16)
        ks = (k * jnp.exp(g_mid - g)).astype(BF16)
        att = jnp.where(tril, _dot_nt(qs, ks), 0.0).astype(BF16)
        outs.append(_dot(att, v) + _dot_nt((q * jnp.exp(g)).astype(BF16), st.astype(BF16)))
        kd = (k * jnp.exp(g_last - g)).astype(BF16)
        st = jnp.exp(g_last) * st + _dot_tn(v, kd)
    s_ref[...] = st
    o = jnp.concatenate(outs, axis=0) if nchunk > 1 else outs[0]
    on = o * lax.rsqrt(jnp.mean(o * o, axis=-1, keepdims=True) + EPS) * nw_ref[...]
    o_ref[0] = (on * _silu(g_ref[0].astype(F32))).astype(BF16)

    @pl.when(t_idx == pl.num_programs(2) - 1)
    def _():
        s_out_ref[0, 0] = s_ref[...].T


def _gla(proj, proj_f, lb, hg_norm_w, s0, tt, chunk):
    b, t, _ = proj.shape
    col = lambda base: (lambda i, h, j: (i, j, base // LANES + h))
    return pl.pallas_call(
        functools.partial(_gla_kernel, chunk=chunk),
        grid=(b, HG_HEADS, t // tt),
        in_specs=[pl.BlockSpec((1, tt, LANES), col(COL_Q)),
                  pl.BlockSpec((1, tt, LANES), col(COL_F)),
                  pl.BlockSpec((1, tt, LANES), col(COL_I)),
                  pl.BlockSpec((1, tt, LANES), col(COL_G)),
                  pl.BlockSpec((1, LANES), lambda i, h, j: (0, h)),
                  pl.BlockSpec((1, LANES), lambda i, h, j: (0, h)),
                  pl.BlockSpec((1, 1, HG_DK, LANES), lambda i, h, j: (i, h, 0, 0))],
        out_specs=[pl.BlockSpec((1, tt, LANES), lambda i, h, j: (i, j, h)),
                   pl.BlockSpec((1, 1, HG_DK, LANES), lambda i, h, j: (i, h, 0, 0))],
        out_shape=[jax.ShapeDtypeStruct((b, t, D), BF16),
                   jax.ShapeDtypeStruct((b, HG_HEADS, HG_DK, LANES), F32)],
        scratch_shapes=[pltpu.VMEM((HG_DK, LANES), F32)],
        compiler_params=_cparams(("arbitrary", "arbitrary", "arbitrary")),
        name="gla",
    )(proj, proj_f, proj, proj, lb.reshape(1, D), hg_norm_w.reshape(1, D), s0)


def _ssd_kernel(xs_ref, b_ref, c_ref, z_ref, dt_ref, cwx_ref, cwb_ref, cwc_ref, cbx_ref, cbb_ref, cbc_ref,
                bufx_ref, bufb_ref, bufc_ref, dtb_ref, alog_ref, dpar_ref, sel_ref, exp_ref, shift_ref, nw_ref, h0_ref,
                y_ref, h_out_ref, h_ref, xpx_ref, xpb_ref, xpc_ref):
    L = xs_ref.shape[1]
    t_idx = pl.program_id(2)

    @pl.when(t_idx == 0)
    def _():
        h_ref[...] = h0_ref[0, 0]
        xpx_ref[0:8, :] = bufx_ref[0]
        xpb_ref[0:8, :] = bufb_ref[0]
        xpc_ref[0:8, :] = bufc_ref[0]

    def conv(xp_ref, raw, w_ref, bias_ref):
        cur = raw.astype(F32)
        xp_ref[8:16, :] = cur[0:8, :]
        acc = bias_ref[...] + cur * w_ref[3:4, :]
        for j in range(3):
            delayed = _dot(shift_ref[j], raw)
            head = xp_ref[5 + j:13 + j, :]
            if L > 8:
                delayed = jnp.concatenate([head, delayed[8:, :]], axis=0)
            else:
                delayed = head
            acc = acc + delayed * w_ref[j:j + 1, :]
        xp_ref[0:8, :] = cur[L - 8:L, :]
        return _silu(acc)

    xs = conv(xpx_ref, xs_ref[0], cwx_ref, cbx_ref)
    bm = conv(xpb_ref, b_ref[0], cwb_ref, cbb_ref)
    cm = conv(xpc_ref, c_ref[0], cwc_ref, cbc_ref)
    bm_bf = bm.astype(BF16)
    cm_bf = cm.astype(BF16)

    sel = sel_ref[0]
    expand = exp_ref[...]
    dt_raw = _dot3_right(dt_ref[0], sel)
    dt = jax.nn.softplus(dt_raw + dtb_ref[0])
    a_neg = -jnp.exp(alog_ref[0])
    tril = _tril(L)
    ac = _dot3_left(tril.astype(BF16), dt * a_neg)
    a_last = ac[L - 1:L, :]

    def expand2(v):
        hi = v.astype(BF16)
        lo = (v - hi.astype(F32)).astype(BF16)
        return _dot(hi, expand) + _dot(lo, expand)

    dt_x = expand2(dt)
    eac_x = expand2(jnp.exp(ac))
    wd_x = expand2(jnp.exp(a_last - ac) * dt)
    row8 = lax.broadcasted_iota(I32, (8, LANES), 0)
    tail_x = expand2(jnp.where(row8 == 0, jnp.exp(a_last), jnp.where(row8 == 1, dpar_ref[0], 0.0)))
    decay_x = tail_x[0:1, :]
    d_x = tail_x[1:2, :]

    xdt = (xs * dt_x).astype(BF16)
    cbm = jnp.where(tril, _dot_nt(cm_bf, bm_bf), 0.0)
    ac2 = ac * LOG2E
    ac2_t = ac2.T
    lane = lax.broadcasted_iota(I32, (L, LANES), 1)
    zero = jnp.zeros((), BF16)
    ys = []
    for pair in range(SSM_R // 2):
        xq = xdt[:, pair * LANES:(pair + 1) * LANES]
        acc = None
        for half in range(2):
            r = 2 * pair + half
            seg = jnp.exp2(jnp.minimum(ac2[:, r:r + 1] - ac2_t[r:r + 1, :], 0.0))
            w = (cbm * seg).astype(BF16)
            keep = (lane < SSM_P) if half == 0 else (lane >= SSM_P)
            part = _dot(w, jnp.where(keep, xq, zero))
            acc = part if acc is None else acc + part
        ys.append(acc)
    h_old = h_ref[...]
    y = jnp.concatenate(ys, axis=1) + _dot(cm_bf, h_old.astype(BF16)) * eac_x
    xw = (xs * wd_x).astype(BF16)
    h_ref[...] = decay_x * h_old + _dot_tn(bm_bf, xw)

    y = (y + d_x * xs) * _silu(z_ref[0].astype(F32))
    y = y * lax.rsqrt(jnp.mean(y * y, axis=-1, keepdims=True) + EPS) * nw_ref[...]
    y_ref[0] = y.astype(BF16)

    @pl.when(t_idx == pl.num_programs(2) - 1)
    def _():
        h_out_ref[0, 0] = h_ref[...]


def _ssd(proj, proj_f, conv_w, conv_b, conv_buf8, dt_bias, a_log, ssm_d, ssm_norm_w, h0t, chunk):
    b, t, _ = proj.shape
    L = chunk
    grp = lambda v: jnp.pad(v.reshape(SSM_G, 1, SSM_R), ((0, 0), (0, 0), (0, LANES - SSM_R)))
    hh = jnp.arange(LANES)
    sel = (hh[None, :, None] == (jnp.arange(SSM_G)[:, None, None] * SSM_R + hh[None, None, :])) & (hh[None, None, :] < SSM_R)
    expand = (hh[:, None] == (jnp.arange(SSM_GW)[None, :] // SSM_P))
    tpos = jnp.arange(L)
    shift = jnp.stack([tpos[None, :] == tpos[:, None] - (3 - j) for j in range(3)])
    cb2 = conv_b.reshape(1, SSM_XBC)
    nb_b, nb_c = SSM_INNER // LANES, (SSM_INNER + SSM_G * SSM_N) // LANES
    pcol = lambda base, width: (lambda i, g, j: (i, j, base // width + g))
    return pl.pallas_call(
        _ssd_kernel,
        grid=(b, SSM_G, t // L),
        in_specs=[pl.BlockSpec((1, L, SSM_GW), pcol(COL_XS, SSM_GW)),
                  pl.BlockSpec((1, L, SSM_N), pcol(COL_B, SSM_N)),
                  pl.BlockSpec((1, L, SSM_N), pcol(COL_C, SSM_N)),
                  pl.BlockSpec((1, L, SSM_GW), pcol(COL_Z, SSM_GW)),
                  pl.BlockSpec((1, L, LANES), lambda i, g, j: (i, j, COL_DT // LANES)),
                  pl.BlockSpec((4, SSM_GW), lambda i, g, j: (0, g)),
                  pl.BlockSpec((4, SSM_N), lambda i, g, j: (0, nb_b + g)),
                  pl.BlockSpec((4, SSM_N), lambda i, g, j: (0, nb_c + g)),
                  pl.BlockSpec((1, SSM_GW), lambda i, g, j: (0, g)),
                  pl.BlockSpec((1, SSM_N), lambda i, g, j: (0, nb_b + g)),
                  pl.BlockSpec((1, SSM_N), lambda i, g, j: (0, nb_c + g)),
                  pl.BlockSpec((1, 8, SSM_GW), lambda i, g, j: (i, 0, g)),
                  pl.BlockSpec((1, 8, SSM_N), lambda i, g, j: (i, 0, nb_b + g)),
                  pl.BlockSpec((1, 8, SSM_N), lambda i, g, j: (i, 0, nb_c + g)),
                  pl.BlockSpec((1, 1, LANES), lambda i, g, j: (g, 0, 0)),
                  pl.BlockSpec((1, 1, LANES), lambda i, g, j: (g, 0, 0)),
                  pl.BlockSpec((1, 1, LANES), lambda i, g, j: (g, 0, 0)),
                  pl.BlockSpec((1, LANES, LANES), lambda i, g, j: (g, 0, 0)),
                  pl.BlockSpec((LANES, SSM_GW), lambda i, g, j: (0, 0)),
                  pl.BlockSpec((3, L, L), lambda i, g, j: (0, 0, 0)),
                  pl.BlockSpec((1, SSM_GW), lambda i, g, j: (0, g)),
                  pl.BlockSpec((1, 1, SSM_N, SSM_GW), lambda i, g, j: (i, g, 0, 0))],
        out_specs=[pl.BlockSpec((1, L, SSM_GW), lambda i, g, j: (i, j, g)),
                   pl.BlockSpec((1, 1, SSM_N, SSM_GW), lambda i, g, j: (i, g, 0, 0))],
        out_shape=[jax.ShapeDtypeStruct((b, t, SSM_INNER), BF16),
                   jax.ShapeDtypeStruct((b, SSM_G, SSM_N, SSM_GW), F32)],
        scratch_shapes=[pltpu.VMEM((SSM_N, SSM_GW), F32),
                        pltpu.VMEM((L + 8, SSM_GW), F32),
                        pltpu.VMEM((L + 8, SSM_N), F32),
                        pltpu.VMEM((L + 8, SSM_N), F32)],
        compiler_params=_cparams(("arbitrary", "arbitrary", "arbitrary")),
        name="ssd",
    )(proj, proj, proj, proj, proj_f, conv_w, conv_w, conv_w, cb2, cb2, cb2, conv_buf8, conv_buf8, conv_buf8,
      grp(dt_bias), grp(a_log), grp(ssm_d), sel.astype(BF16), expand.astype(BF16), shift.astype(BF16),
      ssm_norm_w.reshape(1, SSM_INNER), h0t)


def _merge_kernel(o_ref, y_ref, ga_ref, gb_ref, x_ref, g1_ref, sh2_ref, sc2_ref, nw_ref, wa_ref, wb_ref, wo_ref,
                  x1_ref, h2_ref):
    bb, tt, _ = x_ref.shape
    n = bb * tt
    pa = _dot(o_ref[...].reshape(n, D), wa_ref[...])
    pb = _dot(y_ref[...].reshape(n, SSM_INNER), wb_ref[...])
    ga = ga_ref[...].reshape(n, D).astype(F32)
    gb = gb_ref[...].reshape(n, D).astype(F32)
    merged = jax.nn.sigmoid(ga) * pa + jax.nn.sigmoid(gb) * pb
    mix = _dot(merged.astype(BF16), wo_ref[...]).reshape(bb, tt, D)
    x1 = x_ref[...] + g1_ref[...] * mix
    x1_ref[...] = x1
    xn = x1 * lax.rsqrt(jnp.mean(x1 * x1, axis=-1, keepdims=True) + EPS)
    h2_ref[...] = (xn * nw_ref[...] * (1.0 + sc2_ref[...]) + sh2_ref[...]).astype(BF16)


def _merge(o_g, y_g, proj, x, mod3, norm2_w, wa, wb, wo, bb, tt):
    b, t, _ = x.shape
    row = lambda i, j: (i, j, 0)
    modc = lambda c: (lambda i, j: (i, 0, c))
    const = lambda i, j: (0, 0)
    return pl.pallas_call(
        _merge_kernel,
        grid=(b // bb, t // tt),
        in_specs=[pl.BlockSpec((bb, tt, D), row),
                  pl.BlockSpec((bb, tt, SSM_INNER), row),
                  pl.BlockSpec((bb, tt, D), lambda i, j: (i, j, COL_GA // D)),
                  pl.BlockSpec((bb, tt, D), lambda i, j: (i, j, COL_GB // D)),
                  pl.BlockSpec((bb, tt, D), row),
                  pl.BlockSpec((bb, 1, D), modc(2)),
                  pl.BlockSpec((bb, 1, D), modc(3)),
                  pl.BlockSpec((bb, 1, D), modc(4)),
                  pl.BlockSpec((1, D), const),
                  pl.BlockSpec((D, D), const),
                  pl.BlockSpec((SSM_INNER, D), const),
                  pl.BlockSpec((D, D), const)],
        out_specs=[pl.BlockSpec((bb, tt, D), row), pl.BlockSpec((bb, tt, D), row)],
        out_shape=[jax.ShapeDtypeStruct((b, t, D), F32), jax.ShapeDtypeStruct((b, t, D), BF16)],
        compiler_params=_cparams(("arbitrary", "arbitrary")),
        name="merge",
    )(o_g, y_g, proj, proj, x, mod3, mod3, mod3, norm2_w.reshape(1, D), wa, wb, wo)


_NO_ROW = float(1 << 20)


def _tree(op, xs):
    xs = list(xs)
    while len(xs) > 1:
        xs = [op(xs[i], xs[i + 1]) for i in range(0, len(xs) - 1, 2)] + ([xs[-1]] if len(xs) % 2 else [])
    return xs[0]


def _sort_pairs(n):
    pairs = []
    p = 1
    while p < n:
        k = p
        while k >= 1:
            for j in range(k % p, n - k, 2 * k):
                for i in range(min(k, n - j - k)):
                    if (i + j) // (2 * p) == (i + j + k) // (2 * p):
                        pairs.append((i + j, i + j + k))
            k //= 2
        p *= 2
    return pairs


def _first(a, b):
    (va, ia), (vb, ib) = a, b
    eq = va == vb
    return jnp.where(eq, ib, va) > jnp.where(eq, ia, vb)


def _cex(a, b):
    take_a = _first(a, b)
    return ((jnp.maximum(a[0], b[0]), jnp.where(take_a, a[1], b[1])),
            (jnp.minimum(a[0], b[0]), jnp.where(take_a, b[1], a[1])))


def _merge_top(a, b):
    k = len(a)
    c = []
    for i in range(k):
        x, y = a[i], b[k - 1 - i]
        c.append((jnp.maximum(x[0], y[0]), jnp.where(_first(x, y), x[1], y[1])))
    s = k // 2
    while s >= 1:
        for i in range(k):
            if (i & s) == 0:
                c[i], c[i + s] = _cex(c[i], c[i + s])
        s //= 2
    return c


def _topk_sorted(vals, ids, k, presorted=()):
    pairs = _sort_pairs(k)
    items = list(zip(vals, ids))
    blocks = list(presorted)
    for i in range(0, len(items), k):
        blk = items[i:i + k]
        for lo, hi in pairs:
            blk[lo], blk[hi] = _cex(blk[lo], blk[hi])
        blocks.append(blk)
    while len(blocks) > 1:
        nxt = [_merge_top(blocks[i], blocks[i + 1]) for i in range(0, len(blocks) - 1, 2)]
        blocks = nxt + ([blocks[-1]] if len(blocks) % 2 else [])
    return [v for v, _ in blocks[0]], [i for _, i in blocks[0]]


def _pick_list(table, sel):
    out = table[0]
    for i in range(1, len(table)):
        out = jnp.where(sel == float(i), table[i], out)
    return out


def _pair_candidates(v1, v2):
    kk = PEER_TOPK
    first_row = [(v1[0] + v2[j], float(j)) for j in range(kk)]
    vals, ids = [], []
    for i in range(1, kk):
        for j in range(kk):
            if (i + 1) * (j + 1) <= kk:
                vals.append(v1[i] + v2[j])
                ids.append(float(i * kk + j))
    pad = (-len(vals)) % kk
    vals += [jnp.full_like(v1[0], -jnp.inf)] * pad
    ids += [_NO_ROW + float(p) for p in range(pad)]
    return first_row, vals, ids


ROUTE_PITCH = LANES + 8


def _route_kernel(h2_ref, wq_ref, k1_ref, k2_ref, e1_ref, e2_ref, gw_ref, q_ref, s1_ref, s2_ref):
    tm = h2_ref.shape[0]
    groups = tm // LANES
    half = PEER_DK // 2
    q = _dot(h2_ref[...], wq_ref[...])
    for c in range(2 * PEER_HEADS):
        q_ref[c] = q[:, c * half:(c + 1) * half].astype(BF16)
    key_ids = [float(k) for k in range(PEER_NK)]

    def per_key_scores(keys, qh, s_ref):
        for g in range(groups):
            s_ref[g * ROUTE_PITCH:g * ROUTE_PITCH + PEER_NK, :] = _dot_nt(keys, qh[g * LANES:(g + 1) * LANES, :])
        return [s_ref[pl.ds(k, groups, stride=ROUTE_PITCH), :] for k in range(PEER_NK)]

    def body(h, carry):
        v1, i1 = _topk_sorted(per_key_scores(k1_ref[h], q_ref[2 * h], s1_ref), key_ids, PEER_TOPK)
        v2, i2 = _topk_sorted(per_key_scores(k2_ref[h], q_ref[2 * h + 1], s2_ref), key_ids, PEER_TOPK)
        first_row, cand, flat = _pair_candidates(v1, v2)
        sv, sf = _topk_sorted(cand, flat, PEER_TOPK, presorted=[first_row])
        p = [jnp.exp(v - sv[0]) for v in sv]
        inv = 1.0 / _tree(jnp.add, p)
        for k in range(PEER_TOPK):
            sel_i = jnp.floor(sf[k] * (1.0 / PEER_TOPK))
            sel_j = sf[k] - sel_i * float(PEER_TOPK)
            slot = pl.ds(h * PEER_TOPK + k, 1)
            e1_ref[:, slot, :] = _pick_list(i1, sel_i).astype(I32).reshape(groups, 1, LANES)
            e2_ref[:, slot, :] = _pick_list(i2, sel_j).astype(I32).reshape(groups, 1, LANES)
            gw_ref[:, slot, :] = (p[k] * inv).reshape(groups, 1, LANES)
        return carry

    lax.fori_loop(0, PEER_HEADS, body, 0)


def _route(h2, wq, k1, k2, tm):
    n = h2.shape[0]
    groups = tm // LANES
    out = jax.ShapeDtypeStruct((n // LANES, PEER_HK, LANES), I32)
    ospec = pl.BlockSpec((groups, PEER_HK, LANES), lambda i: (i, 0, 0))
    const3 = lambda i: (0, 0, 0)
    return pl.pallas_call(
        _route_kernel,
        grid=(n // tm,),
        in_specs=[pl.BlockSpec((tm, D), lambda i: (i, 0)),
                  pl.BlockSpec((D, PEER_HEADS * PEER_DK), lambda i: (0, 0)),
                  pl.BlockSpec((PEER_HEADS, PEER_NK, PEER_DK // 2), const3),
                  pl.BlockSpec((PEER_HEADS, PEER_NK, PEER_DK // 2), const3)],
        out_specs=[ospec, ospec, ospec],
        out_shape=[out, out, jax.ShapeDtypeStruct((n // LANES, PEER_HK, LANES), F32)],
        scratch_shapes=[pltpu.VMEM((2 * PEER_HEADS, tm, PEER_DK // 2), BF16),
                        pltpu.VMEM((groups * ROUTE_PITCH, LANES), F32),
                        pltpu.VMEM((groups * ROUTE_PITCH, LANES), F32)],
        compiler_params=_cparams(("arbitrary",)),
        name="route",
    )(h2, wq, k1, k2)


EXPERT_AB = 16
W_PAD = 8
SCATTER_UNROLL = 128
HI16 = -65536


def _experts_kernel(h2_ref, e1t_ref, e2t_ref, gwt_ref, u_ref, v_ref, x1_ref, g2_ref, fw_ref, y_ref,
                    e1_ref, e2_ref, gw_ref, gact_ref, coef_ref, w_ref, acc_ref, act_ref):
    tm = h2_ref.shape[0]
    phase = pl.program_id(2)
    j = pl.program_id(3)
    nj = pl.num_programs(3)

    def select_from_act(step):
        e1 = e1_ref[...]
        e2 = e2_ref[...]
        g = jnp.zeros((tm, PEER_HK), F32)
        for p in range(EXPERT_AB // 2):
            both = jnp.take_along_axis(act_ref[:, p * LANES:(p + 1) * LANES], e2, axis=1,
                                       mode="promise_in_bounds")
            first = lax.bitcast_convert_type(both & HI16, F32)
            second = lax.bitcast_convert_type(lax.shift_left(both, 16), F32)
            a = step * EXPERT_AB + 2 * p
            g = jnp.where(e1 == a, first, g)
            g = jnp.where(e1 == a + 1, second, g)
        return g

    def pack_pairs(act):
        out = []
        for p in range(EXPERT_AB // 2):
            first = act[:, (2 * p) * LANES:(2 * p + 1) * LANES].astype(BF16).astype(F32)
            second = act[:, (2 * p + 1) * LANES:(2 * p + 2) * LANES].astype(BF16).astype(F32)
            out.append(lax.bitcast_convert_type(first, I32)
                       | lax.shift_right_logical(lax.bitcast_convert_type(second, I32), 16))
        return jnp.concatenate(out, axis=1)

    @pl.when((phase == 0) & (j == 0))
    def _():
        for r in range(tm // LANES):
            rows = slice(r * LANES, (r + 1) * LANES)
            e1_ref[rows, :] = e1t_ref[r].astype(F32).T.astype(I32)
            e2_ref[rows, :] = e2t_ref[r].astype(F32).T.astype(I32)
            gw_ref[rows, :] = gwt_ref[r].T
        gact_ref[...] = jnp.zeros_like(gact_ref)
        act_ref[...] = jnp.zeros_like(act_ref)

    @pl.when(phase == 0)
    def _():
        gact_ref[...] += select_from_act(j - 1)
        act_ref[...] = pack_pairs(_dot_nt(h2_ref[...], u_ref[...]))

    @pl.when((phase == 1) & (j == 0))
    def _():
        ga = gact_ref[...] + select_from_act(nj - 1)
        gelu = 0.5 * ga * (1.0 + lax.erf(ga * (0.5 ** 0.5)))
        coef_ref[...] = gw_ref[...] * gelu
        acc_ref[...] = jnp.zeros_like(acc_ref)

    hp = tm // 2
    pitch = hp + W_PAD

    @pl.when((phase == 1) & (j == 0))
    def _():
        rows = lax.broadcasted_iota(I32, (PEER_NK, PEER_HK), 0)

        def weights_bits(t):
            ct = jnp.where(rows == e1_ref[pl.ds(t, 1), :], coef_ref[pl.ds(t, 1), :], 0.0).astype(BF16)
            bt = (rows == e2_ref[pl.ds(t, 1), :]).astype(BF16)
            return lax.bitcast_convert_type(_dot_nt(ct, bt).astype(BF16).astype(F32), I32)

        def scatter(t, carry):
            words = weights_bits(t) | lax.shift_right_logical(weights_bits(t + hp), 16)
            w_ref[pl.ds(t, PEER_NK, stride=pitch), :] = words
            return carry

        lax.fori_loop(0, hp, scatter, 0, unroll=SCATTER_UNROLL // 2)

    @pl.when(phase == 1)
    def _():
        parts = []
        for a in range(EXPERT_AB):
            words = w_ref[pl.ds(pl.multiple_of((j * EXPERT_AB + a) * pitch, 8), hp), :]
            first = lax.bitcast_convert_type(words & HI16, F32).astype(BF16)
            second = lax.bitcast_convert_type(lax.shift_left(words, 16), F32).astype(BF16)
            parts.append(jnp.concatenate([first, second], axis=0))
        acc_ref[...] += _dot(jnp.concatenate(parts, axis=1), v_ref[...])

    @pl.when((phase == 1) & (j == nj - 1))
    def _():
        bb, tt, _ = x1_ref.shape
        x2 = x1_ref[...] + g2_ref[...] * acc_ref[...].reshape(bb, tt, D)
        y_ref[...] = x2 * lax.rsqrt(jnp.mean(x2 * x2, axis=-1, keepdims=True) + EPS) * fw_ref[...]


def _experts(h2, e1t, e2t, gwt, u, v, x1, mod3, final_w, bb, tt):
    b, t, _ = x1.shape
    tm = bb * tt
    nj = PEER_NK // EXPERT_AB
    eb = EXPERT_AB * PEER_NK
    nt = t // tt
    tspec = pl.BlockSpec((tm // LANES, PEER_HK, LANES), lambda i, k, p, j: (i * nt + k, 0, 0))
    rspec = pl.BlockSpec((bb, tt, D), lambda i, k, p, j: (i, k, 0))
    return pl.pallas_call(
        _experts_kernel,
        grid=(b // bb, nt, 2, nj),
        in_specs=[pl.BlockSpec((tm, D), lambda i, k, p, j: (i * nt + k, 0)), tspec, tspec, tspec,
                  pl.BlockSpec((eb, D), lambda i, k, p, j: (jnp.where(p == 0, j, nj - 1), 0)),
                  pl.BlockSpec((eb, D), lambda i, k, p, j: (jnp.where(p == 1, j, 0), 0)),
                  rspec,
                  pl.BlockSpec((bb, 1, D), lambda i, k, p, j: (i, 0, 5)),
                  pl.BlockSpec((1, D), lambda i, k, p, j: (0, 0))],
        out_specs=rspec,
        out_shape=jax.ShapeDtypeStruct((b, t, D), F32),
        scratch_shapes=[pltpu.VMEM((tm, PEER_HK), I32),
                        pltpu.VMEM((tm, PEER_HK), I32),
                        pltpu.VMEM((tm, PEER_HK), F32),
                        pltpu.VMEM((tm, PEER_HK), F32),
                        pltpu.VMEM((tm, PEER_HK), F32),
                        pltpu.VMEM((PEER_NK * (tm // 2 + W_PAD), PEER_NK), I32),
                        pltpu.VMEM((tm, D), F32),
                        pltpu.VMEM((tm, EXPERT_AB * LANES // 2), I32)],
        compiler_params=_cparams(("arbitrary", "arbitrary", "arbitrary", "arbitrary")),
        name="experts",
    )(h2, e1t, e2t, gwt, u, v, x1, mod3, final_w.reshape(1, D))


TILE_INPROJ = 1024
TILE_GLA = 2048
TILE_MERGE = 512
TILE_ROUTE = 1024
TILE_EXPERTS = 512
LONG_SEQ = 256


def _tile(total, cap):
    t = min(total, cap)
    assert total % t == 0
    return t


def _row_tile(b, t, cap):
    return (1, _tile(t, cap)) if t >= LONG_SEQ else (b, t)


def _group(x, mod, s_hg, s_ssm, conv_buf, lb, w, gla_chunk, ssd_chunk):
    b, t, _ = x.shape
    n = b * t
    mod3 = mod.reshape(b, 1, N_MOD * D)
    bb, tt = _row_tile(b, t, TILE_INPROJ)
    proj, proj_f = _inproj(x, mod3, w["norm1_w"], w["w_in"], bb, tt)

    o_g, s_hg_new = _gla(proj, proj_f, lb, w["hgrn_norm_w"], s_hg, _tile(t, TILE_GLA), gla_chunk)

    conv_buf8 = jnp.pad(conv_buf, ((0, 0), (5, 0), (0, 0)))
    h0t = jnp.swapaxes(s_ssm.reshape(b, SSM_G, SSM_R * SSM_P, SSM_N), 2, 3)
    y_g, h_t = _ssd(proj, proj_f, w["conv_w"], w["conv_b"], conv_buf8, w["dt_bias"], w["a_log"], w["ssm_d"],
                    w["ssm_norm_w"], h0t, ssd_chunk)
    s_ssm_new = jnp.swapaxes(h_t, 2, 3).reshape(b, SSM_HEADS, SSM_P, SSM_N)
    assert t >= 3
    conv_new = proj[:, t - 3:, COL_XS:COL_XS + SSM_XBC].astype(F32)

    bb2, tt2 = _row_tile(b, t, TILE_MERGE)
    x1, h2 = _merge(o_g, y_g, proj, x, mod3, w["norm2_w"], w["w_branch_a"], w["w_branch_b"], w["w_out"], bb2, tt2)

    h2f = h2.reshape(n, D)
    e1t, e2t, gwt = _route(h2f, w["peer_wq"], w["peer_keys1"], w["peer_keys2"], _tile(n, TILE_ROUTE))
    bb3, tt3 = _row_tile(b, t, TILE_EXPERTS)
    y = _experts(h2f, e1t, e2t, gwt, w["peer_u"], w["peer_v"], x1, mod3, w["final_norm_w"], bb3, tt3)
    return y, s_hg_new, s_ssm_new, conv_new


def kernel(x_prompt, x_sample, c_prompt, c_sample, state_hgrn, state_ssm, state_conv, w_ada, b_ada, norm1_w, w_in,
           hgrn_lower_bounds, hgrn_norm_w, conv_w, conv_b, dt_bias, a_log, ssm_d, ssm_norm_w, w_branch_a,
           w_branch_b, w_out, norm2_w, peer_wq, peer_keys1, peer_keys2, peer_u, peer_v, final_norm_w):
    depth = w_ada.shape[0]
    assert depth == 1
    bp, tp, _ = x_prompt.shape
    bs, ts, _ = x_sample.shape
    lb_all = jnp.cumsum(jax.nn.softmax(hgrn_lower_bounds.astype(F32), axis=0), axis=0)

    l = 0
    wi = w_in[l]
    c_f, c_i, c_dt, c_ga = 1024, 2048, 9216, 9248
    w_pad = jnp.concatenate(
        [wi[:, :c_f], wi[:, c_i:c_dt], wi[:, c_ga:],
         wi[:, c_f:c_i], wi[:, c_dt:c_ga],
         jnp.zeros((D, PROJ_F32_COLS - (c_i - c_f) - SSM_HEADS), wi.dtype)], axis=1).astype(BF16)
    assert w_pad.shape[1] == PROJ_COLS
    w = dict(norm1_w=norm1_w[l], w_in=w_pad, hgrn_norm_w=hgrn_norm_w[l], conv_w=conv_w[l], conv_b=conv_b[l],
             dt_bias=dt_bias[l], a_log=a_log[l], ssm_d=ssm_d[l], ssm_norm_w=ssm_norm_w[l],
             w_branch_a=w_branch_a[l].astype(BF16), w_branch_b=w_branch_b[l].astype(BF16),
             w_out=w_out[l].astype(BF16), norm2_w=norm2_w[l], peer_wq=peer_wq[l].astype(BF16),
             peer_keys1=peer_keys1[l].astype(BF16), peer_keys2=peer_keys2[l].astype(BF16),
             peer_u=peer_u[l].astype(BF16), peer_v=peer_v[l].astype(BF16), final_norm_w=final_norm_w)

    mod = _adaln(jnp.concatenate([c_prompt, c_sample], axis=0), w_ada[l], b_ada[l])

    z_hg = jnp.zeros((bp, HG_HEADS, HG_DK, LANES), F32)
    z_ssm = jnp.zeros((bp, SSM_HEADS, SSM_P, SSM_N), F32)
    z_conv = jnp.zeros((bp, 3, SSM_XBC), F32)
    yp, hg_p, ssm_p, conv_p = _group(x_prompt, mod[:bp], z_hg, z_ssm, z_conv, lb_all[l], w,
                                     min(GLA_CHUNK, tp), min(SSD_CHUNK, tp))
    ys, hg_s, ssm_s, conv_s = _group(x_sample, mod[bp:], state_hgrn[l], state_ssm[l], state_conv[l], lb_all[l], w,
                                     min(GLA_CHUNK, ts), min(SSD_CHUNK, ts))
    return (yp, ys, hg_p[None], ssm_p[None], conv_p[None], hg_s[None], ssm_s[None], conv_s[None])
```
